```python
import math
import jax, jax.numpy as jnp
from jax import lax
import numpy as np

D_MODEL = 1024
BATCH = 16
SEQ = 2048
DEPTH = 2
DEC_BATCH = 32
DEC_SEQ = 8
PAST_LEN = 16384
PAGE_SIZE = 128

N_HEADS = 8
HEAD_DIM = 64
N_KV = 2
HPG = N_HEADS // N_KV
D_ATT = N_HEADS * HEAD_DIM
KV_W = N_KV * HEAD_DIM
CMP_BLOCK = 32
CMP_STRIDE = 16
CMP_HIDDEN = 128
SLC_BLOCK = 64
N_SEL = 16
WINDOW = 512
Q_CHUNK = 32
S5_GROUP = 16
S5_STATE = 64
D_SSM = 512
N_SSM_GROUPS = D_SSM // S5_GROUP
N_BUCKETS = 32
MAX_DISTANCE = 128
ALPHA = (2 * DEPTH) ** 0.25
BETA = (8 * DEPTH) ** -0.25
LN_EPS = 1e-5
IN_WIDTHS = (D_ATT, KV_W, KV_W, KV_W, KV_W, KV_W, KV_W, 3 * N_HEADS, D_ATT, D_SSM, D_SSM, 2 * D_MODEL)
D_IN = sum(IN_WIDTHS)

kernel_name = 'nsa_s5_parallel_deepnorm_step'


def split_in(h):
    offs, acc = [], 0
    for w in IN_WIDTHS[:-1]:
        acc += w
        offs.append(acc)
    return jnp.split(h, offs, axis=-1)


def heads(t, n):
    return t.reshape(t.shape[:-1] + (n, HEAD_DIM))


def stack_kv(k, v):
    return jnp.stack([heads(k, N_KV), heads(v, N_KV)], axis=2)


def head_gates(g):
    return jax.nn.sigmoid(g.astype(jnp.float32)).reshape(g.shape[:-1] + (N_HEADS, 3))


def layer_norm(x, g, b):
    xf = x.astype(jnp.float32)
    mu = xf.mean(-1, keepdims=True)
    var = jnp.square(xf - mu).mean(-1, keepdims=True)
    return ((xf - mu) * lax.rsqrt(var + LN_EPS) * g + b).astype(x.dtype)


def rel_bucket(dist):
    n = jnp.maximum(dist, 0)
    max_exact = N_BUCKETS // 2
    nf = jnp.maximum(n, 1).astype(jnp.float32)
    large = max_exact + (jnp.log(nf / max_exact) / math.log(MAX_DISTANCE / max_exact)
                         * (N_BUCKETS - max_exact)).astype(jnp.int32)
    return jnp.where(n < max_exact, n, jnp.minimum(large, N_BUCKETS - 1))


def masked_softmax(logits, mask):
    l = jnp.where(mask, logits.astype(jnp.float32), -jnp.inf)
    m = jnp.max(l, axis=-1, keepdims=True)
    m = jnp.where(jnp.isfinite(m), m, 0.0)
    e = jnp.where(mask, jnp.exp(l - m), 0.0)
    s = e.sum(-1, keepdims=True)
    return e / jnp.where(s > 0, s, 1.0)


def compress(rows, w1, b1, w2, b2, pos):
    B, N = rows.shape[:2]
    nch = N // CMP_STRIDE
    ch = rows[:, :nch * CMP_STRIDE].reshape(B, nch, CMP_STRIDE, N_KV, HEAD_DIM).astype(jnp.float32)
    h_first = jnp.einsum('bcpgd,pdh->bcgh', ch, w1[:CMP_STRIDE])
    h_second = jnp.einsum('bcpgd,pdh->bcgh', ch, w1[CMP_STRIDE:])
    h = h_first[:, :-1] + h_second[:, 1:] + (jnp.einsum('pd,pdh->h', pos, w1) + b1)
    out = jnp.einsum('bcgh,hd->bcgd', jax.nn.silu(h), w2) + b2
    ends = jnp.arange(nch - 1, dtype=jnp.int32) * CMP_STRIDE + (CMP_BLOCK - 1)
    return out, ends


def to_blocks(rows):
    B, N = rows.shape[:2]
    return rows.reshape(B, N // SLC_BLOCK, SLC_BLOCK, N_KV, HEAD_DIM).transpose(0, 3, 1, 2, 4)


def nsa_attend(q, gates, q_pos, kc, vc, c_end, ks_bg, vs_bg, kw, vw, w_pos, rel_table):
    B, T = q.shape[:2]
    n_cmp = kc.shape[1]
    n_blk = ks_bg.shape[2]
    qg = (q.astype(jnp.float32) * HEAD_DIM ** -0.5).reshape(B, T, N_KV, HPG, HEAD_DIM)
    tq = q_pos[:, None]

    dist_c = tq - c_end[None, :]
    bias_c = rel_table[rel_bucket(dist_c)].reshape(T, n_cmp, N_KV, HPG).transpose(0, 2, 3, 1)
    lc = jnp.einsum('btgrd,bngd->btgrn', qg, kc) + bias_c
    pc = masked_softmax(lc, (dist_c >= 0)[:, None, None, :])
    o_c = jnp.einsum('btgrn,bngd->btgrd', pc, vc)

    ratio = SLC_BLOCK // CMP_STRIDE
    imp = jnp.pad(pc.sum(axis=3), ((0, 0), (0, 0), (0, 0), (1, ratio * n_blk - n_cmp)))
    shp = imp.shape[:-1] + (n_blk, ratio)
    imp_blk = imp[..., :ratio * n_blk].reshape(shp).sum(-1) + imp[..., 1:ratio * n_blk + 1].reshape(shp).sum(-1)
    blk = jnp.arange(n_blk, dtype=jnp.int32)[None, :]
    cur = (q_pos // SLC_BLOCK)[:, None]
    forced = (blk == 0) | (blk == cur) | (blk == cur - 1)
    visible = blk * SLC_BLOCK <= tq
    score = jnp.where(forced[:, None, :], jnp.inf, jnp.where(visible[:, None, :], imp_blk, -jnp.inf))
    _, idx = lax.top_k(score, min(N_SEL, n_blk))
    n_sel = idx.shape[-1]

    b_i = jnp.arange(B)[:, None, None, None]
    g_i = jnp.arange(N_KV)[None, None, :, None]
    k_sel = ks_bg[b_i, g_i, idx]
    v_sel = vs_bg[b_i, g_i, idx]
    s_pos = idx[..., None] * SLC_BLOCK + jnp.arange(SLC_BLOCK, dtype=jnp.int32)
    dist_s = q_pos[None, :, None, None, None] - s_pos
    bias_s = rel_table.reshape(N_BUCKETS, N_KV, HPG)[rel_bucket(dist_s), g_i[..., None]]
    ls = jnp.einsum('btgrd,btgnld->btgrnl', qg, k_sel) + jnp.moveaxis(bias_s, -1, 3)
    ps = masked_softmax(ls.reshape(B, T, N_KV, HPG, n_sel * SLC_BLOCK),
                        (dist_s >= 0).reshape(B, T, N_KV, 1, n_sel * SLC_BLOCK))
    o_s = jnp.einsum('btgrm,btgmd->btgrd', ps, v_sel.reshape(B, T, N_KV, n_sel * SLC_BLOCK, HEAD_DIM))

    dist_w = tq - w_pos[None, :]
    bias_w = rel_table[rel_bucket(dist_w)].reshape(T, -1, N_KV, HPG).transpose(0, 2, 3, 1)
    lw = jnp.einsum('btgrd,bngd->btgrn', qg, kw) + bias_w
    mw = (dist_w >= 0) & (dist_w <= WINDOW) & (w_pos[None, :] >= 0)
    pw = masked_softmax(lw, mw[:, None, None, :])
    o_w = jnp.einsum('btgrn,bngd->btgrd', pw, vw)

    g = gates.reshape(B, T, N_KV, HPG, 3)
    o = g[..., 0:1] * o_c + g[..., 1:2] * o_s + g[..., 2:3] * o_w
    return o.reshape(B, T, D_ATT)


def nsa_prompt(q, gates, kv_c, kv_s, kv_w, cmp_k, cmp_v, rel_table):
    B, S = q.shape[:2]
    kc, c_end = compress(kv_c[:, :, 0], *cmp_k)
    vc, _ = compress(kv_c[:, :, 1], *cmp_v)
    ks_bg, vs_bg = to_blocks(kv_s[:, :, 0]), to_blocks(kv_s[:, :, 1])
    pad = ((0, 0), (WINDOW, 0), (0, 0), (0, 0))
    kw_pad, vw_pad = jnp.pad(kv_w[:, :, 0], pad), jnp.pad(kv_w[:, :, 1], pad)

    def one_block(c0):
        qc = lax.dynamic_slice_in_dim(q, c0, Q_CHUNK, axis=1)
        gc = lax.dynamic_slice_in_dim(gates, c0, Q_CHUNK, axis=1)
        kw = lax.dynamic_slice_in_dim(kw_pad, c0, WINDOW + Q_CHUNK, axis=1)
        vw = lax.dynamic_slice_in_dim(vw_pad, c0, WINDOW + Q_CHUNK, axis=1)
        q_pos = c0 + jnp.arange(Q_CHUNK, dtype=jnp.int32)
        w_pos = c0 - WINDOW + jnp.arange(WINDOW + Q_CHUNK, dtype=jnp.int32)
        return nsa_attend(qc, gc, q_pos, kc, vc, c_end, ks_bg, vs_bg, kw, vw, w_pos, rel_table)

    out = lax.map(one_block, jnp.arange(0, S, Q_CHUNK, dtype=jnp.int32))
    return out.transpose(1, 0, 2, 3).reshape(B, S, D_ATT)


def nsa_sample(q, gates, all_c, all_s, all_w, past_len, cmp_k, cmp_v, rel_table):
    T = q.shape[1]
    N = all_c.shape[1]
    kc, c_end = compress(all_c[:, :, 0], *cmp_k)
    vc, _ = compress(all_c[:, :, 1], *cmp_v)
    ns = -(-N // SLC_BLOCK)
    slc = jnp.pad(all_s, ((0, 0), (0, ns * SLC_BLOCK - N), (0, 0), (0, 0), (0, 0)))
    ks_bg, vs_bg = to_blocks(slc[:, :, 0]), to_blocks(slc[:, :, 1])
    n_w = all_w.shape[1]
    w_pos = (N - n_w) + jnp.arange(n_w, dtype=jnp.int32)
    q_pos = past_len + jnp.arange(T, dtype=jnp.int32)
    return nsa_attend(q, gates, q_pos, kc, vc, c_end, ks_bg, vs_bg, all_w[:, :, 0], all_w[:, :, 1], w_pos, rel_table)


def gather_pages(pool, page_table):
    g = pool[page_table]
    return g.reshape((g.shape[0], g.shape[1] * g.shape[2]) + g.shape[3:])


def s5_combine(e1, e2):
    a1r, a1i, b1r, b1i = e1
    a2r, a2i, b2r, b2i = e2
    return (a2r * a1r - a2i * a1i, a2r * a1i + a2i * a1r,
            a2r * b1r - a2i * b1i + b2r, a2r * b1i + a2i * b1r + b2i)


def s5_scan(u, h0_re, h0_im, a_re, a_im, log_dt, b_re, b_im, c_re, c_im, d):
    f32 = jnp.float32
    u = u.astype(f32)
    a_re, a_im = a_re.astype(f32), a_im.astype(f32)
    dt = jnp.exp(log_dt.astype(f32))[:, None]
    mag = jnp.exp(dt * a_re)
    ab_re, ab_im = mag * jnp.cos(dt * a_im), mag * jnp.sin(dt * a_im)
    den = a_re * a_re + a_im * a_im
    x_re, x_im = ab_re - 1.0, ab_im
    k_re = (x_re * a_re + x_im * a_im) / den
    k_im = (x_im * a_re - x_re * a_im) / den
    bb_re = k_re[..., None] * b_re - k_im[..., None] * b_im
    bb_im = k_re[..., None] * b_im + k_im[..., None] * b_re
    bu_re = jnp.einsum('btgc,gpc->btgp', u, bb_re)
    bu_im = jnp.einsum('btgc,gpc->btgp', u, bb_im)
    h0_re, h0_im = h0_re.astype(f32), h0_im.astype(f32)
    bu_re = bu_re.at[:, 0].add(ab_re * h0_re - ab_im * h0_im)
    bu_im = bu_im.at[:, 0].add(ab_re * h0_im + ab_im * h0_re)
    T = u.shape[1]
    a_seq_re = jnp.broadcast_to(ab_re, (1, T) + ab_re.shape)
    a_seq_im = jnp.broadcast_to(ab_im, (1, T) + ab_im.shape)
    _, _, h_re, h_im = lax.associative_scan(s5_combine, (a_seq_re, a_seq_im, bu_re, bu_im), axis=1)
    y = (jnp.einsum('btgp,gcp->btgc', h_re, c_re) - jnp.einsum('btgp,gcp->btgc', h_im, c_im) + d * u)
    return y, h_re[:, -1], h_im[:, -1]


def s5_branch(u, h0_re, h0_im, ssm_par, glu_w, glu_b):
    B, T = u.shape[:2]
    y, hr, hi = s5_scan(u.reshape(B, T, N_SSM_GROUPS, S5_GROUP), h0_re, h0_im, *ssm_par)
    y = jax.nn.gelu(y.reshape(B, T, D_SSM))
    out = (y @ glu_w[0] + glu_b[0]) * jax.nn.sigmoid(y @ glu_w[1] + glu_b[1])
    return out, hr, hi


def merge_out(x, o_att, z_att, y_ssm, z_ssm, g_merge, w_att_out, w_ssm_out, w_o, ln_g, ln_b):
    p_att = (o_att * jax.nn.silu(z_att)) @ w_att_out
    p_ssm = (y_ssm * jax.nn.silu(z_ssm)) @ w_ssm_out
    g_att, g_ssm = jnp.split(g_merge, 2, axis=-1)
    merged = jax.nn.sigmoid(g_att) * p_att + jax.nn.sigmoid(g_ssm) * p_ssm
    return layer_norm(ALPHA * x + merged @ w_o, ln_g, ln_b)


def setup_inputs(seed: int = 0) -> dict:
    key = jax.random.key(seed)
    ks = jax.random.split(key, 32)
    f32 = jnp.float32

    def nrm(k, shape, scale):
        return jax.random.normal(k, shape, f32) * scale

    n_pages = PAST_LEN // PAGE_SIZE
    n_used = DEC_BATCH * n_pages
    n_pool = n_used + max(1, n_used // 4)
    w_buf = min(WINDOW, PAST_LEN)
    kv_page = (DEPTH, n_pool, PAGE_SIZE, 2, N_KV, HEAD_DIM)
    page_table = jax.random.permutation(ks[5], n_pool)[:n_used].reshape(DEC_BATCH, n_pages).astype(jnp.int32)
    ssm_st = (DEPTH, DEC_BATCH, N_SSM_GROUPS, S5_STATE)
    return {
        'x_prompt': nrm(ks[0], (BATCH, SEQ, D_MODEL), 1.0),
        'x_sample': nrm(ks[1], (DEC_BATCH, DEC_SEQ, D_MODEL), 1.0),
        'cache_kv_cmp': nrm(ks[2], kv_page, 1.0),
        'cache_kv_slc': nrm(ks[3], kv_page, 1.0),
        'state_win_kv': nrm(ks[4], (DEPTH, DEC_BATCH, w_buf, 2, N_KV, HEAD_DIM), 1.0),
        'state_ssm_re': nrm(ks[6], ssm_st, 0.1),
        'state_ssm_im': nrm(ks[7], ssm_st, 0.1),
        'page_table': page_table,
        'rel_bias': nrm(ks[8], (N_BUCKETS, N_HEADS), 0.5),
        'w_in': nrm(ks[9], (DEPTH, D_MODEL, D_IN), D_MODEL ** -0.5),
        'cmp_w1': nrm(ks[10], (DEPTH, 2, CMP_BLOCK, HEAD_DIM, CMP_HIDDEN), (CMP_BLOCK * HEAD_DIM) ** -0.5),
        'cmp_b1': nrm(ks[11], (DEPTH, 2, CMP_HIDDEN), 0.01),
        'cmp_w2': nrm(ks[12], (DEPTH, 2, CMP_HIDDEN, HEAD_DIM), CMP_HIDDEN ** -0.5),
        'cmp_b2': nrm(ks[13], (DEPTH, 2, HEAD_DIM), 0.01),
        'cmp_pos': nrm(ks[14], (DEPTH, 2, CMP_BLOCK, HEAD_DIM), 0.1),
        'ssm_a_re': -0.5 * jnp.exp(nrm(ks[15], (DEPTH, N_SSM_GROUPS, S5_STATE), 0.05)),
        'ssm_a_im': jnp.broadcast_to(math.pi * jnp.arange(S5_STATE, dtype=f32), (DEPTH, N_SSM_GROUPS, S5_STATE)),
        'ssm_log_dt': jax.random.uniform(ks[16], (DEPTH, N_SSM_GROUPS), f32, math.log(1e-3), math.log(1e-1)),
        'ssm_b_re': nrm(ks[17], (DEPTH, N_SSM_GROUPS, S5_STATE, S5_GROUP), (2 * S5_GROUP) ** -0.5),
        'ssm_b_im': nrm(ks[18], (DEPTH, N_SSM_GROUPS, S5_STATE, S5_GROUP), (2 * S5_GROUP) ** -0.5),
        'ssm_c_re': nrm(ks[19], (DEPTH, N_SSM_GROUPS, S5_GROUP, S5_STATE), (2 * S5_STATE) ** -0.5),
        'ssm_c_im': nrm(ks[20], (DEPTH, N_SSM_GROUPS, S5_GROUP, S5_STATE), (2 * S5_STATE) ** -0.5),
        'ssm_d': nrm(ks[21], (DEPTH, N_SSM_GROUPS, S5_GROUP), 1.0),
        'ssm_glu_w': nrm(ks[22], (DEPTH, 2, D_SSM, D_SSM), D_SSM ** -0.5),
        'ssm_glu_b': nrm(ks[23], (DEPTH, 2, D_SSM), 0.01),
        'w_att_out': nrm(ks[24], (DEPTH, D_ATT, D_MODEL), BETA * D_ATT ** -0.5),
        'w_ssm_out': nrm(ks[25], (DEPTH, D_SSM, D_MODEL), BETA * D_SSM ** -0.5),
        'w_o': nrm(ks[26], (DEPTH, D_MODEL, D_MODEL), BETA * D_MODEL ** -0.5),
        'ln_g': 1.0 + nrm(ks[27], (DEPTH, D_MODEL), 0.01),
        'ln_b': nrm(ks[28], (DEPTH, D_MODEL), 0.01),
    }


def reference(x_prompt, x_sample, cache_kv_cmp, cache_kv_slc, state_win_kv, state_ssm_re, state_ssm_im,
              page_table, rel_bias, w_in, cmp_w1, cmp_b1, cmp_w2, cmp_b2, cmp_pos,
              ssm_a_re, ssm_a_im, ssm_log_dt, ssm_b_re, ssm_b_im, ssm_c_re, ssm_c_im, ssm_d,
              ssm_glu_w, ssm_glu_b, w_att_out, w_ssm_out, w_o, ln_g, ln_b):
    past_len = page_table.shape[1] * cache_kv_cmp.shape[2]
    bp, seq = x_prompt.shape[:2]
    w_buf = state_win_kv.shape[2]
    h_p, h_s = x_prompt, x_sample
    p_cmp, p_slc, p_win, p_re, p_im = [], [], [], [], []
    s_cmp, s_slc, s_win, s_re, s_im = [], [], [], [], []
    for l in range(DEPTH):
        cmp_k = (cmp_w1[l, 0], cmp_b1[l, 0], cmp_w2[l, 0], cmp_b2[l, 0], cmp_pos[l, 0])
        cmp_v = (cmp_w1[l, 1], cmp_b1[l, 1], cmp_w2[l, 1], cmp_b2[l, 1], cmp_pos[l, 1])
        ssm_l = (ssm_a_re[l], ssm_a_im[l], ssm_log_dt[l], ssm_b_re[l], ssm_b_im[l],
                 ssm_c_re[l], ssm_c_im[l], ssm_d[l])
        out_l = (w_att_out[l], w_ssm_out[l], w_o[l], ln_g[l], ln_b[l])

        q, kc, vc, ksl, vsl, kwn, vwn, gn, za, u, zs, gm = split_in(h_p @ w_in[l])
        kv_c, kv_s, kv_w = stack_kv(kc, vc), stack_kv(ksl, vsl), stack_kv(kwn, vwn)
        o_att = nsa_prompt(heads(q, N_HEADS), head_gates(gn), kv_c, kv_s, kv_w, cmp_k, cmp_v, rel_bias)
        h0 = jnp.zeros((bp, N_SSM_GROUPS, S5_STATE), jnp.float32)
        y_ssm, hr, hi = s5_branch(u, h0, h0, ssm_l, ssm_glu_w[l], ssm_glu_b[l])
        p_cmp.append(kv_c)
        p_slc.append(kv_s)
        p_win.append(kv_w[:, seq - min(WINDOW, seq):])
        p_re.append(hr)
        p_im.append(hi)
        h_p = merge_out(h_p, o_att, za, y_ssm, zs, gm, *out_l)

        q, kc, vc, ksl, vsl, kwn, vwn, gn, za, u, zs, gm = split_in(h_s @ w_in[l])
        kv_c, kv_s, kv_w = stack_kv(kc, vc), stack_kv(ksl, vsl), stack_kv(kwn, vwn)
        all_c = jnp.concatenate([gather_pages(cache_kv_cmp[l], page_table), kv_c.astype(cache_kv_cmp.dtype)], axis=1)
        all_s = jnp.concatenate([gather_pages(cache_kv_slc[l], page_table), kv_s.astype(cache_kv_slc.dtype)], axis=1)
        all_w = jnp.concatenate([state_win_kv[l], kv_w.astype(state_win_kv.dtype)], axis=1)
        o_att = nsa_sample(heads(q, N_HEADS), head_gates(gn), all_c, all_s, all_w, past_len, cmp_k, cmp_v, rel_bias)
        y_ssm, hr, hi = s5_branch(u, state_ssm_re[l], state_ssm_im[l], ssm_l, ssm_glu_w[l], ssm_glu_b[l])
        s_cmp.append(kv_c)
        s_slc.append(kv_s)
        s_win.append(all_w[:, all_w.shape[1] - w_buf:])
        s_re.append(hr)
        s_im.append(hi)
        h_s = merge_out(h_s, o_att, za, y_ssm, zs, gm, *out_l)

    return (h_p, h_s,
            jnp.stack(p_cmp), jnp.stack(p_slc), jnp.stack(p_win), jnp.stack(p_re), jnp.stack(p_im),
            jnp.stack(s_cmp), jnp.stack(s_slc), jnp.stack(s_win), jnp.stack(s_re), jnp.stack(s_im))
```

```python
import functools
import math

import numpy as np
import jax
import jax.numpy as jnp
from jax import lax
from jax.experimental import pallas as pl
from jax.experimental.pallas import tpu as pltpu

F32 = jnp.float32
BF16 = jnp.bfloat16

D_MODEL = 1024
N_HEADS = 8
HEAD_DIM = 64
N_KV = 2
HPG = N_HEADS // N_KV
D_ATT = N_HEADS * HEAD_DIM
KV_W = N_KV * HEAD_DIM
CMP_BLOCK = 32
CMP_STRIDE = 16
CMP_HIDDEN = 128
SLC_BLOCK = 64
N_SEL = 16
WINDOW = 512
S5_GROUP = 16
S5_STATE = 64
D_SSM = 512
N_SSM_GROUPS = D_SSM // S5_GROUP
N_BUCKETS = 32
MAX_DISTANCE = 128
LN_EPS = 1e-5
IN_WIDTHS = (D_ATT, KV_W, KV_W, KV_W, KV_W, KV_W, KV_W, 3 * N_HEADS, D_ATT, D_SSM, D_SSM, 2 * D_MODEL)

LANES = 128
SUBLANES = 8
VMEM_LIMIT = 56 * 1024 * 1024
NEG = -1e30
TQ = 128
TK = 128
KV2 = 2 * KV_W
GATE_PAD = LANES
S5_CHUNK = 16
SEQ_COLS = D_ATT + 3 * KV2 + GATE_PAD + D_SSM
Z_COLS = D_ATT + D_SSM + 2 * D_MODEL


def _cparams(sem):
    return pltpu.CompilerParams(dimension_semantics=sem, vmem_limit_bytes=VMEM_LIMIT)


def _dot(a, b):
    return jnp.dot(a, b, preferred_element_type=F32)


def _dot_nt(a, b):
    return lax.dot_general(a, b, (((1,), (1,)), ((), ())), preferred_element_type=F32)


def _split3(x):
    h1 = x.astype(BF16)
    r1 = x - h1.astype(F32)
    h2 = r1.astype(BF16)
    h3 = (r1 - h2.astype(F32)).astype(BF16)
    return h1, h2, h3


def _bucket_np(dist):
    n = np.maximum(dist, 0)
    max_exact = N_BUCKETS // 2
    nf = np.maximum(n, 1).astype(np.float32)
    val = (np.log(nf / np.float32(max_exact)) / np.float32(math.log(MAX_DISTANCE / max_exact))
           * np.float32(N_BUCKETS - max_exact))
    large = max_exact + val.astype(np.int32)
    return np.where(n < max_exact, n, np.minimum(large, N_BUCKETS - 1)).astype(np.int32)


def _pair_perm():
    cols = []
    for r in range(HPG):
        cols += list(range(r * HEAD_DIM, (r + 1) * HEAD_DIM))
        cols += list(range((HPG + r) * HEAD_DIM, (HPG + r + 1) * HEAD_DIM))
    return np.asarray(cols, np.int32)


def _imp_matrix(n_cmp_pad, n_blk_pad, n_cmp):
    ratio = SLC_BLOCK // CMP_STRIDE
    n = np.arange(n_cmp_pad)[:, None]
    j = np.arange(n_blk_pad)[None, :]
    m = ((n >= ratio * j - 1) & (n <= ratio * j + ratio - 2)).astype(np.float32)
    m += ((n >= ratio * j) & (n <= ratio * j + ratio - 1)).astype(np.float32)
    m *= (n < n_cmp)
    return m


def _inproj_kernel(x_ref, w_ref, q_ref, kvc_ref, kvs_ref, kvw_ref, g_ref, u_ref):
    h = _dot(x_ref[...].astype(BF16), w_ref[...])
    o = 0
    q_ref[...] = h[:, o:o + D_ATT] * (HEAD_DIM ** -0.5)
    o += D_ATT
    kvc_ref[...] = h[:, o:o + KV2]
    o += KV2
    kvs_ref[...] = h[:, o:o + KV2]
    o += KV2
    kvw_ref[...] = h[:, o:o + KV2]
    o += KV2
    g_ref[...] = jax.nn.sigmoid(h[:, o:o + GATE_PAD])
    o += GATE_PAD
    u_ref[...] = h[:, o:o + D_SSM]


def _inproj(x2d, w_seq):
    m = x2d.shape[0]
    tm = min(512, m)
    widths = (D_ATT, KV2, KV2, KV2, GATE_PAD, D_SSM)
    return pl.pallas_call(
        _inproj_kernel,
        grid=(m // tm,),
        in_specs=[pl.BlockSpec((tm, D_MODEL), lambda i: (i, 0)),
                  pl.BlockSpec((D_MODEL, SEQ_COLS), lambda i: (0, 0))],
        out_specs=[pl.BlockSpec((tm, w), lambda i: (i, 0)) for w in widths],
        out_shape=[jax.ShapeDtypeStruct((m, w), F32) for w in widths],
        compiler_params=_cparams(("parallel",)),
        name="inproj",
    )(x2d, w_seq)


def _compress_hidden(xflat_bf, w1_ref):
    return _dot(xflat_bf, w1_ref[...])


def _compress_out(hf, hs_next, c1, w2_ref, b2_ref):
    h = hf + hs_next + c1
    return _dot(jax.nn.silu(h).astype(BF16), w2_ref[...]) + b2_ref[...]


def _compress_const(pos_ref, w1_ref, b1_ref):
    hh = _dot(pos_ref[...].astype(BF16), w1_ref[...])
    nh = 4 * CMP_HIDDEN
    return hh[0:1, :nh] + hh[1:2, nh:] + b1_ref[...]


def _softmax_cols(st):
    m = jnp.max(st, axis=0, keepdims=True)
    valid = m > 0.5 * NEG
    e = jnp.exp(st - m)
    s = jnp.sum(e, axis=0, keepdims=True)
    return jnp.where(valid, e / s, 0.0)


def _flash_init(m_ref, l_ref, acc_ref):
    m_ref[...] = jnp.full(m_ref.shape, NEG, F32)
    l_ref[...] = jnp.zeros(l_ref.shape, F32)
    acc_ref[...] = jnp.zeros(acc_ref.shape, F32)


def _flash_update(st, vt_bf, m_ref, l_ref, acc_ref):
    m_old = m_ref[...]
    m_new = jnp.maximum(m_old, jnp.max(st, axis=0, keepdims=True))
    alpha = jnp.exp(m_old - m_new)
    e = jnp.exp(st - m_new)
    l_ref[...] = alpha * l_ref[...] + jnp.sum(e, axis=0, keepdims=True)
    acc_ref[...] = alpha * acc_ref[...] + _dot(vt_bf, e.astype(BF16))
    m_ref[...] = m_new


def _flash_coef(m_ref, l_ref):
    valid = m_ref[...] > 0.5 * NEG
    return jnp.where(valid, 1.0 / jnp.where(valid, l_ref[...], 1.0), 0.0)


def _nsa_prompt_kernel(q_ref, kvc_ref, kvs_ref, kvw_ref, kvcf_ref, g_ref,
                       w1_ref, pos_ref, b1_ref, w2_ref, b2_ref,
                       biasc_ref, t0_ref, t1_ref, t4_ref, c31_ref, mimp_ref, eblk_ref,
                       o_ref,
                       kc_scr, vct_scr, ks_scr, vst_scr, kw_scr, vwt_scr, neg_scr,
                       m_scr, l_scr, acc_scr, tot_scr):
    qt = pl.program_id(1)
    seq = kvs_ref.shape[1]
    n_blk = seq // SLC_BLOCK
    nh = 4 * CMP_HIDDEN

    @pl.when(qt == 0)
    def _per_batch():
        hh = _compress_hidden(kvcf_ref[0].astype(BF16), w1_ref)
        c1 = _compress_const(pos_ref, w1_ref, b1_ref)
        n_ch = hh.shape[0]
        hs_next = pltpu.roll(hh[:, nh:], n_ch - 1, 0)
        kv = _compress_out(hh[:, :nh], hs_next, c1, w2_ref, b2_ref)
        kc_scr[...] = kv[:, :KV_W].astype(BF16)
        vct_scr[...] = kv[:, KV_W:].T.astype(BF16)
        ks_scr[...] = kvs_ref[0, :, :KV_W].astype(BF16)
        vst_scr[...] = kvs_ref[0, :, KV_W:].T.astype(BF16)
        kw_scr[...] = kvw_ref[0, :, :KV_W].astype(BF16)
        vwt_scr[...] = kvw_ref[0, :, KV_W:].T.astype(BF16)

    q = q_ref[0]
    gt = g_ref[0].T
    lane = lax.broadcasted_iota(jnp.int32, (TQ, LANES), 1)
    row = lax.broadcasted_iota(jnp.int32, (LANES, HPG * TQ), 0)
    q0 = qt * TQ

    for g in range(N_KV):
        gmask = (lane < HEAD_DIM) if g == 0 else (lane >= HEAD_DIM)
        qg = jnp.concatenate(
            [jnp.where(gmask, q[:, r * LANES:(r + 1) * LANES], 0.0) for r in range(HPG)], axis=0).astype(BF16)

        def gate_row(r, j):
            i = 3 * (g * HPG + r) + j
            return gt[i:i + 1, :]

        sc = _dot_nt(kc_scr[...], qg) + biasc_ref[0, g]
        pt = _softmax_cols(sc)
        oc = _dot(vct_scr[...], pt.astype(BF16))
        tot = jnp.concatenate(
            [oc[:, r * TQ:(r + 1) * TQ] * gate_row(r, 0) for r in range(HPG)], axis=1)
        tot_scr[...] = tot

        psum = pt[:, 0:TQ]
        for r in range(1, HPG):
            psum = psum + pt[:, r * TQ:(r + 1) * TQ]
        p1, p2, p3 = _split3(psum)
        imp = _dot(mimp_ref[...], p1) + _dot(mimp_ref[...], p2) + _dot(mimp_ref[...], p3)
        bj = lax.broadcasted_iota(jnp.int32, (n_blk, TQ), 0)
        tpos = q0 + lax.broadcasted_iota(jnp.int32, (n_blk, TQ), 1)
        cur = tpos // SLC_BLOCK
        forced = (bj == 0) | (bj == cur) | (bj == cur - 1)
        visible = bj * SLC_BLOCK <= tpos
        score = jnp.where(forced, -NEG, jnp.where(visible, imp, NEG))
        rank = jnp.zeros((n_blk, TQ), F32)
        for i in range(n_blk):
            si = score[i:i + 1, :]
            ahead = (si > score) | ((si == score) & (bj > i))
            rank = rank + jnp.where(ahead, 1.0, 0.0)
        sel = jnp.where(rank < float(N_SEL), 1.0, 0.0).astype(BF16)
        neg_scr[...] = (_dot(eblk_ref[...], sel) - 1.0) * (-NEG)

        _flash_init(m_scr, l_scr, acc_scr)

        def sel_tile(kt, bias):
            k0 = pl.multiple_of(kt * TK, TK)
            st = _dot_nt(ks_scr[pl.ds(k0, TK), :], qg) + bias
            ng = neg_scr[pl.ds(k0, TK), :]
            st = st + jnp.concatenate([ng] * HPG, axis=1)
            _flash_update(st, vst_scr[:, pl.ds(k0, TK)], m_scr, l_scr, acc_scr)

        def far_body(kt, carry):
            sel_tile(kt, c31_ref[g])
            return carry

        lax.fori_loop(0, jnp.maximum(qt - 1, 0), far_body, 0)

        @pl.when(qt >= 1)
        def _near():
            sel_tile(qt - 1, t1_ref[g])

        sel_tile(qt, t0_ref[g])
        coef = _flash_coef(m_scr, l_scr)
        tot_scr[...] = tot_scr[...] + jnp.concatenate(
            [acc_scr[:, r * TQ:(r + 1) * TQ] * (coef[:, r * TQ:(r + 1) * TQ] * gate_row(r, 1))
             for r in range(HPG)], axis=1)

        _flash_init(m_scr, l_scr, acc_scr)

        def win_tile(kt, bias):
            k0 = pl.multiple_of(kt * TK, TK)
            st = _dot_nt(kw_scr[pl.ds(k0, TK), :], qg) + bias
            _flash_update(st, vwt_scr[:, pl.ds(k0, TK)], m_scr, l_scr, acc_scr)

        n_win = WINDOW // TK
        for d in range(n_win, 0, -1):
            bias_ref = t4_ref if d == n_win else (t1_ref if d == 1 else c31_ref)

            @pl.when(qt >= d)
            def _w(d=d, bias_ref=bias_ref):
                win_tile(qt - d, bias_ref[g])

        win_tile(qt, t0_ref[g])
        coef = _flash_coef(m_scr, l_scr)
        tot = tot_scr[...] + jnp.concatenate(
            [acc_scr[:, r * TQ:(r + 1) * TQ] * (coef[:, r * TQ:(r + 1) * TQ] * gate_row(r, 2))
             for r in range(HPG)], axis=1)
        if g == 0:
            tot0 = tot
        else:
            comb = jnp.where(row < HEAD_DIM, tot0, tot)
            for r in range(HPG):
                o_ref[0, :, r * LANES:(r + 1) * LANES] = comb[:, r * TQ:(r + 1) * TQ].T


def _prompt_bias_tables(rel_bias, seq):
    n_qt = seq // TQ
    n_cmp_pad = seq // CMP_STRIDE
    dmax = seq + WINDOW
    tab = rel_bias[_bucket_np(np.arange(dmax))]
    c31 = rel_bias[N_BUCKETS - 1]
    jj = np.arange(TK)[:, None]
    ii = np.arange(TQ)[None, :]

    def by_head(fn):
        return jnp.stack([jnp.concatenate([fn(g * HPG + r) for r in range(HPG)], axis=1) for g in range(N_KV)])

    d0 = ii - jj
    t0 = by_head(lambda h: jnp.where(d0 >= 0, tab[np.clip(d0, 0, None), h], NEG))
    t1 = by_head(lambda h: tab[TK + d0, h])
    t4 = by_head(lambda h: jnp.where(d0 <= 0, c31[h], NEG))
    c31t = by_head(lambda h: jnp.full((1, TQ), 1.0, F32) * c31[h])
    nn = np.arange(n_cmp_pad)[:, None]
    c_end = nn * CMP_STRIDE + (CMP_BLOCK - 1)
    tables = []
    for qt in range(n_qt):
        dc = (qt * TQ + ii) - c_end
        tables.append(by_head(lambda h: jnp.where(dc >= 0, tab[np.clip(dc, 0, None), h], NEG)))
    biasc = jnp.stack(tables)
    return biasc, t0, t1, t4, c31t


def _nsa_prompt(q, kvc, kvs, kvw, gates, cw, tables):
    b, seq, _ = q.shape
    n_qt = seq // TQ
    n_ch = seq // CMP_STRIDE
    n_blk = seq // SLC_BLOCK
    biasc, t0, t1, t4, c31t = tables
    mimp = jnp.asarray(_imp_matrix(n_ch, n_blk, n_ch - 1).T, BF16)
    eblk = jnp.asarray((np.arange(seq)[:, None] // SLC_BLOCK == np.arange(n_blk)[None, :]), BF16)
    kvcf = kvc.reshape(b, n_ch, CMP_STRIDE * KV2)
    cols = HPG * TQ

    def full(a):
        nd = a.ndim
        return pl.BlockSpec(a.shape, lambda i, j, nd=nd: (0,) * nd)

    seq_spec = pl.BlockSpec((1, seq, KV2), lambda i, j: (i, 0, 0))
    return pl.pallas_call(
        _nsa_prompt_kernel,
        grid=(b, n_qt),
        in_specs=[pl.BlockSpec((1, TQ, D_ATT), lambda i, j: (i, j, 0)),
                  seq_spec, seq_spec, seq_spec,
                  pl.BlockSpec((1, n_ch, CMP_STRIDE * KV2), lambda i, j: (i, 0, 0)),
                  pl.BlockSpec((1, TQ, GATE_PAD), lambda i, j: (i, j, 0)),
                  full(cw["w1"]), full(cw["pos"]), full(cw["b1"]), full(cw["w2"]), full(cw["b2"]),
                  pl.BlockSpec((1,) + biasc.shape[1:], lambda i, j: (j, 0, 0, 0)),
                  full(t0), full(t1), full(t4), full(c31t), full(mimp), full(eblk)],
        out_specs=pl.BlockSpec((1, TQ, D_ATT), lambda i, j: (i, j, 0)),
        out_shape=jax.ShapeDtypeStruct((b, seq, D_ATT), F32),
        scratch_shapes=[pltpu.VMEM((n_ch, KV_W), BF16), pltpu.VMEM((KV_W, n_ch), BF16),
                        pltpu.VMEM((seq, KV_W), BF16), pltpu.VMEM((KV_W, seq), BF16),
                        pltpu.VMEM((seq, KV_W), BF16), pltpu.VMEM((KV_W, seq), BF16),
                        pltpu.VMEM((seq, TQ), F32),
                        pltpu.VMEM((1, cols), F32), pltpu.VMEM((1, cols), F32),
                        pltpu.VMEM((KV_W, cols), F32), pltpu.VMEM((KV_W, cols), F32)],
        compiler_params=_cparams(("arbitrary", "arbitrary")),
        name="nsa_prompt",
    )(q, kvc, kvs, kvw, kvcf, gates, cw["w1"], cw["pos"], cw["b1"], cw["w2"], cw["b2"],
      biasc, t0, t1, t4, c31t, mimp, eblk)


def _s5_kernel(u_ref, h0_ref, v_ref, m_ref, wh_ref, a1_ref, a2_ref, y_ref, ht_ref, z_scr, *, n_chunks, bt):
    gc = u_ref.shape[1]
    sw = 2 * S5_STATE
    for g in range(gc):
        z_scr[:, g * sw:(g + 1) * sw] = _dot(u_ref[0, g], v_ref[g])
    a1 = a1_ref[...]
    a2 = a2_ref[...]

    def step(k, h):
        r0 = pl.multiple_of(k * bt, bt)
        z = z_scr[pl.ds(r0, bt), :]
        z_scr[pl.ds(r0, bt), :] = h
        hsw = jnp.concatenate(
            [pltpu.roll(h[:, g * sw:(g + 1) * sw], S5_STATE, 1) for g in range(gc)], axis=1)
        return a1 * h + a2 * hsw + z

    ht_ref[0] = lax.fori_loop(0, n_chunks, step, h0_ref[0])
    for g in range(gc):
        y_ref[0, g] = (_dot(u_ref[0, g], m_ref[g])
                       + _dot(z_scr[:, g * sw:(g + 1) * sw].astype(BF16), wh_ref[g]))


def _s5_mats(a_re, a_im, log_dt, b_re, b_im, c_re, c_im, chunk):
    hp = lax.Precision.HIGHEST
    dt = jnp.exp(log_dt)[:, None]
    mag = jnp.exp(dt * a_re)
    ab_re, ab_im = mag * jnp.cos(dt * a_im), mag * jnp.sin(dt * a_im)
    den = a_re * a_re + a_im * a_im
    x_re, x_im = ab_re - 1.0, ab_im
    k_re = (x_re * a_re + x_im * a_im) / den
    k_im = (x_im * a_re - x_re * a_im) / den
    bb_re = k_re[..., None] * b_re - k_im[..., None] * b_im
    bb_im = k_re[..., None] * b_im + k_im[..., None] * b_re
    pw_re, pw_im = [jnp.ones_like(ab_re)], [jnp.zeros_like(ab_im)]
    for _ in range(chunk):
        pr, pi = pw_re[-1], pw_im[-1]
        pw_re.append(pr * ab_re - pi * ab_im)
        pw_im.append(pr * ab_im + pi * ab_re)
    pw_re, pw_im = jnp.stack(pw_re), jnp.stack(pw_im)
    cl_re = c_re[None] * pw_re[:, :, None, :] - c_im[None] * pw_im[:, :, None, :]
    cl_im = c_re[None] * pw_im[:, :, None, :] + c_im[None] * pw_re[:, :, None, :]
    t = (jnp.einsum('tgcp,gpd->tgcd', cl_re, bb_re, precision=hp)
         - jnp.einsum('tgcp,gpd->tgcd', cl_im, bb_im, precision=hp))
    s_i = np.arange(chunk)[:, None]
    t_i = np.arange(chunk)[None, :]
    lag = np.clip(t_i - s_i, 0, None)
    blk = jnp.where((t_i >= s_i)[:, :, None, None, None], t[lag], 0.0)
    ng = a_re.shape[0]
    m = blk.transpose(2, 0, 4, 1, 3).reshape(ng, chunk * S5_GROUP, chunk * S5_GROUP)
    rev = chunk - 1 - np.arange(chunk)
    vb_re = pw_re[rev][:, :, :, None] * bb_re[None] - pw_im[rev][:, :, :, None] * bb_im[None]
    vb_im = pw_re[rev][:, :, :, None] * bb_im[None] + pw_im[rev][:, :, :, None] * bb_re[None]
    v = jnp.concatenate([vb_re, vb_im], axis=2).transpose(1, 0, 3, 2).reshape(ng, chunk * S5_GROUP, 2 * S5_STATE)
    wh = jnp.concatenate([cl_re[1:], -cl_im[1:]], axis=3)
    wh = wh.transpose(1, 3, 0, 2).reshape(ng, 2 * S5_STATE, chunk * S5_GROUP)
    a1 = jnp.concatenate([pw_re[chunk], pw_re[chunk]], axis=1).reshape(1, ng * 2 * S5_STATE)
    a2 = jnp.concatenate([-pw_im[chunk], pw_im[chunk]], axis=1).reshape(1, ng * 2 * S5_STATE)
    return v.astype(BF16), m.astype(BF16), wh.astype(BF16), a1, a2


def _s5(u, h0_re, h0_im, mats, chunk, bt, gc=8):
    b, t_len, _ = u.shape
    ng = N_SSM_GROUPS
    n_chunks = t_len // chunk
    nbc = b // bt
    lc = chunk * S5_GROUP
    rows = n_chunks * bt
    sw = 2 * S5_STATE
    v, m, wh, a1, a2 = mats
    ur = u.reshape(nbc, bt, n_chunks, chunk, ng, S5_GROUP).transpose(0, 4, 2, 1, 3, 5)
    ur = ur.reshape(nbc, ng, rows, lc).astype(BF16)
    h0 = jnp.concatenate([h0_re, h0_im], axis=-1).reshape(nbc, bt, ng * sw)
    kern = functools.partial(_s5_kernel, n_chunks=n_chunks, bt=bt)
    y, ht = pl.pallas_call(
        kern,
        grid=(nbc, ng // gc),
        in_specs=[pl.BlockSpec((1, gc, rows, lc), lambda i, j: (i, j, 0, 0)),
                  pl.BlockSpec((1, bt, gc * sw), lambda i, j: (i, 0, j)),
                  pl.BlockSpec((gc, lc, sw), lambda i, j: (j, 0, 0)),
                  pl.BlockSpec((gc, lc, lc), lambda i, j: (j, 0, 0)),
                  pl.BlockSpec((gc, sw, lc), lambda i, j: (j, 0, 0)),
                  pl.BlockSpec((1, gc * sw), lambda i, j: (0, j)),
                  pl.BlockSpec((1, gc * sw), lambda i, j: (0, j))],
        out_specs=[pl.BlockSpec((1, gc, rows, lc), lambda i, j: (i, j, 0, 0)),
                   pl.BlockSpec((1, bt, gc * sw), lambda i, j: (i, 0, j))],
        out_shape=[jax.ShapeDtypeStruct((nbc, ng, rows, lc), F32),
                   jax.ShapeDtypeStruct((nbc, bt, ng * sw), F32)],
        scratch_shapes=[pltpu.VMEM((rows, gc * sw), F32)],
        compiler_params=_cparams(("parallel", "parallel")),
        name="s5_scan",
    )(ur, h0, v, m, wh, a1, a2)
    y = y.reshape(nbc, ng, n_chunks, bt, chunk, S5_GROUP).transpose(0, 3, 2, 4, 1, 5).reshape(b, t_len, D_SSM)
    ht = ht.reshape(b, ng, sw)
    return y, ht[..., :S5_STATE], ht[..., S5_STATE:]


def _epilogue_kernel(x_ref, o_ref, y_ref, u_ref, wz_ref, d_ref, glu0_ref, glu1_ref, gb0_ref, gb1_ref,
                     wa_ref, ws_ref, wo_ref, lng_ref, lnb_ref, out_ref, *, alpha):
    x = x_ref[...]
    z = _dot(x.astype(BF16), wz_ref[...])
    za = z[:, :D_ATT]
    zs = z[:, D_ATT:D_ATT + D_SSM]
    ga = z[:, D_ATT + D_SSM:D_ATT + D_SSM + D_MODEL]
    gs = z[:, D_ATT + D_SSM + D_MODEL:]
    p_att = _dot((o_ref[...] * jax.nn.silu(za)).astype(BF16), wa_ref[...])
    y = jax.nn.gelu(y_ref[...] + d_ref[...] * u_ref[...]).astype(BF16)
    s = (_dot(y, glu0_ref[...]) + gb0_ref[...]) * jax.nn.sigmoid(_dot(y, glu1_ref[...]) + gb1_ref[...])
    p_ssm = _dot((s * jax.nn.silu(zs)).astype(BF16), ws_ref[...])
    merged = jax.nn.sigmoid(ga) * p_att + jax.nn.sigmoid(gs) * p_ssm
    r = alpha * x + _dot(merged.astype(BF16), wo_ref[...])
    mu = jnp.mean(r, axis=-1, keepdims=True)
    c = r - mu
    var = jnp.mean(c * c, axis=-1, keepdims=True)
    out_ref[...] = c * lax.rsqrt(var + LN_EPS) * lng_ref[...] + lnb_ref[...]


def _epilogue(x2d, o2d, y2d, u2d, ew, alpha):
    m = x2d.shape[0]
    tm = min(256, m)
    names = ("wz", "d", "glu0", "glu1", "gb0", "gb1", "wa", "ws", "wo", "lng", "lnb")
    ws = [ew[n] for n in names]

    def tok(w):
        return pl.BlockSpec((tm, w), lambda i: (i, 0))

    return pl.pallas_call(
        functools.partial(_epilogue_kernel, alpha=alpha),
        grid=(m // tm,),
        in_specs=[tok(D_MODEL), tok(D_ATT), tok(D_SSM), tok(D_SSM)]
        + [pl.BlockSpec(w.shape, lambda i: (0, 0)) for w in ws],
        out_specs=tok(D_MODEL),
        out_shape=jax.ShapeDtypeStruct((m, D_MODEL), F32),
        compiler_params=_cparams(("parallel",)),
        name="epilogue",
    )(x2d, o2d, y2d, u2d, *ws)


def _layer_weights(l, w_in, cmp_w1, cmp_b1, cmp_w2, cmp_b2, cmp_pos, ssm_d, ssm_glu_w, ssm_glu_b,
                   w_att_out, w_ssm_out, w_o, ln_g, ln_b):
    offs = np.concatenate([[0], np.cumsum(IN_WIDTHS)])
    col = lambda i: w_in[l][:, offs[i]:offs[i + 1]]
    perm = _pair_perm()
    gate = jnp.pad(col(7), ((0, 0), (0, GATE_PAD - 3 * N_HEADS)))
    w_seq = jnp.concatenate([col(0)[:, perm], col(1), col(2), col(3), col(4), col(5), col(6), gate, col(9)],
                            axis=1).astype(BF16)
    wz = jnp.concatenate([col(8)[:, perm], col(10), col(11)], axis=1).astype(BF16)

    half = CMP_BLOCK // 2
    eye = jnp.eye(4, dtype=F32)
    w1s = jnp.stack([cmp_w1[l, 0], cmp_w1[l, 0], cmp_w1[l, 1], cmp_w1[l, 1]])
    bd = jnp.einsum('spdh,st->psdth', w1s, eye)
    bd = bd.reshape(CMP_BLOCK, 4 * HEAD_DIM, 4 * CMP_HIDDEN)
    w1 = jnp.concatenate([bd[:half].reshape(half * 4 * HEAD_DIM, -1),
                          bd[half:].reshape(half * 4 * HEAD_DIM, -1)], axis=1).astype(BF16)
    pos_s = jnp.stack([cmp_pos[l, 0], cmp_pos[l, 0], cmp_pos[l, 1], cmp_pos[l, 1]], axis=1)
    pos = jnp.zeros((SUBLANES, half * 4 * HEAD_DIM), F32)
    pos = pos.at[0].set(pos_s[:half].reshape(-1)).at[1].set(pos_s[half:].reshape(-1))
    b1 = jnp.concatenate([cmp_b1[l, 0], cmp_b1[l, 0], cmp_b1[l, 1], cmp_b1[l, 1]]).reshape(1, -1)
    w2s = jnp.stack([cmp_w2[l, 0], cmp_w2[l, 0], cmp_w2[l, 1], cmp_w2[l, 1]])
    w2 = jnp.einsum('shd,st->shtd', w2s, eye).reshape(4 * CMP_HIDDEN, 4 * HEAD_DIM).astype(BF16)
    b2 = jnp.concatenate([cmp_b2[l, 0], cmp_b2[l, 0], cmp_b2[l, 1], cmp_b2[l, 1]]).reshape(1, -1)
    cw = dict(w1=w1, pos=pos, b1=b1, w2=w2, b2=b2)

    ew = dict(wz=wz, d=ssm_d[l].reshape(1, D_SSM),
              glu0=ssm_glu_w[l, 0].astype(BF16), glu1=ssm_glu_w[l, 1].astype(BF16),
              gb0=ssm_glu_b[l, 0].reshape(1, -1), gb1=ssm_glu_b[l, 1].reshape(1, -1),
              wa=w_att_out[l][perm].astype(BF16), ws=w_ssm_out[l].astype(BF16), wo=w_o[l].astype(BF16),
              lng=ln_g[l].reshape(1, -1), lnb=ln_b[l].reshape(1, -1))
    return w_seq, cw, ew


def _stack_kv_out(kv, b, t):
    return kv.reshape(b, t, 2, N_KV, HEAD_DIM)


def _prompt_layer(h_p, lw, s5m, tables, alpha):
    b, seq, _ = h_p.shape
    w_seq, cw, ew = lw
    x2d = h_p.reshape(b * seq, D_MODEL)
    q, kvc, kvs, kvw, gates, u = _inproj(x2d, w_seq)
    r3 = lambda a: a.reshape(b, seq, a.shape[-1])
    kvc, kvs, kvw = r3(kvc), r3(kvs), r3(kvw)
    o_att = _nsa_prompt(r3(q), kvc, kvs, kvw, r3(gates), cw, tables)
    zero = jnp.zeros((b, N_SSM_GROUPS, S5_STATE), F32)
    y, hr, hi = _s5(r3(u), zero, zero, s5m, S5_CHUNK, bt=SUBLANES)
    out = _epilogue(x2d, o_att.reshape(b * seq, D_ATT), y.reshape(b * seq, D_SSM), u, ew, alpha)
    return out.reshape(b, seq, D_MODEL), kvc, kvs, kvw, hr, hi


PAGES_PER_STEP = 32


def _page_copies(cache_hbm, pt_ref, buf, sem, layer, b, c, slot):
    return [pltpu.make_async_copy(cache_hbm.at[layer, pt_ref[b, c * PAGES_PER_STEP + j]],
                                  buf.at[slot, j], sem.at[slot])
            for j in range(PAGES_PER_STEP)]


def _gather_step(cache_hbm, pt_ref, buf, sem, layer):
    b, c = pl.program_id(0), pl.program_id(1)
    n_c = pl.num_programs(1)
    step = b * n_c + c
    total = pl.num_programs(0) * n_c
    slot = step % 2

    @pl.when(step == 0)
    def _first():
        for cp in _page_copies(cache_hbm, pt_ref, buf, sem, layer, b, c, slot):
            cp.start()

    @pl.when(step + 1 < total)
    def _prefetch():
        nxt = step + 1
        for cp in _page_copies(cache_hbm, pt_ref, buf, sem, layer, nxt // n_c, nxt % n_c, 1 - slot):
            cp.start()

    for cp in _page_copies(cache_hbm, pt_ref, buf, sem, layer, b, c, slot):
        cp.wait()
    return slot


def _sample_queries(q):
    t = q.shape[0]
    lane = lax.broadcasted_iota(jnp.int32, (t, LANES), 1)
    parts = []
    for g in range(N_KV):
        gmask = (lane < HEAD_DIM) if g == 0 else (lane >= HEAD_DIM)
        parts += [jnp.where(gmask, q[:, r * LANES:(r + 1) * LANES], 0.0) for r in range(HPG)]
    return jnp.concatenate(parts, axis=0).astype(BF16)


def _softmax_rows(s):
    m = jnp.max(s, axis=1, keepdims=True)
    valid = m > 0.5 * NEG
    e = jnp.exp(s - m)
    return jnp.where(valid, e / jnp.sum(e, axis=1, keepdims=True), 0.0)


def _cmp_sample_kernel(pt_ref, cache_hbm, q_ref, w1_ref, pos_ref, b1_ref, w2_ref, b2_ref, biasc_ref, mimp_ref,
                       oc_ref, sel_ref, buf, sem, hf_scr, hs_scr, *, layer, t_len, past_len, n_blk):
    c = pl.program_id(1)
    n_c = pl.num_programs(1)
    nh = 4 * CMP_HIDDEN
    rows = PAGES_PER_STEP * SUBLANES
    slot = _gather_step(cache_hbm, pt_ref, buf, sem, layer)
    x = buf[slot].reshape(rows, CMP_STRIDE * KV2).astype(BF16)
    hh = _compress_hidden(x, w1_ref)
    r0 = pl.multiple_of(c * rows, rows)
    hf_scr[pl.ds(r0, rows), :] = hh[:, :nh]
    hs_scr[pl.ds(r0, rows), :] = hh[:, nh:]

    @pl.when(c == n_c - 1)
    def _finish():
        n_ch = hf_scr.shape[0]
        c1 = _compress_const(pos_ref, w1_ref, b1_ref)
        hs_next = pltpu.roll(hs_scr[...], n_ch - 1, 0)
        kv = _compress_out(hf_scr[...], hs_next, c1, w2_ref, b2_ref)
        q64 = _sample_queries(q_ref[0])
        p = _softmax_rows(_dot_nt(q64, kv[:, :KV_W].astype(BF16)) + biasc_ref[...])
        oc_ref[0] = _dot(p.astype(BF16), kv[:, KV_W:].astype(BF16))
        psum = []
        for g in range(N_KV):
            acc = p[g * HPG * t_len:g * HPG * t_len + t_len]
            for r in range(1, HPG):
                acc = acc + p[(g * HPG + r) * t_len:(g * HPG + r + 1) * t_len]
            psum.append(acc)
        psum = jnp.concatenate(psum, axis=0)
        p1, p2, p3 = _split3(psum)
        imp = _dot(p1, mimp_ref[...]) + _dot(p2, mimp_ref[...]) + _dot(p3, mimp_ref[...])
        shape = imp.shape
        bj = lax.broadcasted_iota(jnp.int32, shape, 1)
        tpos = past_len + lax.broadcasted_iota(jnp.int32, shape, 0) % t_len
        cur = tpos // SLC_BLOCK
        forced = (bj == 0) | (bj == cur) | (bj == cur - 1)
        visible = bj * SLC_BLOCK <= tpos
        score = jnp.where(forced, -NEG, jnp.where(visible, imp, NEG))
        score = jnp.where(bj < n_blk, score, 2.0 * NEG)
        sel = jnp.zeros(shape, F32)
        for _ in range(min(N_SEL, n_blk)):
            best = jnp.max(score, axis=1, keepdims=True)
            first = jnp.min(jnp.where(score == best, bj, shape[1]), axis=1, keepdims=True)
            hit = bj == first
            sel = jnp.where(hit, 1.0, sel)
            score = jnp.where(hit, 3.0 * NEG, score)
        sel_ref[0] = sel


def _slc_sample_kernel(pt_ref, cache_hbm, q_ref, sel_ref, knew_ref, bias_ref, biasn_ref, eblk_ref,
                       os_ref, buf, sem, m_scr, l_scr, acc_scr, *, layer):
    c = pl.program_id(1)
    n_c = pl.num_programs(1)
    rows = PAGES_PER_STEP * LANES
    slot = _gather_step(cache_hbm, pt_ref, buf, sem, layer)
    q64 = _sample_queries(q_ref[0])

    @pl.when(c == 0)
    def _init():
        m_scr[...] = jnp.full(m_scr.shape, NEG, F32)
        l_scr[...] = jnp.zeros(l_scr.shape, F32)
        acc_scr[...] = jnp.zeros(acc_scr.shape, F32)

    def update(s, v_bf):
        m_old = m_scr[...]
        m_new = jnp.maximum(m_old, jnp.max(s, axis=1, keepdims=True))
        alpha = jnp.exp(m_old - m_new)
        e = jnp.exp(s - m_new)
        l_scr[...] = alpha * l_scr[...] + jnp.sum(e, axis=1, keepdims=True)
        acc_scr[...] = alpha * acc_scr[...] + _dot(e.astype(BF16), v_bf)
        m_scr[...] = m_new

    x = buf[slot].reshape(rows, KV2)
    neg = (_dot(sel_ref[0, 0].astype(BF16), eblk_ref[...]) - 1.0) * (-NEG)
    update(_dot_nt(q64, x[:, :KV_W].astype(BF16)) + bias_ref[...] + neg, x[:, KV_W:].astype(BF16))

    @pl.when(c == n_c - 1)
    def _finish():
        kn = knew_ref[0]
        update(_dot_nt(q64, kn[:, :KV_W].astype(BF16)) + biasn_ref[...], kn[:, KV_W:].astype(BF16))
        valid = m_scr[...] > 0.5 * NEG
        os_ref[0] = jnp.where(valid, acc_scr[...] / jnp.where(valid, l_scr[...], 1.0), 0.0)


def _win_sample_kernel(q_ref, kvw_ref, biasw_ref, g_ref, oc_ref, os_ref, o_ref, *, t_len):
    q64 = _sample_queries(q_ref[0])
    kw = kvw_ref[0]
    p = _softmax_rows(_dot_nt(q64, kw[:, :KV_W].astype(BF16)) + biasw_ref[...])
    ow = _dot(p.astype(BF16), kw[:, KV_W:].astype(BF16))
    gates = g_ref[0]
    oc, osel = oc_ref[0], os_ref[0]
    lane = lax.broadcasted_iota(jnp.int32, (t_len, LANES), 1)

    def head(g, r):
        h = g * HPG + r
        sl = slice(h * t_len, (h + 1) * t_len)
        return (gates[:, 3 * h:3 * h + 1] * oc[sl] + gates[:, 3 * h + 1:3 * h + 2] * osel[sl]
                + gates[:, 3 * h + 2:3 * h + 3] * ow[sl])

    for r in range(HPG):
        o_ref[0, :, r * LANES:(r + 1) * LANES] = jnp.where(lane < HEAD_DIM, head(0, r), head(1, r))


def _sample_bias_tables(rel_bias, past_len, t_len, w_buf, w_pad):
    n_tok = past_len + t_len
    tab = rel_bias[_bucket_np(np.arange(n_tok + 1))]
    tt = np.arange(t_len)

    def rows(fn):
        return jnp.concatenate([fn(h) for h in range(N_HEADS)], axis=0)

    n_ch = past_len // CMP_STRIDE
    c_end = np.arange(n_ch) * CMP_STRIDE + (CMP_BLOCK - 1)
    dc = (past_len + tt)[:, None] - c_end[None, :]
    ok_c = (dc >= 0) & (np.arange(n_ch)[None, :] < (n_tok // CMP_STRIDE - 1))
    biasc = rows(lambda h: jnp.where(ok_c, tab[np.clip(dc, 0, None), h], NEG))
    ds = (past_len + tt)[:, None] - np.arange(past_len)[None, :]
    biass = rows(lambda h: tab[ds, h])
    ll = np.arange(LANES)
    dn = tt[:, None] - ll[None, :]
    ok_n = (dn >= 0) & (ll[None, :] < t_len)
    biasn = rows(lambda h: jnp.where(ok_n, tab[np.clip(dn, 0, None), h], NEG))
    ii = np.arange(w_pad)
    w_pos = n_tok - (w_buf + t_len) + ii
    dw = (past_len + tt)[:, None] - w_pos[None, :]
    ok_w = (dw >= 0) & (dw <= WINDOW) & (w_pos[None, :] >= 0) & (ii[None, :] < w_buf + t_len)
    biasw = rows(lambda h: jnp.where(ok_w, tab[np.clip(dw, 0, n_tok), h], NEG))
    return biasc, biass, biasn, biasw


def _nsa_sample(layer, q, gates, kvs_new, all_w_pad, cache_cmp, cache_slc, page_table, cw, tables):
    b, t_len, _ = q.shape
    n_pages = page_table.shape[1]
    page = cache_cmp.shape[2]
    past_len = n_pages * page
    n_c = n_pages // PAGES_PER_STEP
    n_ch = past_len // CMP_STRIDE
    n_blk = -(-(past_len + t_len) // SLC_BLOCK)
    blk_pad = -(-n_blk // LANES) * LANES
    blk_step = PAGES_PER_STEP * page // SLC_BLOCK
    nq = N_HEADS * t_len
    biasc, biass, biasn, biasw = tables
    mimp = jnp.asarray(_imp_matrix(n_ch, blk_pad, (past_len + t_len) // CMP_STRIDE - 1), BF16)
    cmp_flat = cache_cmp.reshape(cache_cmp.shape[0], cache_cmp.shape[1], page // CMP_STRIDE, CMP_STRIDE * KV2)
    slc_rows = cache_slc.reshape(cache_slc.shape[0], cache_slc.shape[1], page, KV2)

    def full(a):
        nd = a.ndim
        return pl.BlockSpec(a.shape, lambda i, j, pt, nd=nd: (0,) * nd)

    q_spec = pl.BlockSpec((1, t_len, D_ATT), lambda i, j, pt: (i, 0, 0))
    o_spec = pl.BlockSpec((1, nq, KV_W), lambda i, j, pt: (i, 0, 0))
    any_spec = pl.BlockSpec(memory_space=pl.ANY)
    oc, sel = pl.pallas_call(
        functools.partial(_cmp_sample_kernel, layer=layer, t_len=t_len, past_len=past_len, n_blk=n_blk),
        grid_spec=pltpu.PrefetchScalarGridSpec(
            num_scalar_prefetch=1, grid=(b, n_c),
            in_specs=[any_spec, q_spec, full(cw["w1"]), full(cw["pos"]), full(cw["b1"]), full(cw["w2"]),
                      full(cw["b2"]), full(biasc), full(mimp)],
            out_specs=[o_spec, pl.BlockSpec((1, N_KV * t_len, blk_pad), lambda i, j, pt: (i, 0, 0))],
            scratch_shapes=[pltpu.VMEM((2, PAGES_PER_STEP, page // CMP_STRIDE, CMP_STRIDE * KV2), F32),
                            pltpu.SemaphoreType.DMA((2,)),
                            pltpu.VMEM((n_ch, 4 * CMP_HIDDEN), F32), pltpu.VMEM((n_ch, 4 * CMP_HIDDEN), F32)]),
        out_shape=[jax.ShapeDtypeStruct((b, nq, KV_W), F32),
                   jax.ShapeDtypeStruct((b, N_KV * t_len, blk_pad), F32)],
        compiler_params=_cparams(("arbitrary", "arbitrary")),
        name="nsa_sample_cmp",
    )(page_table, cmp_flat, q, cw["w1"], cw["pos"], cw["b1"], cw["w2"], cw["b2"], biasc, mimp)

    sel_c = sel[:, :, :n_c * blk_step].reshape(b, N_KV, 1, t_len, n_c, blk_step)
    sel_c = jnp.broadcast_to(sel_c, (b, N_KV, HPG, t_len, n_c, blk_step))
    sel_c = sel_c.transpose(0, 4, 1, 2, 3, 5).reshape(b, n_c, nq, blk_step)
    sel_c = jnp.pad(sel_c, ((0, 0), (0, 0), (0, 0), (0, LANES - blk_step)))
    rows = PAGES_PER_STEP * page
    eblk = jnp.asarray(np.arange(LANES)[:, None] == (np.arange(rows)[None, :] // SLC_BLOCK), BF16)
    o_s = pl.pallas_call(
        functools.partial(_slc_sample_kernel, layer=layer),
        grid_spec=pltpu.PrefetchScalarGridSpec(
            num_scalar_prefetch=1, grid=(b, n_c),
            in_specs=[any_spec, q_spec,
                      pl.BlockSpec((1, 1, nq, LANES), lambda i, j, pt: (i, j, 0, 0)),
                      pl.BlockSpec((1, LANES, KV2), lambda i, j, pt: (i, 0, 0)),
                      pl.BlockSpec((nq, rows), lambda i, j, pt: (0, j)),
                      full(biasn), full(eblk)],
            out_specs=o_spec,
            scratch_shapes=[pltpu.VMEM((2, PAGES_PER_STEP, page, KV2), F32),
                            pltpu.SemaphoreType.DMA((2,)),
                            pltpu.VMEM((nq, 1), F32), pltpu.VMEM((nq, 1), F32), pltpu.VMEM((nq, KV_W), F32)]),
        out_shape=jax.ShapeDtypeStruct((b, nq, KV_W), F32),
        compiler_params=_cparams(("arbitrary", "arbitrary")),
        name="nsa_sample_slc",
    )(page_table, slc_rows, q, sel_c, kvs_new, biass, biasn, eblk)

    w_pad = all_w_pad.shape[1]
    return pl.pallas_call(
        functools.partial(_win_sample_kernel, t_len=t_len),
        grid=(b,),
        in_specs=[pl.BlockSpec((1, t_len, D_ATT), lambda i: (i, 0, 0)),
                  pl.BlockSpec((1, w_pad, KV2), lambda i: (i, 0, 0)),
                  pl.BlockSpec(biasw.shape, lambda i: (0, 0)),
                  pl.BlockSpec((1, t_len, GATE_PAD), lambda i: (i, 0, 0)),
                  pl.BlockSpec((1, nq, KV_W), lambda i: (i, 0, 0)),
                  pl.BlockSpec((1, nq, KV_W), lambda i: (i, 0, 0))],
        out_specs=pl.BlockSpec((1, t_len, D_ATT), lambda i: (i, 0, 0)),
        out_shape=jax.ShapeDtypeStruct((b, t_len, D_ATT), F32),
        compiler_params=_cparams(("parallel",)),
        name="nsa_sample_win",
    )(q, all_w_pad, biasw, gates, oc, o_s)


def _sample_layer(layer, h_s, lw, s5m, tables, alpha, cache_cmp, cache_slc, win_state, h0_re, h0_im, page_table):
    b, t_len, _ = h_s.shape
    w_seq, cw, ew = lw
    x2d = h_s.reshape(b * t_len, D_MODEL)
    q, kvc, kvs, kvw, gates, u = _inproj(x2d, w_seq)
    r3 = lambda a: a.reshape(b, t_len, a.shape[-1])
    kvc, kvs, kvw = r3(kvc), r3(kvs), r3(kvw)
    w_buf = win_state.shape[1]
    all_w = jnp.concatenate([win_state.reshape(b, w_buf, KV2), kvw], axis=1)
    w_pad = -(-(w_buf + t_len) // LANES) * LANES
    all_w_pad = jnp.pad(all_w, ((0, 0), (0, w_pad - (w_buf + t_len)), (0, 0)))
    kvs_new = jnp.pad(kvs, ((0, 0), (0, LANES - t_len), (0, 0)))
    o_att = _nsa_sample(layer, r3(q), r3(gates), kvs_new, all_w_pad, cache_cmp, cache_slc, page_table, cw, tables)
    y, hr, hi = _s5(r3(u), h0_re, h0_im, s5m, t_len, bt=b)
    out = _epilogue(x2d, o_att.reshape(b * t_len, D_ATT), y.reshape(b * t_len, D_SSM), u, ew, alpha)
    return out.reshape(b, t_len, D_MODEL), kvc, kvs, all_w[:, t_len:], hr, hi


def kernel(x_prompt, x_sample, cache_kv_cmp, cache_kv_slc, state_win_kv, state_ssm_re, state_ssm_im,
           page_table, rel_bias, w_in, cmp_w1, cmp_b1, cmp_w2, cmp_b2, cmp_pos,
           ssm_a_re, ssm_a_im, ssm_log_dt, ssm_b_re, ssm_b_im, ssm_c_re, ssm_c_im, ssm_d,
           ssm_glu_w, ssm_glu_b, w_att_out, w_ssm_out, w_o, ln_g, ln_b):
    depth = w_in.shape[0]
    alpha = (2 * depth) ** 0.25
    bp, seq = x_prompt.shape[:2]
    bs, t_len = x_sample.shape[:2]
    past_len = page_table.shape[1] * cache_kv_cmp.shape[2]
    w_buf = state_win_kv.shape[2]
    w_pad = -(-(w_buf + t_len) // LANES) * LANES
    p_tables = _prompt_bias_tables(rel_bias, seq)
    s_tables = _sample_bias_tables(rel_bias, past_len, t_len, w_buf, w_pad)
    h_p, h_s = x_prompt, x_sample
    outs_p, outs_s = [], []
    for l in range(depth):
        lw = _layer_weights(l, w_in, cmp_w1, cmp_b1, cmp_w2, cmp_b2, cmp_pos, ssm_d, ssm_glu_w, ssm_glu_b,
                            w_att_out, w_ssm_out, w_o, ln_g, ln_b)
        ssm = (ssm_a_re[l], ssm_a_im[l], ssm_log_dt[l], ssm_b_re[l], ssm_b_im[l], ssm_c_re[l], ssm_c_im[l])
        h_p, kvc, kvs, kvw, hr, hi = _prompt_layer(h_p, lw, _s5_mats(*ssm, S5_CHUNK), p_tables, alpha)
        outs_p.append((_stack_kv_out(kvc, bp, seq), _stack_kv_out(kvs, bp, seq),
                       _stack_kv_out(kvw[:, seq - min(WINDOW, seq):], bp, min(WINDOW, seq)), hr, hi))
        h_s, kvc, kvs, win, hr, hi = _sample_layer(
            l, h_s, lw, _s5_mats(*ssm, t_len), s_tables, alpha, cache_kv_cmp, cache_kv_slc,
            state_win_kv[l], state_ssm_re[l], state_ssm_im[l], page_table)
        outs_s.append((_stack_kv_out(kvc, bs, t_len), _stack_kv_out(kvs, bs, t_len),
                       _stack_kv_out(win, bs, w_buf), hr, hi))
    stack = lambda outs, i: jnp.stack([o[i] for o in outs])
    return (h_p, h_s) + tuple(stack(outs_p, i) for i in range(5)) + tuple(stack(outs_s, i) for i in range(5))
```

```python
import functools
import math

import numpy as np
import jax
import jax.numpy as jnp
from jax import lax
from jax.experimental import pallas as pl
from jax.experimental.pallas import tpu as pltpu

F32 = jnp.float32
BF16 = jnp.bfloat16

D_MODEL = 1024
N_HEADS = 8
HEAD_DIM = 64
N_KV = 2
HPG = N_HEADS // N_KV
D_ATT = N_HEADS * HEAD_DIM
KV_W = N_KV * HEAD_DIM
CMP_BLOCK = 32
CMP_STRIDE = 16
CMP_HIDDEN = 128
SLC_BLOCK = 64
N_SEL = 16
WINDOW = 512
S5_GROUP = 16
S5_STATE = 64
D_SSM = 512
N_SSM_GROUPS = D_SSM // S5_GROUP
N_BUCKETS = 32
MAX_DISTANCE = 128
LN_EPS = 1e-5
IN_WIDTHS = (D_ATT, KV_W, KV_W, KV_W, KV_W, KV_W, KV_W, 3 * N_HEADS, D_ATT, D_SSM, D_SSM, 2 * D_MODEL)

LANES = 128
SUBLANES = 8
VMEM_LIMIT = 56 * 1024 * 1024
NEG = -1e30
TQ = 128
TK = 128
KV2 = 2 * KV_W
GATE_PAD = LANES
S5_CHUNK = 16
SEQ_COLS = D_ATT + 3 * KV2 + GATE_PAD + D_SSM
QCOLS = N_HEADS * TQ
GCOLS = HPG * TQ
PAGES_PER_STEP = 32


def _cparams(sem):
    return pltpu.CompilerParams(dimension_semantics=sem, vmem_limit_bytes=VMEM_LIMIT)


def _dot(a, b):
    return jnp.dot(a, b, preferred_element_type=F32)


def _dot_nt(a, b):
    return lax.dot_general(a, b, (((1,), (1,)), ((), ())), preferred_element_type=F32)


def _split3(x):
    h1 = x.astype(BF16)
    r1 = x - h1.astype(F32)
    h2 = r1.astype(BF16)
    h3 = (r1 - h2.astype(F32)).astype(BF16)
    return h1, h2, h3


def _bucket_np(dist):
    n = np.maximum(dist, 0)
    max_exact = N_BUCKETS // 2
    nf = np.maximum(n, 1).astype(np.float32)
    val = (np.log(nf / np.float32(max_exact)) / np.float32(math.log(MAX_DISTANCE / max_exact))
           * np.float32(N_BUCKETS - max_exact))
    large = max_exact + val.astype(np.int32)
    return np.where(n < max_exact, n, np.minimum(large, N_BUCKETS - 1)).astype(np.int32)


def _pair_perm():
    cols = []
    for r in range(HPG):
        cols += list(range(r * HEAD_DIM, (r + 1) * HEAD_DIM))
        cols += list(range((HPG + r) * HEAD_DIM, (HPG + r + 1) * HEAD_DIM))
    return np.asarray(cols, np.int32)


def _imp_matrix(n_cmp_pad, n_blk_pad, n_cmp):
    ratio = SLC_BLOCK // CMP_STRIDE
    n = np.arange(n_cmp_pad)[:, None]
    j = np.arange(n_blk_pad)[None, :]
    m = ((n >= ratio * j - 1) & (n <= ratio * j + ratio - 2)).astype(np.float32)
    m += ((n >= ratio * j) & (n <= ratio * j + ratio - 1)).astype(np.float32)
    m *= (n < n_cmp)
    return m


def _bias_tables(rel_bias, specs):
    idx, ok, shapes = [], [], []
    for dist, head, valid in specs:
        dist, head, valid = np.broadcast_arrays(dist, head, valid)
        shapes.append(dist.shape)
        idx.append((_bucket_np(dist) * N_HEADS + head).reshape(-1))
        ok.append(valid.reshape(-1))
    flat = jnp.take(rel_bias.reshape(-1), jnp.asarray(np.concatenate(idx).astype(np.int32)))
    flat = jnp.where(jnp.asarray(np.concatenate(ok)), flat, NEG)
    out, o = [], 0
    for shp in shapes:
        n = int(np.prod(shp))
        out.append(flat[o:o + n].reshape(shp))
        o += n
    return out


def _inproj_kernel(x_ref, w_ref, q_ref, kvc_ref, kvs_ref, kvw_ref, kvct_ref, kvst_ref, kvwt_ref, g_ref, u_ref):
    h = _dot(x_ref[...].astype(BF16), w_ref[...])
    o = 0
    q_ref[...] = h[:, o:o + D_ATT] * (HEAD_DIM ** -0.5)
    o += D_ATT
    for row_ref, t_ref in ((kvc_ref, kvct_ref), (kvs_ref, kvst_ref), (kvw_ref, kvwt_ref)):
        kv = h[:, o:o + KV2]
        row_ref[...] = kv
        t_ref[0] = kv.T
        o += KV2
    g_ref[...] = jax.nn.sigmoid(h[:, o:o + GATE_PAD])
    o += GATE_PAD
    u_ref[...] = h[:, o:o + D_SSM]


def _inproj(x2d, w_seq, rows_per_seq):
    m = x2d.shape[0]
    tm = min(512, m)
    tps = rows_per_seq // tm
    tok = lambda w: pl.BlockSpec((tm, w), lambda i: (i, 0))
    ft = pl.BlockSpec((1, KV2, tm), lambda i: (i // tps, 0, i % tps))
    tok_shape = lambda w: jax.ShapeDtypeStruct((m, w), F32)
    ft_shape = jax.ShapeDtypeStruct((m // rows_per_seq, KV2, rows_per_seq), F32)
    return pl.pallas_call(
        _inproj_kernel,
        grid=(m // tm,),
        in_specs=[tok(D_MODEL), pl.BlockSpec((D_MODEL, SEQ_COLS), lambda i: (0, 0))],
        out_specs=[tok(D_ATT), tok(KV2), tok(KV2), tok(KV2), ft, ft, ft, tok(GATE_PAD), tok(D_SSM)],
        out_shape=[tok_shape(D_ATT), tok_shape(KV2), tok_shape(KV2), tok_shape(KV2), ft_shape, ft_shape, ft_shape,
                   tok_shape(GATE_PAD), tok_shape(D_SSM)],
        compiler_params=_cparams(("parallel",)),
        name="inproj",
    )(x2d, w_seq)


def _compress_out(hf, hs_next, c1, w2_ref, b2_ref):
    h = hf + hs_next + c1
    return _dot(jax.nn.silu(h).astype(BF16), w2_ref[...]) + b2_ref[...]


def _compress_const(pos_ref, w1_ref, b1_ref):
    hh = _dot(pos_ref[...].astype(BF16), w1_ref[...])
    nh = 4 * CMP_HIDDEN
    return hh[0:1, :nh] + hh[1:2, nh:] + b1_ref[...]


def _softmax_cols(st):
    m = jnp.max(st, axis=0, keepdims=True)
    valid = m > 0.5 * NEG
    e = jnp.exp(st - m)
    s = jnp.sum(e, axis=0, keepdims=True)
    return jnp.where(valid, e / s, 0.0)


def _pv_groups(vt_ref, k0, tk, e_bf):
    v0 = vt_ref[0:HEAD_DIM, pl.ds(k0, tk)]
    v1 = vt_ref[HEAD_DIM:KV_W, pl.ds(k0, tk)]
    return jnp.concatenate([_dot(v0, e_bf[:, :GCOLS]), _dot(v1, e_bf[:, GCOLS:])], axis=1)


def _flash_init(m_ref, l_ref, acc_ref):
    m_ref[...] = jnp.full(m_ref.shape, NEG, F32)
    l_ref[...] = jnp.zeros(l_ref.shape, F32)
    acc_ref[...] = jnp.zeros(acc_ref.shape, F32)


def _flash_update(st, vt_ref, k0, tk, m_ref, l_ref, acc_ref):
    m_old = m_ref[...]
    m_new = jnp.maximum(m_old, jnp.max(st, axis=0, keepdims=True))
    alpha = jnp.exp(m_old - m_new)
    e = jnp.exp(st - m_new)
    l_ref[...] = alpha * l_ref[...] + jnp.sum(e, axis=0, keepdims=True)
    acc_ref[...] = alpha * acc_ref[...] + _pv_groups(vt_ref, k0, tk, e.astype(BF16))
    m_ref[...] = m_new


def _flash_coef(m_ref, l_ref):
    valid = m_ref[...] > 0.5 * NEG
    return jnp.where(valid, 1.0 / jnp.where(valid, l_ref[...], 1.0), 0.0)


def _nsa_prompt_kernel(q_ref, kck_ref, kcv_ref, ks_ref, kw_ref, vst_ref, vwt_ref, g_ref,
                       w1_ref, pos_ref, b1_ref, w2_ref, b2_ref,
                       cmaster_ref, t0_ref, t1_ref, t4_ref, c31_ref, mimp_ref, eblk_ref,
                       o_ref,
                       kc_scr, vct_scr, ks_scr, vs_scr, kw_scr, vw_scr, neg_scr, m_scr, l_scr, acc_scr, tot_scr):
    qt = pl.program_id(1)
    n_qt = pl.num_programs(1)
    seq = ks_ref.shape[1]
    n_ch = seq // CMP_STRIDE
    n_blk = seq // SLC_BLOCK
    nh = 4 * CMP_HIDDEN

    @pl.when(qt == 0)
    def _per_batch():
        x = jnp.concatenate([half[0, pl.ds(p, n_ch, stride=CMP_STRIDE), :]
                             for p in range(CMP_STRIDE) for half in (kck_ref, kcv_ref)],
                            axis=1).astype(BF16)
        hh = _dot(x, w1_ref[...])
        c1 = _compress_const(pos_ref, w1_ref, b1_ref)
        hs_next = pltpu.roll(hh[:, nh:], n_ch - 1, 0)
        kv = _compress_out(hh[:, :nh], hs_next, c1, w2_ref, b2_ref)
        kc_scr[...] = kv[:, :KV_W].astype(BF16)
        vct_scr[...] = kv[:, KV_W:].T.astype(BF16)
        ks_scr[...] = ks_ref[0].astype(BF16)
        kw_scr[...] = kw_ref[0].astype(BF16)
        vs_scr[...] = vst_ref[0].astype(BF16)
        vw_scr[...] = vwt_ref[0].astype(BF16)

    q = q_ref[0]
    gt = g_ref[0].T
    lane = lax.broadcasted_iota(jnp.int32, (TQ, LANES), 1)
    parts = []
    for g in range(N_KV):
        gmask = (lane < HEAD_DIM) if g == 0 else (lane >= HEAD_DIM)
        parts += [jnp.where(gmask, q[:, r * LANES:(r + 1) * LANES], 0.0) for r in range(HPG)]
    qb = jnp.concatenate(parts, axis=0).astype(BF16)
    q0 = qt * TQ

    def gate_cols(j):
        return jnp.concatenate([gt[3 * h + j:3 * h + j + 1, :] for h in range(N_HEADS)], axis=1)

    r0 = pl.multiple_of((n_qt - 1 - qt) * (TQ // CMP_STRIDE), SUBLANES)
    sc = _dot_nt(kc_scr[...], qb) + cmaster_ref[pl.ds(r0, n_ch), :]
    pt = _softmax_cols(sc)
    pb = pt.astype(BF16)
    oc = jnp.concatenate([_dot(vct_scr[0:HEAD_DIM, :], pb[:, :GCOLS]),
                          _dot(vct_scr[HEAD_DIM:KV_W, :], pb[:, GCOLS:])], axis=1)
    tot_scr[...] = oc * gate_cols(0)

    psum = []
    for g in range(N_KV):
        acc = pt[:, g * GCOLS:g * GCOLS + TQ]
        for r in range(1, HPG):
            acc = acc + pt[:, g * GCOLS + r * TQ:g * GCOLS + (r + 1) * TQ]
        psum.append(acc)
    psum = jnp.concatenate(psum, axis=1)
    p1, p2, p3 = _split3(psum)
    imp = _dot(mimp_ref[...], p1) + _dot(mimp_ref[...], p2) + _dot(mimp_ref[...], p3)
    shape = (n_blk, N_KV * TQ)
    bj = lax.broadcasted_iota(jnp.int32, shape, 0)
    tpos = q0 + lax.broadcasted_iota(jnp.int32, shape, 1) % TQ
    cur = tpos // SLC_BLOCK
    forced = (bj == 0) | (bj == cur) | (bj == cur - 1)
    visible = bj * SLC_BLOCK <= tpos
    score = jnp.where(forced, -NEG, jnp.where(visible, imp, NEG))
    rank = jnp.zeros(shape, F32)
    for i in range(n_blk):
        si = score[i:i + 1, :]
        ahead = (si > score) | ((si == score) & (bj > i))
        rank = rank + jnp.where(ahead, 1.0, 0.0)
    sel = jnp.where(rank < float(N_SEL), 1.0, 0.0).astype(BF16)
    neg_scr[...] = (_dot(eblk_ref[...], sel) - 1.0) * (-NEG)

    _flash_init(m_scr, l_scr, acc_scr)

    def sel_block(k0, tk, bias):
        st = _dot_nt(ks_scr[pl.ds(k0, tk), :], qb) + bias
        ng = neg_scr[pl.ds(k0, tk), :]
        st = st + jnp.concatenate([ng[:, :TQ]] * HPG + [ng[:, TQ:]] * HPG, axis=1)
        _flash_update(st, vs_scr, k0, tk, m_scr, l_scr, acc_scr)

    n_far = jnp.maximum(qt - 1, 0)

    def far_body(j, carry):
        sel_block(pl.multiple_of(j * 2 * TK, 2 * TK), 2 * TK, c31_ref[...])
        return carry

    lax.fori_loop(0, n_far // 2, far_body, 0)

    @pl.when(n_far % 2 == 1)
    def _far_tail():
        sel_block(pl.multiple_of((n_far - 1) * TK, TK), TK, c31_ref[...])

    @pl.when(qt >= 1)
    def _near():
        sel_block(pl.multiple_of((qt - 1) * TK, TK), TK, t1_ref[...])

    sel_block(pl.multiple_of(qt * TK, TK), TK, t0_ref[...])
    tot_scr[...] = tot_scr[...] + acc_scr[...] * (_flash_coef(m_scr, l_scr) * gate_cols(1))

    _flash_init(m_scr, l_scr, acc_scr)

    def win_block(k0, tk, bias):
        st = _dot_nt(kw_scr[pl.ds(k0, tk), :], qb) + bias
        _flash_update(st, vw_scr, k0, tk, m_scr, l_scr, acc_scr)

    @pl.when(qt >= 4)
    def _w4():
        win_block(pl.multiple_of((qt - 4) * TK, TK), TK, t4_ref[...])

    @pl.when(qt >= 3)
    def _w32():
        win_block(pl.multiple_of((qt - 3) * TK, TK), 2 * TK, c31_ref[...])

    @pl.when(qt == 2)
    def _w2():
        win_block(0, TK, c31_ref[...])

    @pl.when(qt >= 1)
    def _w1():
        win_block(pl.multiple_of((qt - 1) * TK, TK), TK, t1_ref[...])

    win_block(pl.multiple_of(qt * TK, TK), TK, t0_ref[...])
    tot = tot_scr[...] + acc_scr[...] * (_flash_coef(m_scr, l_scr) * gate_cols(2))
    for r in range(HPG):
        blk = jnp.concatenate([tot[:, r * TQ:(r + 1) * TQ], tot[:, GCOLS + r * TQ:GCOLS + (r + 1) * TQ]], axis=0)
        o_ref[0, :, r * LANES:(r + 1) * LANES] = blk.T


def _prompt_table_specs(seq):
    assert WINDOW == 4 * TK and TQ == TK
    n_qt = seq // TQ
    n_ch = seq // CMP_STRIDE
    col = np.arange(QCOLS)
    head = (col // TQ)[None, :]
    ii = (col % TQ)[None, :]
    jj = np.arange(TK)[:, None]
    d0 = ii - jj
    true = np.ones((1, 1), bool)
    far = np.full((1, 1), MAX_DISTANCE)
    assert _bucket_np(np.arange(TK + 1, seq + WINDOW)).min() == N_BUCKETS - 1
    shift = TQ // CMP_STRIDE
    nn = np.arange(n_ch + (n_qt - 1) * shift)[:, None] - (n_qt - 1) * shift
    dc = ii - (nn * CMP_STRIDE + CMP_BLOCK - 1)
    return [(d0, head, d0 >= 0),
            (TK + d0, head, true),
            (far, head, d0 <= 0),
            (far, head, np.ones((1, QCOLS), bool)),
            (dc, head, dc >= 0)]


def _nsa_prompt(q, kvc, kvs, kvw, kvst, kvwt, gates, cw, tables):
    b, seq, _ = q.shape
    n_qt = seq // TQ
    n_ch = seq // CMP_STRIDE
    n_blk = seq // SLC_BLOCK
    t0, t1, t4, c31, cmaster = tables
    mimp = jnp.asarray(_imp_matrix(n_ch, n_blk, n_ch - 1).T, BF16)
    eblk = jnp.asarray((np.arange(seq)[:, None] // SLC_BLOCK == np.arange(n_blk)[None, :]), BF16)

    def full(a):
        nd = a.ndim
        return pl.BlockSpec(a.shape, lambda i, j, nd=nd: (0,) * nd)

    k_spec = pl.BlockSpec((1, seq, KV_W), lambda i, j: (i, 0, 0))
    vt_spec = pl.BlockSpec((1, KV_W, seq), lambda i, j: (i, 1, 0))
    return pl.pallas_call(
        _nsa_prompt_kernel,
        grid=(b, n_qt),
        in_specs=[pl.BlockSpec((1, TQ, D_ATT), lambda i, j: (i, j, 0)),
                  k_spec, pl.BlockSpec((1, seq, KV_W), lambda i, j: (i, 0, 1)),
                  k_spec, k_spec, vt_spec, vt_spec,
                  pl.BlockSpec((1, TQ, GATE_PAD), lambda i, j: (i, j, 0)),
                  full(cw["w1"]), full(cw["pos"]), full(cw["b1"]), full(cw["w2"]), full(cw["b2"]),
                  full(cmaster), full(t0), full(t1), full(t4), full(c31), full(mimp), full(eblk)],
        out_specs=pl.BlockSpec((1, TQ, D_ATT), lambda i, j: (i, j, 0)),
        out_shape=jax.ShapeDtypeStruct((b, seq, D_ATT), F32),
        scratch_shapes=[pltpu.VMEM((n_ch, KV_W), BF16), pltpu.VMEM((KV_W, n_ch), BF16),
                        pltpu.VMEM((seq, KV_W), BF16), pltpu.VMEM((KV_W, seq), BF16),
                        pltpu.VMEM((seq, KV_W), BF16), pltpu.VMEM((KV_W, seq), BF16),
                        pltpu.VMEM((seq, N_KV * TQ), F32),
                        pltpu.VMEM((1, QCOLS), F32), pltpu.VMEM((1, QCOLS), F32),
                        pltpu.VMEM((HEAD_DIM, QCOLS), F32), pltpu.VMEM((HEAD_DIM, QCOLS), F32)],
        compiler_params=_cparams(("arbitrary", "arbitrary")),
        name="nsa_prompt",
    )(q, kvc, kvc, kvs, kvw, kvst, kvwt, gates, cw["w1"], cw["pos"], cw["b1"], cw["w2"], cw["b2"],
      cmaster, t0, t1, t4, c31, mimp, eblk)


def _s5_kernel(u_ref, h0_ref, v_ref, m_ref, wh_ref, a1_ref, a2_ref, y_ref, ht_ref, z_scr, *, n_chunks, bt):
    gc = u_ref.shape[1]
    sw = 2 * S5_STATE
    for g in range(gc):
        z_scr[:, g * sw:(g + 1) * sw] = _dot(u_ref[0, g], v_ref[g])
    a1 = a1_ref[...]
    a2 = a2_ref[...]

    def step(k, h):
        r0 = pl.multiple_of(k * bt, bt)
        z = z_scr[pl.ds(r0, bt), :]
        z_scr[pl.ds(r0, bt), :] = h
        hsw = jnp.concatenate(
            [pltpu.roll(h[:, g * sw:(g + 1) * sw], S5_STATE, 1) for g in range(gc)], axis=1)
        return a1 * h + a2 * hsw + z

    ht_ref[0] = lax.fori_loop(0, n_chunks, step, h0_ref[0])
    for g in range(gc):
        y_ref[0, g] = (_dot(u_ref[0, g], m_ref[g])
                       + _dot(z_scr[:, g * sw:(g + 1) * sw].astype(BF16), wh_ref[g]))


def _s5_mats(a_re, a_im, log_dt, b_re, b_im, c_re, c_im, chunk):
    hp = lax.Precision.HIGHEST
    dt = jnp.exp(log_dt)[:, None]
    mag = jnp.exp(dt * a_re)
    ab_re, ab_im = mag * jnp.cos(dt * a_im), mag * jnp.sin(dt * a_im)
    den = a_re * a_re + a_im * a_im
    x_re, x_im = ab_re - 1.0, ab_im
    k_re = (x_re * a_re + x_im * a_im) / den
    k_im = (x_im * a_re - x_re * a_im) / den
    bb_re = k_re[..., None] * b_re - k_im[..., None] * b_im
    bb_im = k_re[..., None] * b_im + k_im[..., None] * b_re
    pw_re, pw_im = [jnp.ones_like(ab_re)], [jnp.zeros_like(ab_im)]
    for _ in range(chunk):
        pr, pi = pw_re[-1], pw_im[-1]
        pw_re.append(pr * ab_re - pi * ab_im)
        pw_im.append(pr * ab_im + pi * ab_re)
    pw_re, pw_im = jnp.stack(pw_re), jnp.stack(pw_im)
    cl_re = c_re[None] * pw_re[:, :, None, :] - c_im[None] * pw_im[:, :, None, :]
    cl_im = c_re[None] * pw_im[:, :, None, :] + c_im[None] * pw_re[:, :, None, :]
    t = (jnp.einsum('tgcp,gpd->tgcd', cl_re, bb_re, precision=hp)
         - jnp.einsum('tgcp,gpd->tgcd', cl_im, bb_im, precision=hp))
    s_i = np.arange(chunk)[:, None]
    t_i = np.arange(chunk)[None, :]
    lag = np.clip(t_i - s_i, 0, None)
    blk = jnp.where((t_i >= s_i)[:, :, None, None, None], t[lag], 0.0)
    ng = a_re.shape[0]
    m = blk.transpose(2, 0, 4, 1, 3).reshape(ng, chunk * S5_GROUP, chunk * S5_GROUP)
    rev = chunk - 1 - np.arange(chunk)
    vb_re = pw_re[rev][:, :, :, None] * bb_re[None] - pw_im[rev][:, :, :, None] * bb_im[None]
    vb_im = pw_re[rev][:, :, :, None] * bb_im[None] + pw_im[rev][:, :, :, None] * bb_re[None]
    v = jnp.concatenate([vb_re, vb_im], axis=2).transpose(1, 0, 3, 2).reshape(ng, chunk * S5_GROUP, 2 * S5_STATE)
    wh = jnp.concatenate([cl_re[1:], -cl_im[1:]], axis=3)
    wh = wh.transpose(1, 3, 0, 2).reshape(ng, 2 * S5_STATE, chunk * S5_GROUP)
    a1 = jnp.concatenate([pw_re[chunk], pw_re[chunk]], axis=1).reshape(1, ng * 2 * S5_STATE)
    a2 = jnp.concatenate([-pw_im[chunk], pw_im[chunk]], axis=1).reshape(1, ng * 2 * S5_STATE)
    return v.astype(BF16), m.astype(BF16), wh.astype(BF16), a1, a2


def _s5(u, h0_re, h0_im, mats, chunk, bt, gc=8):
    b, t_len, _ = u.shape
    ng = N_SSM_GROUPS
    n_chunks = t_len // chunk
    nbc = b // bt
    lc = chunk * S5_GROUP
    rows = n_chunks * bt
    sw = 2 * S5_STATE
    v, m, wh, a1, a2 = mats
    ur = u.reshape(nbc, bt, n_chunks, chunk, ng, S5_GROUP).transpose(0, 4, 2, 1, 3, 5)
    ur = ur.reshape(nbc, ng, rows, lc).astype(BF16)
    h0 = jnp.concatenate([h0_re, h0_im], axis=-1).reshape(nbc, bt, ng * sw)
    kern = functools.partial(_s5_kernel, n_chunks=n_chunks, bt=bt)
    y, ht = pl.pallas_call(
        kern,
        grid=(nbc, ng // gc),
        in_specs=[pl.BlockSpec((1, gc, rows, lc), lambda i, j: (i, j, 0, 0)),
                  pl.BlockSpec((1, bt, gc * sw), lambda i, j: (i, 0, j)),
                  pl.BlockSpec((gc, lc, sw), lambda i, j: (j, 0, 0)),
                  pl.BlockSpec((gc, lc, lc), lambda i, j: (j, 0, 0)),
                  pl.BlockSpec((gc, sw, lc), lambda i, j: (j, 0, 0)),
                  pl.BlockSpec((1, gc * sw), lambda i, j: (0, j)),
                  pl.BlockSpec((1, gc * sw), lambda i, j: (0, j))],
        out_specs=[pl.BlockSpec((1, gc, rows, lc), lambda i, j: (i, j, 0, 0)),
                   pl.BlockSpec((1, bt, gc * sw), lambda i, j: (i, 0, j))],
        out_shape=[jax.ShapeDtypeStruct((nbc, ng, rows, lc), F32),
                   jax.ShapeDtypeStruct((nbc, bt, ng * sw), F32)],
        scratch_shapes=[pltpu.VMEM((rows, gc * sw), F32)],
        compiler_params=_cparams(("parallel", "parallel")),
        name="s5_scan",
    )(ur, h0, v, m, wh, a1, a2)
    y = y.reshape(nbc, ng, n_chunks, bt, chunk, S5_GROUP).transpose(0, 3, 2, 4, 1, 5).reshape(b, t_len, D_SSM)
    ht = ht.reshape(b, ng, sw)
    return y, ht[..., :S5_STATE], ht[..., S5_STATE:]


def _epilogue_kernel(x_ref, o_ref, y_ref, u_ref, wz_ref, d_ref, glu0_ref, glu1_ref, gb0_ref, gb1_ref,
                     wa_ref, ws_ref, wo_ref, lng_ref, lnb_ref, out_ref, *, alpha):
    x = x_ref[...]
    z = _dot(x.astype(BF16), wz_ref[...])
    za = z[:, :D_ATT]
    zs = z[:, D_ATT:D_ATT + D_SSM]
    ga = z[:, D_ATT + D_SSM:D_ATT + D_SSM + D_MODEL]
    gs = z[:, D_ATT + D_SSM + D_MODEL:]
    p_att = _dot((o_ref[...] * jax.nn.silu(za)).astype(BF16), wa_ref[...])
    y = jax.nn.gelu(y_ref[...] + d_ref[...] * u_ref[...]).astype(BF16)
    s = (_dot(y, glu0_ref[...]) + gb0_ref[...]) * jax.nn.sigmoid(_dot(y, glu1_ref[...]) + gb1_ref[...])
    p_ssm = _dot((s * jax.nn.silu(zs)).astype(BF16), ws_ref[...])
    merged = jax.nn.sigmoid(ga) * p_att + jax.nn.sigmoid(gs) * p_ssm
    r = alpha * x + _dot(merged.astype(BF16), wo_ref[...])
    mu = jnp.mean(r, axis=-1, keepdims=True)
    c = r - mu
    var = jnp.mean(c * c, axis=-1, keepdims=True)
    out_ref[...] = c * lax.rsqrt(var + LN_EPS) * lng_ref[...] + lnb_ref[...]


def _epilogue(x2d, o2d, y2d, u2d, ew, alpha):
    m = x2d.shape[0]
    tm = min(256, m)
    names = ("wz", "d", "glu0", "glu1", "gb0", "gb1", "wa", "ws", "wo", "lng", "lnb")
    ws = [ew[n] for n in names]

    def tok(w):
        return pl.BlockSpec((tm, w), lambda i: (i, 0))

    return pl.pallas_call(
        functools.partial(_epilogue_kernel, alpha=alpha),
        grid=(m // tm,),
        in_specs=[tok(D_MODEL), tok(D_ATT), tok(D_SSM), tok(D_SSM)]
        + [pl.BlockSpec(w.shape, lambda i: (0, 0)) for w in ws],
        out_specs=tok(D_MODEL),
        out_shape=jax.ShapeDtypeStruct((m, D_MODEL), F32),
        compiler_params=_cparams(("parallel",)),
        name="epilogue",
    )(x2d, o2d, y2d, u2d, *ws)


def _layer_weights(l, w_in, cmp_w1, cmp_b1, cmp_w2, cmp_b2, cmp_pos, ssm_d, ssm_glu_w, ssm_glu_b,
                   w_att_out, w_ssm_out, w_o, ln_g, ln_b):
    offs = np.concatenate([[0], np.cumsum(IN_WIDTHS)])
    col = lambda i: w_in[l][:, offs[i]:offs[i + 1]]
    perm = _pair_perm()
    gate = jnp.pad(col(7), ((0, 0), (0, GATE_PAD - 3 * N_HEADS)))
    w_seq = jnp.concatenate([col(0)[:, perm], col(1), col(2), col(3), col(4), col(5), col(6), gate, col(9)],
                            axis=1).astype(BF16)
    wz = jnp.concatenate([col(8)[:, perm], col(10), col(11)], axis=1).astype(BF16)

    half = CMP_BLOCK // 2
    eye = jnp.eye(4, dtype=F32)
    w1s = jnp.stack([cmp_w1[l, 0], cmp_w1[l, 0], cmp_w1[l, 1], cmp_w1[l, 1]])
    bd = jnp.einsum('spdh,st->psdth', w1s, eye)
    bd = bd.reshape(CMP_BLOCK, 4 * HEAD_DIM, 4 * CMP_HIDDEN)
    w1 = jnp.concatenate([bd[:half].reshape(half * 4 * HEAD_DIM, -1),
                          bd[half:].reshape(half * 4 * HEAD_DIM, -1)], axis=1).astype(BF16)
    pos_s = jnp.stack([cmp_pos[l, 0], cmp_pos[l, 0], cmp_pos[l, 1], cmp_pos[l, 1]], axis=1)
    pos = jnp.zeros((SUBLANES, half * 4 * HEAD_DIM), F32)
    pos = pos.at[0].set(pos_s[:half].reshape(-1)).at[1].set(pos_s[half:].reshape(-1))
    b1 = jnp.concatenate([cmp_b1[l, 0], cmp_b1[l, 0], cmp_b1[l, 1], cmp_b1[l, 1]]).reshape(1, -1)
    w2s = jnp.stack([cmp_w2[l, 0], cmp_w2[l, 0], cmp_w2[l, 1], cmp_w2[l, 1]])
    w2 = jnp.einsum('shd,st->shtd', w2s, eye).reshape(4 * CMP_HIDDEN, 4 * HEAD_DIM).astype(BF16)
    b2 = jnp.concatenate([cmp_b2[l, 0], cmp_b2[l, 0], cmp_b2[l, 1], cmp_b2[l, 1]]).reshape(1, -1)
    cw = dict(w1=w1, pos=pos, b1=b1, w2=w2, b2=b2)

    ew = dict(wz=wz, d=ssm_d[l].reshape(1, D_SSM),
              glu0=ssm_glu_w[l, 0].astype(BF16), glu1=ssm_glu_w[l, 1].astype(BF16),
              gb0=ssm_glu_b[l, 0].reshape(1, -1), gb1=ssm_glu_b[l, 1].reshape(1, -1),
              wa=w_att_out[l][perm].astype(BF16), ws=w_ssm_out[l].astype(BF16), wo=w_o[l].astype(BF16),
              lng=ln_g[l].reshape(1, -1), lnb=ln_b[l].reshape(1, -1))
    return w_seq, cw, ew


def _prompt_layer(h_p, lw, s5m, tables, alpha):
    b, seq, _ = h_p.shape
    w_seq, cw, ew = lw
    x2d = h_p.reshape(b * seq, D_MODEL)
    q, kvc, kvs, kvw, kvct, kvst, kvwt, gates, u = _inproj(x2d, w_seq, seq)
    r3 = lambda a: a.reshape(b, seq, a.shape[-1])
    o_att = _nsa_prompt(r3(q), r3(kvc), r3(kvs), r3(kvw), kvst, kvwt, r3(gates), cw, tables)
    zero = jnp.zeros((b, N_SSM_GROUPS, S5_STATE), F32)
    y, hr, hi = _s5(r3(u), zero, zero, s5m, S5_CHUNK, bt=SUBLANES)
    out = _epilogue(x2d, o_att.reshape(b * seq, D_ATT), y.reshape(b * seq, D_SSM), u, ew, alpha)
    return out.reshape(b, seq, D_MODEL), kvct, kvst, kvwt, hr, hi


def _gather_step(page_dma):
    b, c = pl.program_id(0), pl.program_id(1)
    n_c = pl.num_programs(1)
    step = b * n_c + c
    total = pl.num_programs(0) * n_c
    slot = step % 2

    @pl.when(step == 0)
    def _first():
        for j in range(PAGES_PER_STEP):
            page_dma(b, c, j, slot).start()

    @pl.when(step + 1 < total)
    def _prefetch():
        nxt = step + 1
        for j in range(PAGES_PER_STEP):
            page_dma(nxt // n_c, nxt % n_c, j, 1 - slot).start()

    for j in range(PAGES_PER_STEP):
        page_dma(b, c, j, slot).wait()
    return slot


def _sample_queries(q):
    t = q.shape[0]
    lane = lax.broadcasted_iota(jnp.int32, (t, LANES), 1)
    parts = []
    for g in range(N_KV):
        gmask = (lane < HEAD_DIM) if g == 0 else (lane >= HEAD_DIM)
        parts += [jnp.where(gmask, q[:, r * LANES:(r + 1) * LANES], 0.0) for r in range(HPG)]
    return jnp.concatenate(parts, axis=0).astype(BF16)


def _softmax_rows(s):
    m = jnp.max(s, axis=1, keepdims=True)
    valid = m > 0.5 * NEG
    e = jnp.exp(s - m)
    return jnp.where(valid, e / jnp.sum(e, axis=1, keepdims=True), 0.0)


def _cmp_sample_kernel(pt_ref, cache_hbm, q_ref, perm_ref, w1_ref, pos_ref, b1_ref, w2_ref, b2_ref, biasc_ref,
                       mimp_ref, oc_ref, sel_ref, buf, sem, x_scr, hf_scr, hs_scr,
                       *, layer, t_len, past_len, n_blk):
    c = pl.program_id(1)
    n_c = pl.num_programs(1)
    nh = 4 * CMP_HIDDEN
    cpp = buf.shape[3] // CMP_STRIDE
    rows = PAGES_PER_STEP * cpp

    def page_dma(b, cc, j, slot):
        page = pt_ref[b, cc * PAGES_PER_STEP + j]
        return pltpu.make_async_copy(cache_hbm.at[layer, page], buf.at[slot, j], sem.at[slot])

    slot = _gather_step(page_dma)

    def regroup(j, carry):
        xp = _dot_nt(perm_ref[...], buf[slot, j].astype(BF16))
        r0 = pl.multiple_of(j * cpp, cpp)
        for p in range(CMP_STRIDE):
            x_scr[pl.ds(r0, cpp), p * KV2:(p + 1) * KV2] = xp[p * cpp:(p + 1) * cpp, :]
        return carry

    lax.fori_loop(0, PAGES_PER_STEP, regroup, 0)
    hh = _dot(x_scr[...].astype(BF16), w1_ref[...])
    r0 = pl.multiple_of(c * rows, rows)
    hf_scr[pl.ds(r0, rows), :] = hh[:, :nh]
    hs_scr[pl.ds(r0, rows), :] = hh[:, nh:]

    @pl.when(c == n_c - 1)
    def _finish():
        n_ch = hf_scr.shape[0]
        c1 = _compress_const(pos_ref, w1_ref, b1_ref)
        hs_next = pltpu.roll(hs_scr[...], n_ch - 1, 0)
        kv = _compress_out(hf_scr[...], hs_next, c1, w2_ref, b2_ref)
        q64 = _sample_queries(q_ref[0])
        p = _softmax_rows(_dot_nt(q64, kv[:, :KV_W].astype(BF16)) + biasc_ref[...])
        oc_ref[0] = _dot(p.astype(BF16), kv[:, KV_W:].astype(BF16))
        psum = []
        for g in range(N_KV):
            acc = p[g * HPG * t_len:g * HPG * t_len + t_len]
            for r in range(1, HPG):
                acc = acc + p[(g * HPG + r) * t_len:(g * HPG + r + 1) * t_len]
            psum.append(acc)
        psum = jnp.concatenate(psum, axis=0)
        p1, p2, p3 = _split3(psum)
        imp = _dot(p1, mimp_ref[...]) + _dot(p2, mimp_ref[...]) + _dot(p3, mimp_ref[...])
        shape = imp.shape
        bj = lax.broadcasted_iota(jnp.int32, shape, 1)
        tpos = past_len + lax.broadcasted_iota(jnp.int32, shape, 0) % t_len
        cur = tpos // SLC_BLOCK
        forced = (bj == 0) | (bj == cur) | (bj == cur - 1)
        visible = bj * SLC_BLOCK <= tpos
        score = jnp.where(forced, -NEG, jnp.where(visible, imp, NEG))
        score = jnp.where(bj < n_blk, score, 2.0 * NEG)
        sel = jnp.zeros(shape, F32)
        for _ in range(min(N_SEL, n_blk)):
            best = jnp.max(score, axis=1, keepdims=True)
            first = jnp.min(jnp.where(score == best, bj, shape[1]), axis=1, keepdims=True)
            hit = bj == first
            sel = jnp.where(hit, 1.0, sel)
            score = jnp.where(hit, 3.0 * NEG, score)
        sel_ref[0] = sel


def _slc_sample_kernel(pt_ref, cache_hbm, q_ref, sel_ref, knew_ref, bias_ref, c31_ref, biasn_ref, eblk_ref,
                       os_ref, buf, sem, m_scr, l_scr, acc_scr, *, layer):
    c = pl.program_id(1)
    n_c = pl.num_programs(1)
    page = buf.shape[2] // PAGES_PER_STEP

    def page_dma(b, cc, j, slot):
        pg = pt_ref[b, cc * PAGES_PER_STEP + j]
        return pltpu.make_async_copy(cache_hbm.at[layer, pg], buf.at[slot, :, pl.ds(j * page, page)], sem.at[slot])

    slot = _gather_step(page_dma)
    q64 = _sample_queries(q_ref[0])

    @pl.when(c == 0)
    def _init():
        m_scr[...] = jnp.full(m_scr.shape, NEG, F32)
        l_scr[...] = jnp.zeros(l_scr.shape, F32)
        acc_scr[...] = jnp.zeros(acc_scr.shape, F32)

    def update(s, vt_bf):
        m_old = m_scr[...]
        m_new = jnp.maximum(m_old, jnp.max(s, axis=1, keepdims=True))
        alpha = jnp.exp(m_old - m_new)
        e = jnp.exp(s - m_new)
        l_scr[...] = alpha * l_scr[...] + jnp.sum(e, axis=1, keepdims=True)
        acc_scr[...] = alpha * acc_scr[...] + _dot_nt(e.astype(BF16), vt_bf)
        m_scr[...] = m_new

    neg = (_dot(sel_ref[0, 0].astype(BF16), eblk_ref[...]) - 1.0) * (-NEG)
    s = _dot(q64, buf[slot, 0:KV_W, :].astype(BF16)) + neg

    @pl.when(c < n_c - 1)
    def _far():
        update(s + c31_ref[...], buf[slot, KV_W:KV2, :].astype(BF16))

    @pl.when(c == n_c - 1)
    def _last():
        update(s + bias_ref[...], buf[slot, KV_W:KV2, :].astype(BF16))
        kn = knew_ref[0]
        update(_dot(q64, kn[0:KV_W, :].astype(BF16)) + biasn_ref[...], kn[KV_W:KV2, :].astype(BF16))
        valid = m_scr[...] > 0.5 * NEG
        os_ref[0] = jnp.where(valid, acc_scr[...] / jnp.where(valid, l_scr[...], 1.0), 0.0)


def _win_sample_kernel(q_ref, kvw_ref, biasw_ref, g_ref, oc_ref, os_ref, o_ref, *, t_len):
    q64 = _sample_queries(q_ref[0])
    kw = kvw_ref[0]
    p = _softmax_rows(_dot(q64, kw[0:KV_W, :].astype(BF16)) + biasw_ref[...])
    ow = _dot_nt(p.astype(BF16), kw[KV_W:KV2, :].astype(BF16))
    gates = g_ref[0]
    oc, osel = oc_ref[0], os_ref[0]
    lane = lax.broadcasted_iota(jnp.int32, (t_len, LANES), 1)

    def head(g, r):
        h = g * HPG + r
        sl = slice(h * t_len, (h + 1) * t_len)
        return (gates[:, 3 * h:3 * h + 1] * oc[sl] + gates[:, 3 * h + 1:3 * h + 2] * osel[sl]
                + gates[:, 3 * h + 2:3 * h + 3] * ow[sl])

    for r in range(HPG):
        o_ref[0, :, r * LANES:(r + 1) * LANES] = jnp.where(lane < HEAD_DIM, head(0, r), head(1, r))


def _sample_table_specs(past_len, t_len, w_buf, w_pad, last_rows):
    n_tok = past_len + t_len
    row = np.arange(N_HEADS * t_len)
    head = (row // t_len)[:, None]
    qpos = (past_len + row % t_len)[:, None]
    n_ch = past_len // CMP_STRIDE
    nn = np.arange(n_ch)[None, :]
    dc = qpos - (nn * CMP_STRIDE + CMP_BLOCK - 1)
    ok_c = (dc >= 0) & (nn < n_tok // CMP_STRIDE - 1)
    ds = qpos - np.arange(past_len - last_rows, past_len)[None, :]
    assert _bucket_np(np.arange(last_rows, n_tok + 1)).min() == N_BUCKETS - 1
    ll = np.arange(LANES)[None, :]
    dn = qpos - (past_len + ll)
    ok_n = (dn >= 0) & (ll < t_len)
    ii = np.arange(w_pad)[None, :]
    w_pos = n_tok - (w_buf + t_len) + ii
    dw = qpos - w_pos
    ok_w = (dw >= 0) & (dw <= WINDOW) & (w_pos >= 0) & (ii < w_buf + t_len)
    return [(dc, head, ok_c),
            (ds, head, np.ones((1, 1), bool)),
            (np.full((1, 1), MAX_DISTANCE), head, np.ones((1, LANES), bool)),
            (dn, head, ok_n),
            (dw, head, ok_w)]


def _nsa_sample(layer, q, gates, kvs_new_t, all_w_t, cmp_t, slc_t, page_table, cw, tables):
    b, t_len, _ = q.shape
    n_pages = page_table.shape[1]
    page = cmp_t.shape[3]
    past_len = n_pages * page
    n_c = n_pages // PAGES_PER_STEP
    n_ch = past_len // CMP_STRIDE
    n_blk = -(-(past_len + t_len) // SLC_BLOCK)
    blk_pad = -(-n_blk // LANES) * LANES
    rows = PAGES_PER_STEP * page
    blk_step = rows // SLC_BLOCK
    nq = N_HEADS * t_len
    assert past_len % SLC_BLOCK == 0 and t_len <= SLC_BLOCK and (past_len + t_len) // CMP_STRIDE == n_ch
    biasc, biass, c31, biasn, biasw = tables
    c31 = c31[:, :1]
    mimp = jnp.asarray(_imp_matrix(n_ch, blk_pad, (past_len + t_len) // CMP_STRIDE - 1), BF16)
    cpp = page // CMP_STRIDE
    pm = np.zeros((page, page), np.float32)
    for p in range(CMP_STRIDE):
        for cc in range(cpp):
            pm[p * cpp + cc, cc * CMP_STRIDE + p] = 1.0
    pm = jnp.asarray(pm, BF16)

    def full(a):
        nd = a.ndim
        return pl.BlockSpec(a.shape, lambda i, j, pt, nd=nd: (0,) * nd)

    q_spec = pl.BlockSpec((1, t_len, D_ATT), lambda i, j, pt: (i, 0, 0))
    o_spec = pl.BlockSpec((1, nq, KV_W), lambda i, j, pt: (i, 0, 0))
    any_spec = pl.BlockSpec(memory_space=pl.ANY)
    oc, sel = pl.pallas_call(
        functools.partial(_cmp_sample_kernel, layer=layer, t_len=t_len, past_len=past_len, n_blk=n_blk),
        grid_spec=pltpu.PrefetchScalarGridSpec(
            num_scalar_prefetch=1, grid=(b, n_c),
            in_specs=[any_spec, q_spec, full(pm), full(cw["w1"]), full(cw["pos"]), full(cw["b1"]), full(cw["w2"]),
                      full(cw["b2"]), full(biasc), full(mimp)],
            out_specs=[o_spec, pl.BlockSpec((1, N_KV * t_len, blk_pad), lambda i, j, pt: (i, 0, 0))],
            scratch_shapes=[pltpu.VMEM((2, PAGES_PER_STEP, KV2, page), F32),
                            pltpu.SemaphoreType.DMA((2,)),
                            pltpu.VMEM((PAGES_PER_STEP * cpp, CMP_STRIDE * KV2), F32),
                            pltpu.VMEM((n_ch, 4 * CMP_HIDDEN), F32), pltpu.VMEM((n_ch, 4 * CMP_HIDDEN), F32)]),
        out_shape=[jax.ShapeDtypeStruct((b, nq, KV_W), F32),
                   jax.ShapeDtypeStruct((b, N_KV * t_len, blk_pad), F32)],
        compiler_params=_cparams(("arbitrary", "arbitrary")),
        name="nsa_sample_cmp",
    )(page_table, cmp_t, q, pm, cw["w1"], cw["pos"], cw["b1"], cw["w2"], cw["b2"], biasc, mimp)

    sel_c = sel[:, :, :n_c * blk_step].reshape(b, N_KV, 1, t_len, n_c, blk_step)
    sel_c = jnp.broadcast_to(sel_c, (b, N_KV, HPG, t_len, n_c, blk_step))
    sel_c = sel_c.transpose(0, 4, 1, 2, 3, 5).reshape(b, n_c, nq, blk_step)
    sel_c = jnp.pad(sel_c, ((0, 0), (0, 0), (0, 0), (0, LANES - blk_step)))
    eblk = jnp.asarray(np.arange(LANES)[:, None] == (np.arange(rows)[None, :] // SLC_BLOCK), BF16)
    o_s = pl.pallas_call(
        functools.partial(_slc_sample_kernel, layer=layer),
        grid_spec=pltpu.PrefetchScalarGridSpec(
            num_scalar_prefetch=1, grid=(b, n_c),
            in_specs=[any_spec, q_spec,
                      pl.BlockSpec((1, 1, nq, LANES), lambda i, j, pt: (i, j, 0, 0)),
                      pl.BlockSpec((1, KV2, LANES), lambda i, j, pt: (i, 0, 0)),
                      full(biass), full(c31), full(biasn), full(eblk)],
            out_specs=o_spec,
            scratch_shapes=[pltpu.VMEM((2, KV2, rows), F32),
                            pltpu.SemaphoreType.DMA((2,)),
                            pltpu.VMEM((nq, 1), F32), pltpu.VMEM((nq, 1), F32), pltpu.VMEM((nq, KV_W), F32)]),
        out_shape=jax.ShapeDtypeStruct((b, nq, KV_W), F32),
        compiler_params=_cparams(("arbitrary", "arbitrary")),
        name="nsa_sample_slc",
    )(page_table, slc_t, q, sel_c, kvs_new_t, biass, c31, biasn, eblk)

    w_pad = all_w_t.shape[2]
    return pl.pallas_call(
        functools.partial(_win_sample_kernel, t_len=t_len),
        grid=(b,),
        in_specs=[pl.BlockSpec((1, t_len, D_ATT), lambda i: (i, 0, 0)),
                  pl.BlockSpec((1, KV2, w_pad), lambda i: (i, 0, 0)),
                  pl.BlockSpec(biasw.shape, lambda i: (0, 0)),
                  pl.BlockSpec((1, t_len, GATE_PAD), lambda i: (i, 0, 0)),
                  pl.BlockSpec((1, nq, KV_W), lambda i: (i, 0, 0)),
                  pl.BlockSpec((1, nq, KV_W), lambda i: (i, 0, 0))],
        out_specs=pl.BlockSpec((1, t_len, D_ATT), lambda i: (i, 0, 0)),
        out_shape=jax.ShapeDtypeStruct((b, t_len, D_ATT), F32),
        compiler_params=_cparams(("parallel",)),
        name="nsa_sample_win",
    )(q, all_w_t, biasw, gates, oc, o_s)


def _sample_layer(layer, h_s, lw, s5m, tables, alpha, cmp_t, slc_t, win_t, h0_re, h0_im, page_table, w_pad):
    b, t_len, _ = h_s.shape
    w_seq, cw, ew = lw
    m = b * t_len
    x2d = h_s.reshape(m, D_MODEL)
    q, kvc, kvs, kvw, kvct, kvst, kvwt, gates, u = _inproj(x2d, w_seq, m)
    r3 = lambda a: a.reshape(b, t_len, a.shape[-1])
    per_row = lambda a: a.reshape(KV2, b, t_len).transpose(1, 0, 2)
    w_buf = win_t.shape[2]
    all_w_t = jnp.concatenate([win_t, per_row(kvwt)], axis=2)
    all_w_pad = jnp.pad(all_w_t, ((0, 0), (0, 0), (0, w_pad - (w_buf + t_len))))
    kvs_new_t = jnp.pad(per_row(kvst), ((0, 0), (0, 0), (0, LANES - t_len)))
    o_att = _nsa_sample(layer, r3(q), r3(gates), kvs_new_t, all_w_pad, cmp_t, slc_t, page_table, cw, tables)
    y, hr, hi = _s5(r3(u), h0_re, h0_im, s5m, t_len, bt=b)
    out = _epilogue(x2d, o_att.reshape(m, D_ATT), y.reshape(m, D_SSM), u, ew, alpha)
    return out.reshape(b, t_len, D_MODEL), r3(kvc), r3(kvs), all_w_t[:, :, t_len:], hr, hi


def _feature_major(a):
    nd = a.ndim
    perm = tuple(range(nd - 4)) + (nd - 3, nd - 2, nd - 1, nd - 4)
    t = a.transpose(perm)
    return t.reshape(t.shape[:nd - 4] + (KV2, a.shape[nd - 4]))


def _row_major(a):
    nd = a.ndim
    t = a.reshape(a.shape[:nd - 2] + (2, N_KV, HEAD_DIM, a.shape[-1]))
    perm = tuple(range(nd - 2)) + (nd + 1, nd - 2, nd - 1, nd)
    return t.transpose(perm)


def kernel(x_prompt, x_sample, cache_kv_cmp, cache_kv_slc, state_win_kv, state_ssm_re, state_ssm_im,
           page_table, rel_bias, w_in, cmp_w1, cmp_b1, cmp_w2, cmp_b2, cmp_pos,
           ssm_a_re, ssm_a_im, ssm_log_dt, ssm_b_re, ssm_b_im, ssm_c_re, ssm_c_im, ssm_d,
           ssm_glu_w, ssm_glu_b, w_att_out, w_ssm_out, w_o, ln_g, ln_b):
    depth = w_in.shape[0]
    alpha = (2 * depth) ** 0.25
    bp, seq = x_prompt.shape[:2]
    bs, t_len = x_sample.shape[:2]
    page = cache_kv_cmp.shape[2]
    past_len = page_table.shape[1] * page
    w_buf = state_win_kv.shape[2]
    w_pad = -(-(w_buf + t_len) // LANES) * LANES
    n_win = min(WINDOW, seq)
    p_specs = _prompt_table_specs(seq)
    s_specs = _sample_table_specs(past_len, t_len, w_buf, w_pad, PAGES_PER_STEP * page)
    tabs = _bias_tables(rel_bias, p_specs + s_specs)
    p_tables, s_tables = tabs[:len(p_specs)], tabs[len(p_specs):]
    cmp_t, slc_t, win_t = _feature_major(cache_kv_cmp), _feature_major(cache_kv_slc), _feature_major(state_win_kv)
    h_p, h_s = x_prompt, x_sample
    outs_p, outs_s = [], []
    for l in range(depth):
        lw = _layer_weights(l, w_in, cmp_w1, cmp_b1, cmp_w2, cmp_b2, cmp_pos, ssm_d, ssm_glu_w, ssm_glu_b,
                            w_att_out, w_ssm_out, w_o, ln_g, ln_b)
        ssm = (ssm_a_re[l], ssm_a_im[l], ssm_log_dt[l], ssm_b_re[l], ssm_b_im[l], ssm_c_re[l], ssm_c_im[l])
        h_p, kvct, kvst, kvwt, hr, hi = _prompt_layer(h_p, lw, _s5_mats(*ssm, S5_CHUNK), p_tables, alpha)
        outs_p.append((kvct, kvst, kvwt[:, :, seq - n_win:], hr, hi))
        h_s, kvc, kvs, win, hr, hi = _sample_layer(
            l, h_s, lw, _s5_mats(*ssm, t_len), s_tables, alpha, cmp_t, slc_t, win_t[l],
            state_ssm_re[l], state_ssm_im[l], page_table, w_pad)
        outs_s.append((kvc.reshape(bs, t_len, 2, N_KV, HEAD_DIM), kvs.reshape(bs, t_len, 2, N_KV, HEAD_DIM),
                       win, hr, hi))
    stack = lambda outs, i: jnp.stack([o[i] for o in outs])
    return ((h_p, h_s)
            + tuple(_row_major(stack(outs_p, i)) for i in range(3)) + (stack(outs_p, 3), stack(outs_p, 4))
            + (stack(outs_s, 0), stack(outs_s, 1), _row_major(stack(outs_s, 2)), stack(outs_s, 3), stack(outs_s, 4)))
```

```python
import functools
import math

import numpy as np
import jax
import jax.numpy as jnp
from jax import lax
from jax.experimental import pallas as pl
from jax.experimental.pallas import tpu as pltpu

F32 = jnp.float32
BF16 = jnp.bfloat16

D_MODEL = 1024
N_HEADS = 8
HEAD_DIM = 64
N_KV = 2
HPG = N_HEADS // N_KV
D_ATT = N_HEADS * HEAD_DIM
KV_W = N_KV * HEAD_DIM
CMP_BLOCK = 32
CMP_STRIDE = 16
CMP_HIDDEN = 128
SLC_BLOCK = 64
N_SEL = 16
WINDOW = 512
S5_GROUP = 16
S5_STATE = 64
D_SSM = 512
N_SSM_GROUPS = D_SSM // S5_GROUP
N_BUCKETS = 32
MAX_DISTANCE = 128
LN_EPS = 1e-5
IN_WIDTHS = (D_ATT, KV_W, KV_W, KV_W, KV_W, KV_W, KV_W, 3 * N_HEADS, D_ATT, D_SSM, D_SSM, 2 * D_MODEL)

LANES = 128
SUBLANES = 8
VMEM_LIMIT = 56 * 1024 * 1024
NEG = -1e30
TQ = 128
TK = 128
KV2 = 2 * KV_W
GATE_PAD = LANES
S5_CHUNK = 8
S5_LANE_GROUPS = LANES // S5_GROUP
S5_BATCH_ROWS = 4
SEQ_COLS = D_ATT + 3 * KV2 + GATE_PAD + D_SSM
QCOLS = N_HEADS * TQ
GCOLS = HPG * TQ
PAGES_PER_STEP = 32


def _cparams(sem):
    return pltpu.CompilerParams(dimension_semantics=sem, vmem_limit_bytes=VMEM_LIMIT)


def _dot(a, b):
    return jnp.dot(a, b, preferred_element_type=F32)


def _dot_nt(a, b):
    return lax.dot_general(a, b, (((1,), (1,)), ((), ())), preferred_element_type=F32)


def _split3(x):
    h1 = x.astype(BF16)
    r1 = x - h1.astype(F32)
    h2 = r1.astype(BF16)
    h3 = (r1 - h2.astype(F32)).astype(BF16)
    return h1, h2, h3


def _bucket_np(dist):
    n = np.maximum(dist, 0)
    max_exact = N_BUCKETS // 2
    nf = np.maximum(n, 1).astype(np.float32)
    val = (np.log(nf / np.float32(max_exact)) / np.float32(math.log(MAX_DISTANCE / max_exact))
           * np.float32(N_BUCKETS - max_exact))
    large = max_exact + val.astype(np.int32)
    return np.where(n < max_exact, n, np.minimum(large, N_BUCKETS - 1)).astype(np.int32)


def _pair_perm():
    cols = []
    for r in range(HPG):
        cols += list(range(r * HEAD_DIM, (r + 1) * HEAD_DIM))
        cols += list(range((HPG + r) * HEAD_DIM, (HPG + r + 1) * HEAD_DIM))
    return np.asarray(cols, np.int32)


def _imp_matrix(n_cmp_pad, n_blk_pad, n_cmp):
    ratio = SLC_BLOCK // CMP_STRIDE
    n = np.arange(n_cmp_pad)[:, None]
    j = np.arange(n_blk_pad)[None, :]
    m = ((n >= ratio * j - 1) & (n <= ratio * j + ratio - 2)).astype(np.float32)
    m += ((n >= ratio * j) & (n <= ratio * j + ratio - 1)).astype(np.float32)
    m *= (n < n_cmp)
    return m


def _bucket_starts():
    b = _bucket_np(np.arange(2 * MAX_DISTANCE))
    assert (np.diff(b) >= 0).all() and b.max() == N_BUCKETS - 1
    starts = [int(np.argmax(b >= k)) for k in range(1, N_BUCKETS)]
    assert all(b[s] == k for k, s in zip(range(1, N_BUCKETS), starts))
    return starts


def _bias_of_dist(d, bucket_value):
    val = bucket_value(0)
    for k, start in zip(range(1, N_BUCKETS), _bucket_starts()):
        val = jnp.where(d >= start, bucket_value(k), val)
    return val


def _prompt_tables_kernel(rb_ref, t0_ref, t1_ref, t4_ref, c31_ref, cm_ref, *, row_shift):
    bias = lambda d: _bias_of_dist(d, lambda k: rb_ref[k:k + 1, :])
    last = rb_ref[N_BUCKETS - 1:N_BUCKETS, :]
    shape = (TK, QCOLS)
    d0 = lax.broadcasted_iota(jnp.int32, shape, 1) % TQ - lax.broadcasted_iota(jnp.int32, shape, 0)
    t0_ref[...] = jnp.where(d0 >= 0, bias(d0), NEG)
    t1_ref[...] = bias(TK + d0)
    t4_ref[...] = jnp.where(d0 <= 0, last, NEG)
    c31_ref[...] = last
    shape = cm_ref.shape
    nn = lax.broadcasted_iota(jnp.int32, shape, 0) - row_shift
    dc = lax.broadcasted_iota(jnp.int32, shape, 1) % TQ - (nn * CMP_STRIDE + CMP_BLOCK - 1)
    cm_ref[...] = jnp.where(dc >= 0, bias(dc), NEG)


def _prompt_tables(rel_bias, seq):
    assert WINDOW == 4 * TK and TQ == TK and _bucket_starts()[-1] <= TK
    n_qt = seq // TQ
    n_ch = seq // CMP_STRIDE
    row_shift = (n_qt - 1) * (TQ // CMP_STRIDE)
    tile = jax.ShapeDtypeStruct((TK, QCOLS), F32)
    return pl.pallas_call(
        functools.partial(_prompt_tables_kernel, row_shift=row_shift),
        out_shape=[tile, tile, tile, jax.ShapeDtypeStruct((1, QCOLS), F32),
                   jax.ShapeDtypeStruct((n_ch + row_shift, QCOLS), F32)],
        compiler_params=pltpu.CompilerParams(vmem_limit_bytes=VMEM_LIMIT),
        name="prompt_tables",
    )(jnp.repeat(rel_bias, TQ, axis=1))


def _sample_tables_kernel(rb_ref, bc_ref, bs_ref, c31_ref, bn_ref, bw_ref, *, past_len, t_len, w_buf):
    bias = lambda d: _bias_of_dist(d, lambda k: rb_ref[:, k:k + 1])
    n_tok = past_len + t_len

    def grid(ref):
        qpos = past_len + lax.broadcasted_iota(jnp.int32, ref.shape, 0) % t_len
        return qpos, lax.broadcasted_iota(jnp.int32, ref.shape, 1)

    qpos, nn = grid(bc_ref)
    dc = qpos - (nn * CMP_STRIDE + CMP_BLOCK - 1)
    bc_ref[...] = jnp.where((dc >= 0) & (nn < n_tok // CMP_STRIDE - 1), bias(dc), NEG)
    qpos, ll = grid(bs_ref)
    bs_ref[...] = bias(qpos - (past_len - bs_ref.shape[1] + ll))
    c31_ref[...] = jnp.broadcast_to(rb_ref[:, N_BUCKETS - 1:N_BUCKETS], c31_ref.shape)
    qpos, ll = grid(bn_ref)
    dn = qpos - (past_len + ll)
    bn_ref[...] = jnp.where((dn >= 0) & (ll < t_len), bias(dn), NEG)
    qpos, ii = grid(bw_ref)
    w_pos = n_tok - (w_buf + t_len) + ii
    dw = qpos - w_pos
    bw_ref[...] = jnp.where((dw >= 0) & (dw <= WINDOW) & (w_pos >= 0) & (ii < w_buf + t_len), bias(dw), NEG)


def _sample_tables(rel_bias, past_len, t_len, w_buf, w_pad, last_rows):
    assert _bucket_starts()[-1] <= last_rows
    nq = N_HEADS * t_len
    shp = lambda n: jax.ShapeDtypeStruct((nq, n), F32)
    return pl.pallas_call(
        functools.partial(_sample_tables_kernel, past_len=past_len, t_len=t_len, w_buf=w_buf),
        out_shape=[shp(past_len // CMP_STRIDE), shp(last_rows), shp(LANES), shp(LANES), shp(w_pad)],
        compiler_params=pltpu.CompilerParams(vmem_limit_bytes=VMEM_LIMIT),
        name="sample_tables",
    )(jnp.repeat(rel_bias.T, t_len, axis=0))


def _inproj_kernel(x_ref, w_ref, q_ref, kvc_ref, kvs_ref, kvw_ref, kvct_ref, kvst_ref, kvwt_ref, g_ref, u_ref):
    h = _dot(x_ref[...].astype(BF16), w_ref[...])
    o = 0
    q_ref[...] = h[:, o:o + D_ATT] * (HEAD_DIM ** -0.5)
    o += D_ATT
    for row_ref, t_ref in ((kvc_ref, kvct_ref), (kvs_ref, kvst_ref), (kvw_ref, kvwt_ref)):
        kv = h[:, o:o + KV2]
        row_ref[...] = kv
        t_ref[0] = kv.T
        o += KV2
    g_ref[...] = jax.nn.sigmoid(h[:, o:o + GATE_PAD])
    o += GATE_PAD
    u_ref[...] = h[:, o:o + D_SSM]


def _inproj(x2d, w_seq, rows_per_seq):
    m = x2d.shape[0]
    tm = min(512, m)
    tps = rows_per_seq // tm
    tok = lambda w: pl.BlockSpec((tm, w), lambda i: (i, 0))
    ft = pl.BlockSpec((1, KV2, tm), lambda i: (i // tps, 0, i % tps))
    tok_shape = lambda w: jax.ShapeDtypeStruct((m, w), F32)
    ft_shape = jax.ShapeDtypeStruct((m // rows_per_seq, KV2, rows_per_seq), F32)
    return pl.pallas_call(
        _inproj_kernel,
        grid=(m // tm,),
        in_specs=[tok(D_MODEL), pl.BlockSpec((D_MODEL, SEQ_COLS), lambda i: (0, 0))],
        out_specs=[tok(D_ATT), tok(KV2), tok(KV2), tok(KV2), ft, ft, ft, tok(GATE_PAD), tok(D_SSM)],
        out_shape=[tok_shape(D_ATT), tok_shape(KV2), tok_shape(KV2), tok_shape(KV2), ft_shape, ft_shape, ft_shape,
                   tok_shape(GATE_PAD), tok_shape(D_SSM)],
        compiler_params=_cparams(("parallel",)),
        name="inproj",
    )(x2d, w_seq)


def _compress_out(hf, hs_next, c1, w2_ref, b2_ref):
    h = hf + hs_next + c1
    return _dot(jax.nn.silu(h).astype(BF16), w2_ref[...]) + b2_ref[...]


def _compress_const(pos_ref, w1_ref, b1_ref):
    hh = _dot(pos_ref[...].astype(BF16), w1_ref[...])
    nh = 4 * CMP_HIDDEN
    return hh[0:1, :nh] + hh[1:2, nh:] + b1_ref[...]


def _softmax_cols(st):
    m = jnp.max(st, axis=0, keepdims=True)
    valid = m > 0.5 * NEG
    e = jnp.exp(st - m)
    s = jnp.sum(e, axis=0, keepdims=True)
    return jnp.where(valid, e / s, 0.0)


def _pv_groups(vt_ref, k0, tk, e_bf):
    v0 = vt_ref[0:HEAD_DIM, pl.ds(k0, tk)]
    v1 = vt_ref[HEAD_DIM:KV_W, pl.ds(k0, tk)]
    return jnp.concatenate([_dot(v0, e_bf[:, :GCOLS]), _dot(v1, e_bf[:, GCOLS:])], axis=1)


def _flash_init(m_ref, l_ref, acc_ref):
    m_ref[...] = jnp.full(m_ref.shape, NEG, F32)
    l_ref[...] = jnp.zeros(l_ref.shape, F32)
    acc_ref[...] = jnp.zeros(acc_ref.shape, F32)


def _flash_update(st, vt_ref, k0, tk, m_ref, l_ref, acc_ref):
    m_old = m_ref[...]
    m_new = jnp.maximum(m_old, jnp.max(st, axis=0, keepdims=True))
    alpha = jnp.exp(m_old - m_new)
    e = jnp.exp(st - m_new)
    l_ref[...] = alpha * l_ref[...] + jnp.sum(e, axis=0, keepdims=True)
    acc_ref[...] = alpha * acc_ref[...] + _pv_groups(vt_ref, k0, tk, e.astype(BF16))
    m_ref[...] = m_new


def _flash_coef(m_ref, l_ref):
    valid = m_ref[...] > 0.5 * NEG
    return jnp.where(valid, 1.0 / jnp.where(valid, l_ref[...], 1.0), 0.0)


def _nsa_prompt_kernel(q_ref, kck_ref, kcv_ref, ks_ref, kw_ref, vst_ref, vwt_ref, g_ref,
                       w1_ref, pos_ref, b1_ref, w2_ref, b2_ref,
                       cmaster_ref, t0_ref, t1_ref, t4_ref, c31_ref, mimp_ref, eblk_ref,
                       o_ref,
                       kc_scr, vct_scr, ks_scr, vs_scr, kw_scr, vw_scr, neg_scr, m_scr, l_scr, acc_scr, tot_scr):
    qt = pl.program_id(1)
    n_qt = pl.num_programs(1)
    seq = ks_ref.shape[1]
    n_ch = seq // CMP_STRIDE
    n_blk = seq // SLC_BLOCK
    nh = 4 * CMP_HIDDEN

    @pl.when(qt == 0)
    def _per_batch():
        x = jnp.concatenate([half[0, pl.ds(p, n_ch, stride=CMP_STRIDE), :]
                             for p in range(CMP_STRIDE) for half in (kck_ref, kcv_ref)],
                            axis=1).astype(BF16)
        hh = _dot(x, w1_ref[...])
        c1 = _compress_const(pos_ref, w1_ref, b1_ref)
        hs_next = pltpu.roll(hh[:, nh:], n_ch - 1, 0)
        kv = _compress_out(hh[:, :nh], hs_next, c1, w2_ref, b2_ref)
        kc_scr[...] = kv[:, :KV_W].astype(BF16)
        vct_scr[...] = kv[:, KV_W:].T.astype(BF16)
        ks_scr[...] = ks_ref[0].astype(BF16)
        kw_scr[...] = kw_ref[0].astype(BF16)
        vs_scr[...] = vst_ref[0].astype(BF16)
        vw_scr[...] = vwt_ref[0].astype(BF16)

    q = q_ref[0]
    gt = g_ref[0].T
    lane = lax.broadcasted_iota(jnp.int32, (TQ, LANES), 1)
    parts = []
    for g in range(N_KV):
        gmask = (lane < HEAD_DIM) if g == 0 else (lane >= HEAD_DIM)
        parts += [jnp.where(gmask, q[:, r * LANES:(r + 1) * LANES], 0.0) for r in range(HPG)]
    qb = jnp.concatenate(parts, axis=0).astype(BF16)
    q0 = qt * TQ

    def gate_cols(j):
        return jnp.concatenate([gt[3 * h + j:3 * h + j + 1, :] for h in range(N_HEADS)], axis=1)

    r0 = pl.multiple_of((n_qt - 1 - qt) * (TQ // CMP_STRIDE), SUBLANES)
    sc = _dot_nt(kc_scr[...], qb) + cmaster_ref[pl.ds(r0, n_ch), :]
    pt = _softmax_cols(sc)
    pb = pt.astype(BF16)
    oc = jnp.concatenate([_dot(vct_scr[0:HEAD_DIM, :], pb[:, :GCOLS]),
                          _dot(vct_scr[HEAD_DIM:KV_W, :], pb[:, GCOLS:])], axis=1)
    tot_scr[...] = oc * gate_cols(0)

    psum = []
    for g in range(N_KV):
        acc = pt[:, g * GCOLS:g * GCOLS + TQ]
        for r in range(1, HPG):
            acc = acc + pt[:, g * GCOLS + r * TQ:g * GCOLS + (r + 1) * TQ]
        psum.append(acc)
    psum = jnp.concatenate(psum, axis=1)
    p1, p2, p3 = _split3(psum)
    imp = _dot(mimp_ref[...], p1) + _dot(mimp_ref[...], p2) + _dot(mimp_ref[...], p3)
    shape = (n_blk, N_KV * TQ)
    bj = lax.broadcasted_iota(jnp.int32, shape, 0)
    tpos = q0 + lax.broadcasted_iota(jnp.int32, shape, 1) % TQ
    cur = tpos // SLC_BLOCK
    forced = (bj == 0) | (bj == cur) | (bj == cur - 1)
    visible = bj * SLC_BLOCK <= tpos
    score = jnp.where(forced, -NEG, jnp.where(visible, imp, NEG))
    rank = jnp.zeros(shape, F32)
    for i in range(n_blk):
        si = score[i:i + 1, :]
        ahead = (si > score) | ((si == score) & (bj > i))
        rank = rank + jnp.where(ahead, 1.0, 0.0)
    sel = jnp.where(rank < float(N_SEL), 1.0, 0.0).astype(BF16)
    neg_scr[...] = (_dot(eblk_ref[...], sel) - 1.0) * (-NEG)

    _flash_init(m_scr, l_scr, acc_scr)

    def sel_block(k0, tk, bias):
        st = _dot_nt(ks_scr[pl.ds(k0, tk), :], qb) + bias
        ng = neg_scr[pl.ds(k0, tk), :]
        st = st + jnp.concatenate([ng[:, :TQ]] * HPG + [ng[:, TQ:]] * HPG, axis=1)
        _flash_update(st, vs_scr, k0, tk, m_scr, l_scr, acc_scr)

    n_far = jnp.maximum(qt - 1, 0)

    def far_body(j, carry):
        sel_block(pl.multiple_of(j * 2 * TK, 2 * TK), 2 * TK, c31_ref[...])
        return carry

    lax.fori_loop(0, n_far // 2, far_body, 0)

    @pl.when(n_far % 2 == 1)
    def _far_tail():
        sel_block(pl.multiple_of((n_far - 1) * TK, TK), TK, c31_ref[...])

    @pl.when(qt >= 1)
    def _near():
        sel_block(pl.multiple_of((qt - 1) * TK, TK), TK, t1_ref[...])

    sel_block(pl.multiple_of(qt * TK, TK), TK, t0_ref[...])
    tot_scr[...] = tot_scr[...] + acc_scr[...] * (_flash_coef(m_scr, l_scr) * gate_cols(1))

    _flash_init(m_scr, l_scr, acc_scr)

    def win_block(k0, tk, bias):
        st = _dot_nt(kw_scr[pl.ds(k0, tk), :], qb) + bias
        _flash_update(st, vw_scr, k0, tk, m_scr, l_scr, acc_scr)

    @pl.when(qt >= 4)
    def _w4():
        win_block(pl.multiple_of((qt - 4) * TK, TK), TK, t4_ref[...])

    @pl.when(qt >= 3)
    def _w32():
        win_block(pl.multiple_of((qt - 3) * TK, TK), 2 * TK, c31_ref[...])

    @pl.when(qt == 2)
    def _w2():
        win_block(0, TK, c31_ref[...])

    @pl.when(qt >= 1)
    def _w1():
        win_block(pl.multiple_of((qt - 1) * TK, TK), TK, t1_ref[...])

    win_block(pl.multiple_of(qt * TK, TK), TK, t0_ref[...])
    tot = tot_scr[...] + acc_scr[...] * (_flash_coef(m_scr, l_scr) * gate_cols(2))
    for r in range(HPG):
        blk = jnp.concatenate([tot[:, r * TQ:(r + 1) * TQ], tot[:, GCOLS + r * TQ:GCOLS + (r + 1) * TQ]], axis=0)
        o_ref[0, :, r * LANES:(r + 1) * LANES] = blk.T


def _nsa_prompt(q, kvc, kvs, kvw, kvst, kvwt, gates, cw, tables):
    b, seq, _ = q.shape
    n_qt = seq // TQ
    n_ch = seq // CMP_STRIDE
    n_blk = seq // SLC_BLOCK
    t0, t1, t4, c31, cmaster = tables
    mimp = jnp.asarray(_imp_matrix(n_ch, n_blk, n_ch - 1).T, BF16)
    eblk = jnp.asarray((np.arange(seq)[:, None] // SLC_BLOCK == np.arange(n_blk)[None, :]), BF16)

    def full(a):
        nd = a.ndim
        return pl.BlockSpec(a.shape, lambda i, j, nd=nd: (0,) * nd)

    k_spec = pl.BlockSpec((1, seq, KV_W), lambda i, j: (i, 0, 0))
    vt_spec = pl.BlockSpec((1, KV_W, seq), lambda i, j: (i, 1, 0))
    return pl.pallas_call(
        _nsa_prompt_kernel,
        grid=(b, n_qt),
        in_specs=[pl.BlockSpec((1, TQ, D_ATT), lambda i, j: (i, j, 0)),
                  k_spec, pl.BlockSpec((1, seq, KV_W), lambda i, j: (i, 0, 1)),
                  k_spec, k_spec, vt_spec, vt_spec,
                  pl.BlockSpec((1, TQ, GATE_PAD), lambda i, j: (i, j, 0)),
                  full(cw["w1"]), full(cw["pos"]), full(cw["b1"]), full(cw["w2"]), full(cw["b2"]),
                  full(cmaster), full(t0), full(t1), full(t4), full(c31), full(mimp), full(eblk)],
        out_specs=pl.BlockSpec((1, TQ, D_ATT), lambda i, j: (i, j, 0)),
        out_shape=jax.ShapeDtypeStruct((b, seq, D_ATT), F32),
        scratch_shapes=[pltpu.VMEM((n_ch, KV_W), BF16), pltpu.VMEM((KV_W, n_ch), BF16),
                        pltpu.VMEM((seq, KV_W), BF16), pltpu.VMEM((KV_W, seq), BF16),
                        pltpu.VMEM((seq, KV_W), BF16), pltpu.VMEM((KV_W, seq), BF16),
                        pltpu.VMEM((seq, N_KV * TQ), F32),
                        pltpu.VMEM((1, QCOLS), F32), pltpu.VMEM((1, QCOLS), F32),
                        pltpu.VMEM((HEAD_DIM, QCOLS), F32), pltpu.VMEM((HEAD_DIM, QCOLS), F32)],
        compiler_params=_cparams(("arbitrary", "arbitrary")),
        name="nsa_prompt",
    )(q, kvc, kvc, kvs, kvw, kvst, kvwt, gates, cw["w1"], cw["pos"], cw["b1"], cw["w2"], cw["b2"],
      cmaster, t0, t1, t4, c31, mimp, eblk)


def _s5_kernel(u_ref, h0_ref, v_ref, m_ref, wh_ref, a1_ref, a2_ref, y_ref, ht_ref, z_scr, *, n_chunks, bt):
    gc = u_ref.shape[1]
    sw = 2 * S5_STATE
    for g in range(gc):
        z_scr[:, g * sw:(g + 1) * sw] = _dot(u_ref[0, g], v_ref[g])
    a1 = a1_ref[...]
    a2 = a2_ref[...]

    def step(k, h):
        r0 = pl.multiple_of(k * bt, bt)
        z = z_scr[pl.ds(r0, bt), :]
        z_scr[pl.ds(r0, bt), :] = h
        hsw = jnp.concatenate(
            [pltpu.roll(h[:, g * sw:(g + 1) * sw], S5_STATE, 1) for g in range(gc)], axis=1)
        return a1 * h + a2 * hsw + z

    ht_ref[0] = lax.fori_loop(0, n_chunks, step, h0_ref[0])
    for g in range(gc):
        y_ref[0, g] = (_dot(u_ref[0, g], m_ref[g])
                       + _dot(z_scr[:, g * sw:(g + 1) * sw].astype(BF16), wh_ref[g]))


def _s5_mats(a_re, a_im, log_dt, b_re, b_im, c_re, c_im, chunk):
    hp = lax.Precision.HIGHEST
    dt = jnp.exp(log_dt)[:, None]
    mag = jnp.exp(dt * a_re)
    ab_re, ab_im = mag * jnp.cos(dt * a_im), mag * jnp.sin(dt * a_im)
    den = a_re * a_re + a_im * a_im
    x_re, x_im = ab_re - 1.0, ab_im
    k_re = (x_re * a_re + x_im * a_im) / den
    k_im = (x_im * a_re - x_re * a_im) / den
    bb_re = k_re[..., None] * b_re - k_im[..., None] * b_im
    bb_im = k_re[..., None] * b_im + k_im[..., None] * b_re
    pw_re, pw_im = [jnp.ones_like(ab_re)], [jnp.zeros_like(ab_im)]
    for _ in range(chunk):
        pr, pi = pw_re[-1], pw_im[-1]
        pw_re.append(pr * ab_re - pi * ab_im)
        pw_im.append(pr * ab_im + pi * ab_re)
    pw_re, pw_im = jnp.stack(pw_re), jnp.stack(pw_im)
    cl_re = c_re[None] * pw_re[:, :, None, :] - c_im[None] * pw_im[:, :, None, :]
    cl_im = c_re[None] * pw_im[:, :, None, :] + c_im[None] * pw_re[:, :, None, :]
    t = (jnp.einsum('tgcp,gpd->tgcd', cl_re, bb_re, precision=hp)
         - jnp.einsum('tgcp,gpd->tgcd', cl_im, bb_im, precision=hp))
    s_i = np.arange(chunk)[:, None]
    t_i = np.arange(chunk)[None, :]
    lag = np.clip(t_i - s_i, 0, None)
    blk = jnp.where((t_i >= s_i)[:, :, None, None, None], t[lag], 0.0)
    ng = a_re.shape[0]
    m = blk.transpose(2, 0, 4, 1, 3).reshape(ng, chunk * S5_GROUP, chunk * S5_GROUP)
    rev = chunk - 1 - np.arange(chunk)
    vb_re = pw_re[rev][:, :, :, None] * bb_re[None] - pw_im[rev][:, :, :, None] * bb_im[None]
    vb_im = pw_re[rev][:, :, :, None] * bb_im[None] + pw_im[rev][:, :, :, None] * bb_re[None]
    v = jnp.concatenate([vb_re, vb_im], axis=2).transpose(1, 0, 3, 2).reshape(ng, chunk * S5_GROUP, 2 * S5_STATE)
    wh = jnp.concatenate([cl_re[1:], -cl_im[1:]], axis=3)
    wh = wh.transpose(1, 3, 0, 2).reshape(ng, 2 * S5_STATE, chunk * S5_GROUP)
    a1 = jnp.concatenate([pw_re[chunk], pw_re[chunk]], axis=1).reshape(1, ng * 2 * S5_STATE)
    a2 = jnp.concatenate([-pw_im[chunk], pw_im[chunk]], axis=1).reshape(1, ng * 2 * S5_STATE)
    return v.astype(BF16), m.astype(BF16), wh.astype(BF16), a1, a2


def _s5(u, h0_re, h0_im, mats, chunk, bt, gc=8):
    b, t_len, _ = u.shape
    ng = N_SSM_GROUPS
    n_chunks = t_len // chunk
    nbc = b // bt
    lc = chunk * S5_GROUP
    rows = n_chunks * bt
    sw = 2 * S5_STATE
    v, m, wh, a1, a2 = mats
    ur = u.reshape(nbc, bt, n_chunks, chunk, ng, S5_GROUP).transpose(0, 4, 2, 1, 3, 5)
    ur = ur.reshape(nbc, ng, rows, lc).astype(BF16)
    h0 = jnp.concatenate([h0_re, h0_im], axis=-1).reshape(nbc, bt, ng * sw)
    kern = functools.partial(_s5_kernel, n_chunks=n_chunks, bt=bt)
    y, ht = pl.pallas_call(
        kern,
        grid=(nbc, ng // gc),
        in_specs=[pl.BlockSpec((1, gc, rows, lc), lambda i, j: (i, j, 0, 0)),
                  pl.BlockSpec((1, bt, gc * sw), lambda i, j: (i, 0, j)),
                  pl.BlockSpec((gc, lc, sw), lambda i, j: (j, 0, 0)),
                  pl.BlockSpec((gc, lc, lc), lambda i, j: (j, 0, 0)),
                  pl.BlockSpec((gc, sw, lc), lambda i, j: (j, 0, 0)),
                  pl.BlockSpec((1, gc * sw), lambda i, j: (0, j)),
                  pl.BlockSpec((1, gc * sw), lambda i, j: (0, j))],
        out_specs=[pl.BlockSpec((1, gc, rows, lc), lambda i, j: (i, j, 0, 0)),
                   pl.BlockSpec((1, bt, gc * sw), lambda i, j: (i, 0, j))],
        out_shape=[jax.ShapeDtypeStruct((nbc, ng, rows, lc), F32),
                   jax.ShapeDtypeStruct((nbc, bt, ng * sw), F32)],
        scratch_shapes=[pltpu.VMEM((rows, gc * sw), F32)],
        compiler_params=_cparams(("parallel", "parallel")),
        name="s5_scan",
    )(ur, h0, v, m, wh, a1, a2)
    y = y.reshape(nbc, ng, n_chunks, bt, chunk, S5_GROUP).transpose(0, 3, 2, 4, 1, 5).reshape(b, t_len, D_SSM)
    ht = ht.reshape(b, ng, sw)
    return y, ht[..., :S5_STATE], ht[..., S5_STATE:]


def _s5_nat_kernel(u_ref, h0_ref, v_ref, m_ref, wh_ref, ar_ref, ai_ref, y_ref, ht_ref, uf_scr, z_scr,
                   *, chunk, n_chunks, bt):
    rows = bt * n_chunks
    half = S5_LANE_GROUPS * S5_STATE
    n_tiles = 2 * half // LANES
    for t in range(chunk):
        piece = u_ref[:, pl.ds(t, n_chunks, stride=chunk), :]
        uf_scr[:, t * LANES:(t + 1) * LANES] = piece.reshape(rows, LANES).astype(BF16)
    uf = uf_scr[...]
    z = _dot(uf, v_ref[0])
    for j in range(n_tiles):
        z_scr[j] = z[:, j * LANES:(j + 1) * LANES]
    ar = ar_ref[...]
    ai = ai_ref[...]

    def step(k, h):
        idx = pl.ds(k, bt, stride=n_chunks)
        zk = jnp.concatenate([z_scr[j, idx, :] for j in range(n_tiles)], axis=1)
        for j in range(n_tiles):
            z_scr[j, idx, :] = h[:, j * LANES:(j + 1) * LANES]
        hre, him = h[:, :half], h[:, half:]
        return jnp.concatenate([ar * hre - ai * him, ar * him + ai * hre], axis=1) + zk

    ht_ref[0] = lax.fori_loop(0, n_chunks, step, h0_ref[0])
    hs = jnp.concatenate([z_scr[j] for j in range(n_tiles)], axis=1).astype(BF16)
    y = _dot(uf, m_ref[0]) + _dot(hs, wh_ref[0])
    for t in range(chunk):
        y_ref[:, pl.ds(t, n_chunks, stride=chunk), :] = y[:, t * LANES:(t + 1) * LANES].reshape(bt, n_chunks, LANES)


def _s5_nat_mats(mats, chunk):
    v, m, wh, a1, a2 = mats
    ng = v.shape[0]
    nx = ng // S5_LANE_GROUPS
    eye = jnp.eye(S5_LANE_GROUPS, dtype=v.dtype)
    lk = chunk * LANES
    gw = S5_LANE_GROUPS * 2 * S5_STATE
    v6 = v.reshape(nx, S5_LANE_GROUPS, chunk, S5_GROUP, 2, S5_STATE)
    v8 = jnp.einsum('xgsiqp,gh->xsgiqhp', v6, eye).reshape(nx, lk, gw)
    m6 = m.reshape(nx, S5_LANE_GROUPS, chunk, S5_GROUP, chunk, S5_GROUP)
    m8 = jnp.einsum('xgsito,gh->xsgitho', m6, eye).reshape(nx, lk, lk)
    w6 = wh.reshape(nx, S5_LANE_GROUPS, 2, S5_STATE, chunk, S5_GROUP)
    w8 = jnp.einsum('xgqpto,gh->xqgptho', w6, eye).reshape(nx, gw, lk)
    ar = a1.reshape(nx, S5_LANE_GROUPS, 2, S5_STATE)[:, :, 0].reshape(1, ng * S5_STATE)
    ai = a2.reshape(nx, S5_LANE_GROUPS, 2, S5_STATE)[:, :, 1].reshape(1, ng * S5_STATE)
    return v8, m8, w8, ar, ai


def _s5_nat(u, h0_re, h0_im, mats, chunk, bt):
    b, t_len, _ = u.shape
    n_chunks = t_len // chunk
    nbc = b // bt
    nx = D_SSM // LANES
    half = S5_LANE_GROUPS * S5_STATE
    gw = 2 * half
    lk = chunk * LANES
    rows = bt * n_chunks
    v8, m8, w8, ar, ai = _s5_nat_mats(mats, chunk)
    planar = lambda a: a.reshape(nbc, bt, nx, 1, half)
    h0 = jnp.concatenate([planar(h0_re), planar(h0_im)], axis=3).reshape(nbc, bt, nx * gw)
    y, ht = pl.pallas_call(
        functools.partial(_s5_nat_kernel, chunk=chunk, n_chunks=n_chunks, bt=bt),
        grid=(nbc, nx),
        in_specs=[pl.BlockSpec((bt, t_len, LANES), lambda i, j: (i, 0, j)),
                  pl.BlockSpec((1, bt, gw), lambda i, j: (i, 0, j)),
                  pl.BlockSpec((1, lk, gw), lambda i, j: (j, 0, 0)),
                  pl.BlockSpec((1, lk, lk), lambda i, j: (j, 0, 0)),
                  pl.BlockSpec((1, gw, lk), lambda i, j: (j, 0, 0)),
                  pl.BlockSpec((1, half), lambda i, j: (0, j)),
                  pl.BlockSpec((1, half), lambda i, j: (0, j))],
        out_specs=[pl.BlockSpec((bt, t_len, LANES), lambda i, j: (i, 0, j)),
                   pl.BlockSpec((1, bt, gw), lambda i, j: (i, 0, j))],
        out_shape=[jax.ShapeDtypeStruct((b, t_len, D_SSM), F32),
                   jax.ShapeDtypeStruct((nbc, bt, nx * gw), F32)],
        scratch_shapes=[pltpu.VMEM((rows, lk), BF16), pltpu.VMEM((gw // LANES, rows, LANES), F32)],
        compiler_params=_cparams(("parallel", "parallel")),
        name="s5_scan_nat",
    )(u, h0, v8, m8, w8, ar, ai)
    ht = ht.reshape(b, nx, 2, S5_LANE_GROUPS, S5_STATE)
    return (y, ht[:, :, 0].reshape(b, N_SSM_GROUPS, S5_STATE), ht[:, :, 1].reshape(b, N_SSM_GROUPS, S5_STATE))


def _epilogue_kernel(x_ref, o_ref, y_ref, u_ref, wz_ref, d_ref, glu0_ref, glu1_ref, gb0_ref, gb1_ref,
                     wa_ref, ws_ref, wo_ref, lng_ref, lnb_ref, out_ref, *, alpha):
    x = x_ref[...]
    z = _dot(x.astype(BF16), wz_ref[...])
    za = z[:, :D_ATT]
    zs = z[:, D_ATT:D_ATT + D_SSM]
    ga = z[:, D_ATT + D_SSM:D_ATT + D_SSM + D_MODEL]
    gs = z[:, D_ATT + D_SSM + D_MODEL:]
    p_att = _dot((o_ref[...] * jax.nn.silu(za)).astype(BF16), wa_ref[...])
    y = jax.nn.gelu(y_ref[...] + d_ref[...] * u_ref[...]).astype(BF16)
    s = (_dot(y, glu0_ref[...]) + gb0_ref[...]) * jax.nn.sigmoid(_dot(y, glu1_ref[...]) + gb1_ref[...])
    p_ssm = _dot((s * jax.nn.silu(zs)).astype(BF16), ws_ref[...])
    merged = jax.nn.sigmoid(ga) * p_att + jax.nn.sigmoid(gs) * p_ssm
    r = alpha * x + _dot(merged.astype(BF16), wo_ref[...])
    mu = jnp.mean(r, axis=-1, keepdims=True)
    c = r - mu
    var = jnp.mean(c * c, axis=-1, keepdims=True)
    out_ref[...] = c * lax.rsqrt(var + LN_EPS) * lng_ref[...] + lnb_ref[...]


def _epilogue(x2d, o2d, y2d, u2d, ew, alpha):
    m = x2d.shape[0]
    tm = min(256, m)
    names = ("wz", "d", "glu0", "glu1", "gb0", "gb1", "wa", "ws", "wo", "lng", "lnb")
    ws = [ew[n] for n in names]

    def tok(w):
        return pl.BlockSpec((tm, w), lambda i: (i, 0))

    return pl.pallas_call(
        functools.partial(_epilogue_kernel, alpha=alpha),
        grid=(m // tm,),
        in_specs=[tok(D_MODEL), tok(D_ATT), tok(D_SSM), tok(D_SSM)]
        + [pl.BlockSpec(w.shape, lambda i: (0, 0)) for w in ws],
        out_specs=tok(D_MODEL),
        out_shape=jax.ShapeDtypeStruct((m, D_MODEL), F32),
        compiler_params=_cparams(("parallel",)),
        name="epilogue",
    )(x2d, o2d, y2d, u2d, *ws)


def _layer_weights(l, w_in, cmp_w1, cmp_b1, cmp_w2, cmp_b2, cmp_pos, ssm_d, ssm_glu_w, ssm_glu_b,
                   w_att_out, w_ssm_out, w_o, ln_g, ln_b):
    offs = np.concatenate([[0], np.cumsum(IN_WIDTHS)])
    col = lambda i: w_in[l][:, offs[i]:offs[i + 1]]
    perm = _pair_perm()
    gate = jnp.pad(col(7), ((0, 0), (0, GATE_PAD - 3 * N_HEADS)))
    w_seq = jnp.concatenate([col(0)[:, perm], col(1), col(2), col(3), col(4), col(5), col(6), gate, col(9)],
                            axis=1).astype(BF16)
    wz = jnp.concatenate([col(8)[:, perm], col(10), col(11)], axis=1).astype(BF16)

    half = CMP_BLOCK // 2
    eye = jnp.eye(4, dtype=F32)
    w1s = jnp.stack([cmp_w1[l, 0], cmp_w1[l, 0], cmp_w1[l, 1], cmp_w1[l, 1]])
    bd = jnp.einsum('spdh,st->psdth', w1s, eye)
    bd = bd.reshape(CMP_BLOCK, 4 * HEAD_DIM, 4 * CMP_HIDDEN)
    w1 = jnp.concatenate([bd[:half].reshape(half * 4 * HEAD_DIM, -1),
                          bd[half:].reshape(half * 4 * HEAD_DIM, -1)], axis=1).astype(BF16)
    pos_s = jnp.stack([cmp_pos[l, 0], cmp_pos[l, 0], cmp_pos[l, 1], cmp_pos[l, 1]], axis=1)
    pos = jnp.zeros((SUBLANES, half * 4 * HEAD_DIM), F32)
    pos = pos.at[0].set(pos_s[:half].reshape(-1)).at[1].set(pos_s[half:].reshape(-1))
    b1 = jnp.concatenate([cmp_b1[l, 0], cmp_b1[l, 0], cmp_b1[l, 1], cmp_b1[l, 1]]).reshape(1, -1)
    w2s = jnp.stack([cmp_w2[l, 0], cmp_w2[l, 0], cmp_w2[l, 1], cmp_w2[l, 1]])
    w2 = jnp.einsum('shd,st->shtd', w2s, eye).reshape(4 * CMP_HIDDEN, 4 * HEAD_DIM).astype(BF16)
    b2 = jnp.concatenate([cmp_b2[l, 0], cmp_b2[l, 0], cmp_b2[l, 1], cmp_b2[l, 1]]).reshape(1, -1)
    cw = dict(w1=w1, pos=pos, b1=b1, w2=w2, b2=b2)

    ew = dict(wz=wz, d=ssm_d[l].reshape(1, D_SSM),
              glu0=ssm_glu_w[l, 0].astype(BF16), glu1=ssm_glu_w[l, 1].astype(BF16),
              gb0=ssm_glu_b[l, 0].reshape(1, -1), gb1=ssm_glu_b[l, 1].reshape(1, -1),
              wa=w_att_out[l][perm].astype(BF16), ws=w_ssm_out[l].astype(BF16), wo=w_o[l].astype(BF16),
              lng=ln_g[l].reshape(1, -1), lnb=ln_b[l].reshape(1, -1))
    return w_seq, cw, ew


def _prompt_layer(h_p, lw, s5m, tables, alpha):
    b, seq, _ = h_p.shape
    w_seq, cw, ew = lw
    x2d = h_p.reshape(b * seq, D_MODEL)
    q, kvc, kvs, kvw, kvct, kvst, kvwt, gates, u = _inproj(x2d, w_seq, seq)
    r3 = lambda a: a.reshape(b, seq, a.shape[-1])
    o_att = _nsa_prompt(r3(q), r3(kvc), r3(kvs), r3(kvw), kvst, kvwt, r3(gates), cw, tables)
    zero = jnp.zeros((b, N_SSM_GROUPS, S5_STATE), F32)
    y, hr, hi = _s5_nat(r3(u), zero, zero, s5m, S5_CHUNK, bt=min(S5_BATCH_ROWS, b))
    out = _epilogue(x2d, o_att.reshape(b * seq, D_ATT), y.reshape(b * seq, D_SSM), u, ew, alpha)
    return out.reshape(b, seq, D_MODEL), kvct, kvst, kvwt, hr, hi


def _gather_step(page_dma):
    b, c = pl.program_id(0), pl.program_id(1)
    n_c = pl.num_programs(1)
    step = b * n_c + c
    total = pl.num_programs(0) * n_c
    slot = step % 2

    @pl.when(step == 0)
    def _first():
        for j in range(PAGES_PER_STEP):
            page_dma(b, c, j, slot).start()

    @pl.when(step + 1 < total)
    def _prefetch():
        nxt = step + 1
        for j in range(PAGES_PER_STEP):
            page_dma(nxt // n_c, nxt % n_c, j, 1 - slot).start()

    for j in range(PAGES_PER_STEP):
        page_dma(b, c, j, slot).wait()
    return slot


def _sample_queries(q):
    t = q.shape[0]
    lane = lax.broadcasted_iota(jnp.int32, (t, LANES), 1)
    parts = []
    for g in range(N_KV):
        gmask = (lane < HEAD_DIM) if g == 0 else (lane >= HEAD_DIM)
        parts += [jnp.where(gmask, q[:, r * LANES:(r + 1) * LANES], 0.0) for r in range(HPG)]
    return jnp.concatenate(parts, axis=0).astype(BF16)


def _softmax_rows(s):
    m = jnp.max(s, axis=1, keepdims=True)
    valid = m > 0.5 * NEG
    e = jnp.exp(s - m)
    return jnp.where(valid, e / jnp.sum(e, axis=1, keepdims=True), 0.0)


def _cmp_sample_kernel(pt_ref, cache_hbm, q_ref, perm_ref, w1_ref, pos_ref, b1_ref, w2_ref, b2_ref, biasc_ref,
                       mimp_ref, oc_ref, sel_ref, buf, sem, x_scr, hf_scr, hs_scr,
                       *, layer, t_len, past_len, n_blk):
    c = pl.program_id(1)
    n_c = pl.num_programs(1)
    nh = 4 * CMP_HIDDEN
    cpp = buf.shape[3] // CMP_STRIDE
    rows = PAGES_PER_STEP * cpp

    def page_dma(b, cc, j, slot):
        page = pt_ref[b, cc * PAGES_PER_STEP + j]
        return pltpu.make_async_copy(cache_hbm.at[layer, page], buf.at[slot, j], sem.at[slot])

    slot = _gather_step(page_dma)

    def regroup(j, carry):
        xp = _dot_nt(perm_ref[...], buf[slot, j].astype(BF16))
        r0 = pl.multiple_of(j * cpp, cpp)
        for p in range(CMP_STRIDE):
            x_scr[pl.ds(r0, cpp), p * KV2:(p + 1) * KV2] = xp[p * cpp:(p + 1) * cpp, :]
        return carry

    lax.fori_loop(0, PAGES_PER_STEP, regroup, 0, unroll=4)
    hh = _dot(x_scr[...].astype(BF16), w1_ref[...])
    r0 = pl.multiple_of(c * rows, rows)
    hf_scr[pl.ds(r0, rows), :] = hh[:, :nh]
    hs_scr[pl.ds(r0, rows), :] = hh[:, nh:]

    @pl.when(c == n_c - 1)
    def _finish():
        n_ch = hf_scr.shape[0]
        c1 = _compress_const(pos_ref, w1_ref, b1_ref)
        hs_next = pltpu.roll(hs_scr[...], n_ch - 1, 0)
        kv = _compress_out(hf_scr[...], hs_next, c1, w2_ref, b2_ref)
        q64 = _sample_queries(q_ref[0])
        p = _softmax_rows(_dot_nt(q64, kv[:, :KV_W].astype(BF16)) + biasc_ref[...])
        oc_ref[0] = _dot(p.astype(BF16), kv[:, KV_W:].astype(BF16))
        psum = []
        for g in range(N_KV):
            acc = p[g * HPG * t_len:g * HPG * t_len + t_len]
            for r in range(1, HPG):
                acc = acc + p[(g * HPG + r) * t_len:(g * HPG + r + 1) * t_len]
            psum.append(acc)
        psum = jnp.concatenate(psum, axis=0)
        p1, p2, p3 = _split3(psum)
        imp = _dot(p1, mimp_ref[...]) + _dot(p2, mimp_ref[...]) + _dot(p3, mimp_ref[...])
        shape = imp.shape
        bj = lax.broadcasted_iota(jnp.int32, shape, 1)
        tpos = past_len + lax.broadcasted_iota(jnp.int32, shape, 0) % t_len
        cur = tpos // SLC_BLOCK
        forced = (bj == 0) | (bj == cur) | (bj == cur - 1)
        visible = bj * SLC_BLOCK <= tpos
        score = jnp.where(forced, -NEG, jnp.where(visible, imp, NEG))
        score = jnp.where(bj < n_blk, score, 2.0 * NEG)
        sel = jnp.zeros(shape, F32)
        for _ in range(min(N_SEL, n_blk)):
            best = jnp.max(score, axis=1, keepdims=True)
            first = jnp.min(jnp.where(score == best, bj, shape[1]), axis=1, keepdims=True)
            hit = bj == first
            sel = jnp.where(hit, 1.0, sel)
            score = jnp.where(hit, 3.0 * NEG, score)
        sel_ref[0] = sel


def _slc_sample_kernel(pt_ref, cache_hbm, q_ref, sel_ref, knew_ref, bias_ref, c31_ref, biasn_ref, eblk_ref,
                       os_ref, buf, sem, m_scr, l_scr, acc_scr, *, layer):
    c = pl.program_id(1)
    n_c = pl.num_programs(1)
    page = buf.shape[2] // PAGES_PER_STEP

    def page_dma(b, cc, j, slot):
        pg = pt_ref[b, cc * PAGES_PER_STEP + j]
        return pltpu.make_async_copy(cache_hbm.at[layer, pg], buf.at[slot, :, pl.ds(j * page, page)], sem.at[slot])

    slot = _gather_step(page_dma)
    q64 = _sample_queries(q_ref[0])

    @pl.when(c == 0)
    def _init():
        m_scr[...] = jnp.full(m_scr.shape, NEG, F32)
        l_scr[...] = jnp.zeros(l_scr.shape, F32)
        acc_scr[...] = jnp.zeros(acc_scr.shape, F32)

    def update(s, vt_bf):
        m_old = m_scr[...]
        m_new = jnp.maximum(m_old, jnp.max(s, axis=1, keepdims=True))
        alpha = jnp.exp(m_old - m_new)
        e = jnp.exp(s - m_new)
        l_scr[...] = alpha * l_scr[...] + jnp.sum(e, axis=1, keepdims=True)
        acc_scr[...] = alpha * acc_scr[...] + _dot_nt(e.astype(BF16), vt_bf)
        m_scr[...] = m_new

    neg = (_dot(sel_ref[0, 0].astype(BF16), eblk_ref[...]) - 1.0) * (-NEG)
    s = _dot(q64, buf[slot, 0:KV_W, :].astype(BF16)) + neg

    @pl.when(c < n_c - 1)
    def _far():
        update(s + c31_ref[...], buf[slot, KV_W:KV2, :].astype(BF16))

    @pl.when(c == n_c - 1)
    def _last():
        update(s + bias_ref[...], buf[slot, KV_W:KV2, :].astype(BF16))
        kn = knew_ref[0]
        update(_dot(q64, kn[0:KV_W, :].astype(BF16)) + biasn_ref[...], kn[KV_W:KV2, :].astype(BF16))
        valid = m_scr[...] > 0.5 * NEG
        os_ref[0] = jnp.where(valid, acc_scr[...] / jnp.where(valid, l_scr[...], 1.0), 0.0)


def _win_sample_kernel(q_ref, kvw_ref, biasw_ref, g_ref, oc_ref, os_ref, o_ref, *, t_len):
    q64 = _sample_queries(q_ref[0])
    kw = kvw_ref[0]
    p = _softmax_rows(_dot(q64, kw[0:KV_W, :].astype(BF16)) + biasw_ref[...])
    ow = _dot_nt(p.astype(BF16), kw[KV_W:KV2, :].astype(BF16))
    gates = g_ref[0]
    oc, osel = oc_ref[0], os_ref[0]
    lane = lax.broadcasted_iota(jnp.int32, (t_len, LANES), 1)

    def head(g, r):
        h = g * HPG + r
        sl = slice(h * t_len, (h + 1) * t_len)
        return (gates[:, 3 * h:3 * h + 1] * oc[sl] + gates[:, 3 * h + 1:3 * h + 2] * osel[sl]
                + gates[:, 3 * h + 2:3 * h + 3] * ow[sl])

    for r in range(HPG):
        o_ref[0, :, r * LANES:(r + 1) * LANES] = jnp.where(lane < HEAD_DIM, head(0, r), head(1, r))


def _nsa_sample(layer, q, gates, kvs_new_t, all_w_t, cmp_t, slc_t, page_table, cw, tables):
    b, t_len, _ = q.shape
    n_pages = page_table.shape[1]
    page = cmp_t.shape[3]
    past_len = n_pages * page
    n_c = n_pages // PAGES_PER_STEP
    n_ch = past_len // CMP_STRIDE
    n_blk = -(-(past_len + t_len) // SLC_BLOCK)
    blk_pad = -(-n_blk // LANES) * LANES
    rows = PAGES_PER_STEP * page
    blk_step = rows // SLC_BLOCK
    nq = N_HEADS * t_len
    assert past_len % SLC_BLOCK == 0 and t_len <= SLC_BLOCK and (past_len + t_len) // CMP_STRIDE == n_ch
    biasc, biass, c31, biasn, biasw = tables
    c31 = c31[:, :1]
    mimp = jnp.asarray(_imp_matrix(n_ch, blk_pad, (past_len + t_len) // CMP_STRIDE - 1), BF16)
    cpp = page // CMP_STRIDE
    pm = np.zeros((page, page), np.float32)
    for p in range(CMP_STRIDE):
        for cc in range(cpp):
            pm[p * cpp + cc, cc * CMP_STRIDE + p] = 1.0
    pm = jnp.asarray(pm, BF16)

    def full(a):
        nd = a.ndim
        return pl.BlockSpec(a.shape, lambda i, j, pt, nd=nd: (0,) * nd)

    q_spec = pl.BlockSpec((1, t_len, D_ATT), lambda i, j, pt: (i, 0, 0))
    o_spec = pl.BlockSpec((1, nq, KV_W), lambda i, j, pt: (i, 0, 0))
    any_spec = pl.BlockSpec(memory_space=pl.ANY)
    oc, sel = pl.pallas_call(
        functools.partial(_cmp_sample_kernel, layer=layer, t_len=t_len, past_len=past_len, n_blk=n_blk),
        grid_spec=pltpu.PrefetchScalarGridSpec(
            num_scalar_prefetch=1, grid=(b, n_c),
            in_specs=[any_spec, q_spec, full(pm), full(cw["w1"]), full(cw["pos"]), full(cw["b1"]), full(cw["w2"]),
                      full(cw["b2"]), full(biasc), full(mimp)],
            out_specs=[o_spec, pl.BlockSpec((1, N_KV * t_len, blk_pad), lambda i, j, pt: (i, 0, 0))],
            scratch_shapes=[pltpu.VMEM((2, PAGES_PER_STEP, KV2, page), F32),
                            pltpu.SemaphoreType.DMA((2,)),
                            pltpu.VMEM((PAGES_PER_STEP * cpp, CMP_STRIDE * KV2), F32),
                            pltpu.VMEM((n_ch, 4 * CMP_HIDDEN), F32), pltpu.VMEM((n_ch, 4 * CMP_HIDDEN), F32)]),
        out_shape=[jax.ShapeDtypeStruct((b, nq, KV_W), F32),
                   jax.ShapeDtypeStruct((b, N_KV * t_len, blk_pad), F32)],
        compiler_params=_cparams(("arbitrary", "arbitrary")),
        name="nsa_sample_cmp",
    )(page_table, cmp_t, q, pm, cw["w1"], cw["pos"], cw["b1"], cw["w2"], cw["b2"], biasc, mimp)

    sel_c = sel[:, :, :n_c * blk_step].reshape(b, N_KV, 1, t_len, n_c, blk_step)
    sel_c = jnp.broadcast_to(sel_c, (b, N_KV, HPG, t_len, n_c, blk_step))
    sel_c = sel_c.transpose(0, 4, 1, 2, 3, 5).reshape(b, n_c, nq, blk_step)
    sel_c = jnp.pad(sel_c, ((0, 0), (0, 0), (0, 0), (0, LANES - blk_step)))
    eblk = jnp.asarray(np.arange(LANES)[:, None] == (np.arange(rows)[None, :] // SLC_BLOCK), BF16)
    o_s = pl.pallas_call(
        functools.partial(_slc_sample_kernel, layer=layer),
        grid_spec=pltpu.PrefetchScalarGridSpec(
            num_scalar_prefetch=1, grid=(b, n_c),
            in_specs=[any_spec, q_spec,
                      pl.BlockSpec((1, 1, nq, LANES), lambda i, j, pt: (i, j, 0, 0)),
                      pl.BlockSpec((1, KV2, LANES), lambda i, j, pt: (i, 0, 0)),
                      full(biass), full(c31), full(biasn), full(eblk)],
            out_specs=o_spec,
            scratch_shapes=[pltpu.VMEM((2, KV2, rows), F32),
                            pltpu.SemaphoreType.DMA((2,)),
                            pltpu.VMEM((nq, 1), F32), pltpu.VMEM((nq, 1), F32), pltpu.VMEM((nq, KV_W), F32)]),
        out_shape=jax.ShapeDtypeStruct((b, nq, KV_W), F32),
        compiler_params=_cparams(("arbitrary", "arbitrary")),
        name="nsa_sample_slc",
    )(page_table, slc_t, q, sel_c, kvs_new_t, biass, c31, biasn, eblk)

    w_pad = all_w_t.shape[2]
    return pl.pallas_call(
        functools.partial(_win_sample_kernel, t_len=t_len),
        grid=(b,),
        in_specs=[pl.BlockSpec((1, t_len, D_ATT), lambda i: (i, 0, 0)),
                  pl.BlockSpec((1, KV2, w_pad), lambda i: (i, 0, 0)),
                  pl.BlockSpec(biasw.shape, lambda i: (0, 0)),
                  pl.BlockSpec((1, t_len, GATE_PAD), lambda i: (i, 0, 0)),
                  pl.BlockSpec((1, nq, KV_W), lambda i: (i, 0, 0)),
                  pl.BlockSpec((1, nq, KV_W), lambda i: (i, 0, 0))],
        out_specs=pl.BlockSpec((1, t_len, D_ATT), lambda i: (i, 0, 0)),
        out_shape=jax.ShapeDtypeStruct((b, t_len, D_ATT), F32),
        compiler_params=_cparams(("parallel",)),
        name="nsa_sample_win",
    )(q, all_w_t, biasw, gates, oc, o_s)


def _sample_layer(layer, h_s, lw, s5m, tables, alpha, cmp_t, slc_t, win_t, h0_re, h0_im, page_table, w_pad):
    b, t_len, _ = h_s.shape
    w_seq, cw, ew = lw
    m = b * t_len
    x2d = h_s.reshape(m, D_MODEL)
    q, kvc, kvs, kvw, kvct, kvst, kvwt, gates, u = _inproj(x2d, w_seq, m)
    r3 = lambda a: a.reshape(b, t_len, a.shape[-1])
    per_row = lambda a: a.reshape(KV2, b, t_len).transpose(1, 0, 2)
    w_buf = win_t.shape[2]
    all_w_t = jnp.concatenate([win_t, per_row(kvwt)], axis=2)
    all_w_pad = jnp.pad(all_w_t, ((0, 0), (0, 0), (0, w_pad - (w_buf + t_len))))
    kvs_new_t = jnp.pad(per_row(kvst), ((0, 0), (0, 0), (0, LANES - t_len)))
    o_att = _nsa_sample(layer, r3(q), r3(gates), kvs_new_t, all_w_pad, cmp_t, slc_t, page_table, cw, tables)
    y, hr, hi = _s5(r3(u), h0_re, h0_im, s5m, t_len, bt=b)
    out = _epilogue(x2d, o_att.reshape(m, D_ATT), y.reshape(m, D_SSM), u, ew, alpha)
    return out.reshape(b, t_len, D_MODEL), r3(kvc), r3(kvs), all_w_t[:, :, t_len:], hr, hi


def _feature_major(a):
    nd = a.ndim
    perm = tuple(range(nd - 4)) + (nd - 3, nd - 2, nd - 1, nd - 4)
    t = a.transpose(perm)
    return t.reshape(t.shape[:nd - 4] + (KV2, a.shape[nd - 4]))


def _row_major(a):
    nd = a.ndim
    t = a.reshape(a.shape[:nd - 2] + (2, N_KV, HEAD_DIM, a.shape[-1]))
    perm = tuple(range(nd - 2)) + (nd + 1, nd - 2, nd - 1, nd)
    return t.transpose(perm)


def kernel(x_prompt, x_sample, cache_kv_cmp, cache_kv_slc, state_win_kv, state_ssm_re, state_ssm_im,
           page_table, rel_bias, w_in, cmp_w1, cmp_b1, cmp_w2, cmp_b2, cmp_pos,
           ssm_a_re, ssm_a_im, ssm_log_dt, ssm_b_re, ssm_b_im, ssm_c_re, ssm_c_im, ssm_d,
           ssm_glu_w, ssm_glu_b, w_att_out, w_ssm_out, w_o, ln_g, ln_b):
    depth = w_in.shape[0]
    alpha = (2 * depth) ** 0.25
    bp, seq = x_prompt.shape[:2]
    bs, t_len = x_sample.shape[:2]
    page = cache_kv_cmp.shape[2]
    past_len = page_table.shape[1] * page
    w_buf = state_win_kv.shape[2]
    w_pad = -(-(w_buf + t_len) // LANES) * LANES
    n_win = min(WINDOW, seq)
    p_tables = _prompt_tables(rel_bias, seq)
    s_tables = _sample_tables(rel_bias, past_len, t_len, w_buf, w_pad, PAGES_PER_STEP * page)
    cmp_t, slc_t, win_t = _feature_major(cache_kv_cmp), _feature_major(cache_kv_slc), _feature_major(state_win_kv)
    h_p, h_s = x_prompt, x_sample
    outs_p, outs_s = [], []
    for l in range(depth):
        lw = _layer_weights(l, w_in, cmp_w1, cmp_b1, cmp_w2, cmp_b2, cmp_pos, ssm_d, ssm_glu_w, ssm_glu_b,
                            w_att_out, w_ssm_out, w_o, ln_g, ln_b)
        ssm = (ssm_a_re[l], ssm_a_im[l], ssm_log_dt[l], ssm_b_re[l], ssm_b_im[l], ssm_c_re[l], ssm_c_im[l])
        h_p, kvct, kvst, kvwt, hr, hi = _prompt_layer(h_p, lw, _s5_mats(*ssm, S5_CHUNK), p_tables, alpha)
        outs_p.append((kvct, kvst, kvwt[:, :, seq - n_win:], hr, hi))
        h_s, kvc, kvs, win, hr, hi = _sample_layer(
            l, h_s, lw, _s5_mats(*ssm, t_len), s_tables, alpha, cmp_t, slc_t, win_t[l],
            state_ssm_re[l], state_ssm_im[l], page_table, w_pad)
        outs_s.append((kvc.reshape(bs, t_len, 2, N_KV, HEAD_DIM), kvs.reshape(bs, t_len, 2, N_KV, HEAD_DIM),
                       win, hr, hi))
    stack = lambda outs, i: jnp.stack([o[i] for o in outs])
    return ((h_p, h_s)
            + tuple(_row_major(stack(outs_p, i)) for i in range(3)) + (stack(outs_p, 3), stack(outs_p, 4))
            + (stack(outs_s, 0), stack(outs_s, 1), _row_major(stack(outs_s, 2)), stack(outs_s, 3), stack(outs_s, 4)))
```

```python
import functools
import math

import numpy as np
import jax
import jax.numpy as jnp
from jax import lax
from jax.experimental import pallas as pl
from jax.experimental.pallas import tpu as pltpu

F32 = jnp.float32
BF16 = jnp.bfloat16

D_MODEL = 1024
N_HEADS = 8
HEAD_DIM = 64
N_KV = 2
HPG = N_HEADS // N_KV
D_ATT = N_HEADS * HEAD_DIM
KV_W = N_KV * HEAD_DIM
CMP_BLOCK = 32
CMP_STRIDE = 16
CMP_HIDDEN = 128
SLC_BLOCK = 64
N_SEL = 16
WINDOW = 512
S5_GROUP = 16
S5_STATE = 64
D_SSM = 512
N_SSM_GROUPS = D_SSM // S5_GROUP
N_BUCKETS = 32
MAX_DISTANCE = 128
LN_EPS = 1e-5
IN_WIDTHS = (D_ATT, KV_W, KV_W, KV_W, KV_W, KV_W, KV_W, 3 * N_HEADS, D_ATT, D_SSM, D_SSM, 2 * D_MODEL)

LANES = 128
SUBLANES = 8
VMEM_LIMIT = 56 * 1024 * 1024
NEG = -1e30
LOG2E = 1.4426950408889634
VROWS = HEAD_DIM + 16
TQ = 128
TK = 128
FAR_TILES = 4
KV2 = 2 * KV_W
GATE_PAD = LANES
S5_CHUNK = 8
S5_LANE_GROUPS = LANES // S5_GROUP
S5_BATCH_ROWS = 4
SEQ_COLS = D_ATT + 3 * KV2 + GATE_PAD + D_SSM
QCOLS = N_HEADS * TQ
GCOLS = HPG * TQ
PAGES_PER_STEP = 32


def _cparams(sem):
    return pltpu.CompilerParams(dimension_semantics=sem, vmem_limit_bytes=VMEM_LIMIT)


def _dot(a, b):
    return jnp.dot(a, b, preferred_element_type=F32)


def _dot_nt(a, b):
    return lax.dot_general(a, b, (((1,), (1,)), ((), ())), preferred_element_type=F32)


def _split3(x):
    h1 = x.astype(BF16)
    r1 = x - h1.astype(F32)
    h2 = r1.astype(BF16)
    h3 = (r1 - h2.astype(F32)).astype(BF16)
    return h1, h2, h3


def _bucket_np(dist):
    n = np.maximum(dist, 0)
    max_exact = N_BUCKETS // 2
    nf = np.maximum(n, 1).astype(np.float32)
    val = (np.log(nf / np.float32(max_exact)) / np.float32(math.log(MAX_DISTANCE / max_exact))
           * np.float32(N_BUCKETS - max_exact))
    large = max_exact + val.astype(np.int32)
    return np.where(n < max_exact, n, np.minimum(large, N_BUCKETS - 1)).astype(np.int32)


def _pair_perm():
    cols = []
    for r in range(HPG):
        cols += list(range(r * HEAD_DIM, (r + 1) * HEAD_DIM))
        cols += list(range((HPG + r) * HEAD_DIM, (HPG + r + 1) * HEAD_DIM))
    return np.asarray(cols, np.int32)


def _imp_matrix(n_cmp_pad, n_blk_pad, n_cmp):
    ratio = SLC_BLOCK // CMP_STRIDE
    n = np.arange(n_cmp_pad)[:, None]
    j = np.arange(n_blk_pad)[None, :]
    m = ((n >= ratio * j - 1) & (n <= ratio * j + ratio - 2)).astype(np.float32)
    m += ((n >= ratio * j) & (n <= ratio * j + ratio - 1)).astype(np.float32)
    m *= (n < n_cmp)
    return m


def _bucket_starts():
    b = _bucket_np(np.arange(2 * MAX_DISTANCE))
    assert (np.diff(b) >= 0).all() and b.max() == N_BUCKETS - 1
    starts = [int(np.argmax(b >= k)) for k in range(1, N_BUCKETS)]
    assert all(b[s] == k for k, s in zip(range(1, N_BUCKETS), starts))
    return starts


def _bias_of_dist(d, bucket_value):
    val = bucket_value(0)
    for k, start in zip(range(1, N_BUCKETS), _bucket_starts()):
        val = jnp.where(d >= start, bucket_value(k), val)
    return val


def _prompt_tables_kernel(rb_ref, tn_ref, t4_ref, cm_ref, *, row_shift):
    bias = lambda d: _bias_of_dist(d, lambda k: rb_ref[k:k + 1, :])
    last = rb_ref[N_BUCKETS - 1:N_BUCKETS, :]
    shape = (TK, QCOLS)
    d0 = lax.broadcasted_iota(jnp.int32, shape, 1) % TQ - lax.broadcasted_iota(jnp.int32, shape, 0)
    tn_ref[0:TK, :] = (bias(TK + d0) - last) * LOG2E
    tn_ref[TK:2 * TK, :] = jnp.where(d0 >= 0, (bias(d0) - last) * LOG2E, NEG)
    t4_ref[...] = jnp.where(d0 <= 0, 0.0, NEG)
    shape = cm_ref.shape
    nn = lax.broadcasted_iota(jnp.int32, shape, 0) - row_shift
    dc = lax.broadcasted_iota(jnp.int32, shape, 1) % TQ - (nn * CMP_STRIDE + CMP_BLOCK - 1)
    cm_ref[...] = jnp.where(dc >= 0, bias(dc) * LOG2E, NEG)


def _prompt_tables(rel_bias, seq):
    assert WINDOW == 4 * TK and TQ == TK and _bucket_starts()[-1] <= TK
    n_qt = seq // TQ
    n_ch = seq // CMP_STRIDE
    row_shift = (n_qt - 1) * (TQ // CMP_STRIDE)
    tile = jax.ShapeDtypeStruct((TK, QCOLS), F32)
    return pl.pallas_call(
        functools.partial(_prompt_tables_kernel, row_shift=row_shift),
        out_shape=[jax.ShapeDtypeStruct((2 * TK, QCOLS), F32), tile,
                   jax.ShapeDtypeStruct((n_ch + row_shift, QCOLS), F32)],
        compiler_params=pltpu.CompilerParams(vmem_limit_bytes=VMEM_LIMIT),
        name="prompt_tables",
    )(jnp.repeat(rel_bias, TQ, axis=1))


def _sample_tables_kernel(rb_ref, bc_ref, bs_ref, c31_ref, bn_ref, bw_ref, *, past_len, t_len, w_buf):
    bias = lambda d: _bias_of_dist(d, lambda k: rb_ref[:, k:k + 1]) * LOG2E
    n_tok = past_len + t_len

    def grid(ref):
        qpos = past_len + lax.broadcasted_iota(jnp.int32, ref.shape, 0) % t_len
        return qpos, lax.broadcasted_iota(jnp.int32, ref.shape, 1)

    qpos, nn = grid(bc_ref)
    dc = qpos - (nn * CMP_STRIDE + CMP_BLOCK - 1)
    bc_ref[...] = jnp.where((dc >= 0) & (nn < n_tok // CMP_STRIDE - 1), bias(dc), NEG)
    qpos, ll = grid(bs_ref)
    bs_ref[...] = bias(qpos - (past_len - bs_ref.shape[1] + ll))
    c31_ref[...] = jnp.broadcast_to(rb_ref[:, N_BUCKETS - 1:N_BUCKETS] * LOG2E, c31_ref.shape)
    qpos, ll = grid(bn_ref)
    dn = qpos - (past_len + ll)
    bn_ref[...] = jnp.where((dn >= 0) & (ll < t_len), bias(dn), NEG)
    qpos, ii = grid(bw_ref)
    w_pos = n_tok - (w_buf + t_len) + ii
    dw = qpos - w_pos
    bw_ref[...] = jnp.where((dw >= 0) & (dw <= WINDOW) & (w_pos >= 0) & (ii < w_buf + t_len), bias(dw), NEG)


def _sample_tables(rel_bias, past_len, t_len, w_buf, w_pad, last_rows):
    assert _bucket_starts()[-1] <= last_rows
    nq = N_HEADS * t_len
    shp = lambda n: jax.ShapeDtypeStruct((nq, n), F32)
    return pl.pallas_call(
        functools.partial(_sample_tables_kernel, past_len=past_len, t_len=t_len, w_buf=w_buf),
        out_shape=[shp(past_len // CMP_STRIDE), shp(last_rows), shp(LANES), shp(LANES), shp(w_pad)],
        compiler_params=pltpu.CompilerParams(vmem_limit_bytes=VMEM_LIMIT),
        name="sample_tables",
    )(jnp.repeat(rel_bias.T, t_len, axis=0))


def _inproj_kernel(x_ref, w_ref, q_ref, kvc_ref, kvs_ref, kvw_ref, kvct_ref, kvst_ref, kvwt_ref, g_ref, u_ref):
    h = _dot(x_ref[...].astype(BF16), w_ref[...])
    o = 0
    q_ref[...] = h[:, o:o + D_ATT] * (HEAD_DIM ** -0.5 * LOG2E)
    o += D_ATT
    for row_ref, t_ref in ((kvc_ref, kvct_ref), (kvs_ref, kvst_ref), (kvw_ref, kvwt_ref)):
        kv = h[:, o:o + KV2]
        row_ref[...] = kv
        t_ref[0] = kv.T
        o += KV2
    g_ref[...] = jax.nn.sigmoid(h[:, o:o + GATE_PAD])
    o += GATE_PAD
    u_ref[...] = h[:, o:o + D_SSM]


def _inproj(x2d, w_seq, rows_per_seq):
    m = x2d.shape[0]
    tm = min(512, m)
    tps = rows_per_seq // tm
    tok = lambda w: pl.BlockSpec((tm, w), lambda i: (i, 0))
    ft = pl.BlockSpec((1, KV2, tm), lambda i: (i // tps, 0, i % tps))
    tok_shape = lambda w: jax.ShapeDtypeStruct((m, w), F32)
    ft_shape = jax.ShapeDtypeStruct((m // rows_per_seq, KV2, rows_per_seq), F32)
    return pl.pallas_call(
        _inproj_kernel,
        grid=(m // tm,),
        in_specs=[tok(D_MODEL), pl.BlockSpec((D_MODEL, SEQ_COLS), lambda i: (0, 0))],
        out_specs=[tok(D_ATT), tok(KV2), tok(KV2), tok(KV2), ft, ft, ft, tok(GATE_PAD), tok(D_SSM)],
        out_shape=[tok_shape(D_ATT), tok_shape(KV2), tok_shape(KV2), tok_shape(KV2), ft_shape, ft_shape, ft_shape,
                   tok_shape(GATE_PAD), tok_shape(D_SSM)],
        compiler_params=_cparams(("parallel",)),
        name="inproj",
    )(x2d, w_seq)


def _compress_out(hf, hs_next, c1, w2_ref, b2_ref):
    h = hf + hs_next + c1
    return _dot(jax.nn.silu(h).astype(BF16), w2_ref[...]) + b2_ref[...]


def _compress_hidden(xk, xv, w1_ref):
    hk = _dot(xk, w1_ref[0])
    hv = _dot(xv, w1_ref[1])
    nh = N_KV * CMP_HIDDEN
    return (jnp.concatenate([hk[:, :nh], hv[:, :nh]], axis=1), jnp.concatenate([hk[:, nh:], hv[:, nh:]], axis=1))


def _compress_const(pos_ref, w1_ref, b1_ref):
    hf, hs = _compress_hidden(pos_ref[0].astype(BF16), pos_ref[1].astype(BF16), w1_ref)
    return hf[0:1] + hs[1:2] + b1_ref[...]


def _value_rows(v_scr, g, vt):
    n = v_scr.shape[2]
    pad = lax.broadcasted_iota(jnp.int32, (VROWS - HEAD_DIM, n), 0) == 0
    v_scr[g, 0:HEAD_DIM, :] = vt.astype(BF16)
    v_scr[g, HEAD_DIM:VROWS, :] = jnp.where(pad, 1.0, 0.0).astype(BF16)


def _flash_init(m_ref, acc_ref):
    m_ref[...] = jnp.full(m_ref.shape, NEG, F32)
    acc_ref[...] = jnp.zeros(acc_ref.shape, F32)


def _flash_block(k_scr, v_scr, qb_scr, k0, tk, head_ref, tail_ref, neg_scr, m_scr, acc_scr):
    st = _dot_nt(k_scr[pl.ds(k0, tk), :], qb_scr[...])
    if tail_ref is not None:
        nt = min(tail_ref.shape[0], tk)
        tail = st[tk - nt:] + tail_ref[tail_ref.shape[0] - nt:, :]
        st = tail if nt == tk else jnp.concatenate([st[:tk - nt], tail], axis=0)
    if head_ref is not None:
        nh = head_ref.shape[0]
        st = jnp.concatenate([st[:nh] + head_ref[...], st[nh:]], axis=0)
    if neg_scr is not None:
        ng = neg_scr[pl.ds(k0, tk), :]
        st = st + jnp.concatenate([ng[:, :TQ]] * HPG + [ng[:, TQ:]] * HPG, axis=1)
    m_old = m_scr[...]
    m_new = jnp.maximum(m_old, jnp.max(st, axis=0, keepdims=True))
    e = jnp.exp2(st - m_new).astype(BF16)
    pv = jnp.concatenate([_dot(v_scr[g, :, pl.ds(k0, tk)], e[:, g * GCOLS:(g + 1) * GCOLS])
                          for g in range(N_KV)], axis=1)
    acc_scr[...] = jnp.exp2(m_old - m_new) * acc_scr[...] + pv
    m_scr[...] = m_new


def _flash_add(tot_scr, m_scr, acc_scr, gate):
    valid = m_scr[...] > 0.5 * NEG
    coef = jnp.where(valid, 1.0 / jnp.where(valid, acc_scr[HEAD_DIM:HEAD_DIM + 1, :], 1.0), 0.0)
    tot_scr[...] = tot_scr[...] + acc_scr[0:HEAD_DIM, :] * (coef * gate)


def _nsa_prompt_kernel(q_ref, kck_ref, kcv_ref, ks_ref, kw_ref, vst_ref, vwt_ref, g_ref,
                       w1_ref, pos_ref, b1_ref, w2_ref, b2_ref,
                       cmaster_ref, tn_ref, t4_ref, mimp_ref, eblk_ref,
                       o_ref,
                       kc_scr, vc_scr, ks_scr, vs_scr, kw_scr, vw_scr, qb_scr, neg_scr, m_scr, acc_scr, tot_scr):
    qt = pl.program_id(1)
    n_qt = pl.num_programs(1)
    seq = ks_ref.shape[1]
    n_ch = seq // CMP_STRIDE
    n_blk = seq // SLC_BLOCK
    nh = 4 * CMP_HIDDEN

    @pl.when(qt == 0)
    def _per_batch():
        flat = lambda ref: jnp.concatenate(
            [ref[0, pl.ds(p, n_ch, stride=CMP_STRIDE), :] for p in range(CMP_STRIDE)], axis=1).astype(BF16)
        hf, hs = _compress_hidden(flat(kck_ref), flat(kcv_ref), w1_ref)
        c1 = _compress_const(pos_ref, w1_ref, b1_ref)
        hs_next = pltpu.roll(hs, n_ch - 1, 0)
        kv = _compress_out(hf, hs_next, c1, w2_ref, b2_ref)
        kc_scr[...] = kv[:, :KV_W].astype(BF16)
        vct = kv[:, KV_W:].T
        ks_scr[...] = ks_ref[0].astype(BF16)
        kw_scr[...] = kw_ref[0].astype(BF16)
        for g in range(N_KV):
            rows = slice(g * HEAD_DIM, (g + 1) * HEAD_DIM)
            _value_rows(vc_scr, g, vct[rows, :])
            _value_rows(vs_scr, g, vst_ref[0, rows, :])
            _value_rows(vw_scr, g, vwt_ref[0, rows, :])

    q = q_ref[0]
    gt = g_ref[0].T
    lane = lax.broadcasted_iota(jnp.int32, (TQ, LANES), 1)
    for h in range(N_HEADS):
        g, r = divmod(h, HPG)
        gmask = (lane < HEAD_DIM) if g == 0 else (lane >= HEAD_DIM)
        qb_scr[h * TQ:(h + 1) * TQ, :] = jnp.where(gmask, q[:, r * LANES:(r + 1) * LANES], 0.0).astype(BF16)
    q0 = qt * TQ

    def gate_cols(j):
        return jnp.concatenate([gt[3 * h + j:3 * h + j + 1, :] for h in range(N_HEADS)], axis=1)

    r0 = pl.multiple_of((n_qt - 1 - qt) * (TQ // CMP_STRIDE), SUBLANES)
    sc = _dot_nt(kc_scr[...], qb_scr[...]) + cmaster_ref[pl.ds(r0, n_ch), :]
    m = jnp.max(sc, axis=0, keepdims=True)
    e = jnp.exp2(sc - m)
    pt = jnp.where(m > 0.5 * NEG, e / jnp.sum(e, axis=0, keepdims=True), 0.0)
    pb = pt.astype(BF16)
    oc = jnp.concatenate([_dot(vc_scr[g, 0:HEAD_DIM, :], pb[:, g * GCOLS:(g + 1) * GCOLS]) for g in range(N_KV)],
                         axis=1)
    tot_scr[...] = oc * gate_cols(0)

    psum = []
    for g in range(N_KV):
        acc = pt[:, g * GCOLS:g * GCOLS + TQ]
        for r in range(1, HPG):
            acc = acc + pt[:, g * GCOLS + r * TQ:g * GCOLS + (r + 1) * TQ]
        psum.append(acc)
    p1, p2, p3 = _split3(jnp.concatenate(psum, axis=1))
    imp = _dot(mimp_ref[...], p1) + _dot(mimp_ref[...], p2) + _dot(mimp_ref[...], p3)
    shape = (n_blk, N_KV * TQ)
    bj = lax.broadcasted_iota(jnp.int32, shape, 0)
    tpos = q0 + lax.broadcasted_iota(jnp.int32, shape, 1) % TQ
    cur = tpos // SLC_BLOCK
    forced = (bj == 0) | (bj == cur) | (bj == cur - 1)
    visible = bj * SLC_BLOCK <= tpos
    score = jnp.where(forced, -NEG, jnp.where(visible, imp, NEG))
    rank = jnp.zeros(shape, F32)
    for i in range(n_blk):
        si = score[i:i + 1, :]
        ahead = (si > score) | ((si == score) & (bj > i))
        rank = rank + jnp.where(ahead, 1.0, 0.0)
    unsel = jnp.where(rank < float(N_SEL), 0.0, 1.0).astype(BF16)
    neg_scr[...] = _dot(eblk_ref[...], unsel)

    _flash_init(m_scr, acc_scr)

    def sel_block(k0, n_tiles, tail_ref):
        _flash_block(ks_scr, vs_scr, qb_scr, k0, n_tiles * TK, None, tail_ref, neg_scr, m_scr, acc_scr)

    n_far = jnp.maximum(qt - 1, 0)

    def far_body(j, carry):
        sel_block(pl.multiple_of(j * FAR_TILES * TK, FAR_TILES * TK), FAR_TILES, None)
        return carry

    lax.fori_loop(0, n_far // FAR_TILES, far_body, 0)
    for rem in range(FAR_TILES):
        @pl.when((qt >= 1) & (n_far % FAR_TILES == rem))
        def _sel_last(rem=rem):
            sel_block(pl.multiple_of((qt - 1 - rem) * TK, TK), rem + 2, tn_ref)

    @pl.when(qt == 0)
    def _sel_first():
        sel_block(0, 1, tn_ref)

    _flash_add(tot_scr, m_scr, acc_scr, gate_cols(1))

    _flash_init(m_scr, acc_scr)
    n_win = WINDOW // TK
    for n_prev in range(n_win + 1):
        cond = (qt == n_prev) if n_prev < n_win else (qt >= n_win)

        @pl.when(cond)
        def _win(n_prev=n_prev):
            _flash_block(kw_scr, vw_scr, qb_scr, pl.multiple_of((qt - n_prev) * TK, TK), (n_prev + 1) * TK,
                         t4_ref if n_prev == n_win else None, tn_ref, None, m_scr, acc_scr)

    _flash_add(tot_scr, m_scr, acc_scr, gate_cols(2))
    tot = tot_scr[...]
    for r in range(HPG):
        blk = jnp.concatenate([tot[:, r * TQ:(r + 1) * TQ], tot[:, GCOLS + r * TQ:GCOLS + (r + 1) * TQ]], axis=0)
        o_ref[0, :, r * LANES:(r + 1) * LANES] = blk.T


def _nsa_prompt(q, kvc, kvs, kvw, kvst, kvwt, gates, cw, tables):
    b, seq, _ = q.shape
    n_qt = seq // TQ
    n_ch = seq // CMP_STRIDE
    n_blk = seq // SLC_BLOCK
    tn, t4, cmaster = tables
    mimp = jnp.asarray(_imp_matrix(n_ch, n_blk, n_ch - 1).T, BF16)
    in_blk = np.arange(seq)[:, None] // SLC_BLOCK == np.arange(n_blk)[None, :]
    eblk = jnp.asarray(np.where(in_blk, NEG, 0.0), BF16)

    def full(a):
        nd = a.ndim
        return pl.BlockSpec(a.shape, lambda i, j, nd=nd: (0,) * nd)

    k_spec = pl.BlockSpec((1, seq, KV_W), lambda i, j: (i, 0, 0))
    vt_spec = pl.BlockSpec((1, KV_W, seq), lambda i, j: (i, 1, 0))
    return pl.pallas_call(
        _nsa_prompt_kernel,
        grid=(b, n_qt),
        in_specs=[pl.BlockSpec((1, TQ, D_ATT), lambda i, j: (i, j, 0)),
                  k_spec, pl.BlockSpec((1, seq, KV_W), lambda i, j: (i, 0, 1)),
                  k_spec, k_spec, vt_spec, vt_spec,
                  pl.BlockSpec((1, TQ, GATE_PAD), lambda i, j: (i, j, 0)),
                  full(cw["w1"]), full(cw["pos"]), full(cw["b1"]), full(cw["w2"]), full(cw["b2"]),
                  full(cmaster), full(tn), full(t4), full(mimp), full(eblk)],
        out_specs=pl.BlockSpec((1, TQ, D_ATT), lambda i, j: (i, j, 0)),
        out_shape=jax.ShapeDtypeStruct((b, seq, D_ATT), F32),
        scratch_shapes=[pltpu.VMEM((n_ch, KV_W), BF16), pltpu.VMEM((N_KV, VROWS, n_ch), BF16),
                        pltpu.VMEM((seq, KV_W), BF16), pltpu.VMEM((N_KV, VROWS, seq), BF16),
                        pltpu.VMEM((seq, KV_W), BF16), pltpu.VMEM((N_KV, VROWS, seq), BF16),
                        pltpu.VMEM((QCOLS, LANES), BF16),
                        pltpu.VMEM((seq, N_KV * TQ), F32),
                        pltpu.VMEM((1, QCOLS), F32), pltpu.VMEM((VROWS, QCOLS), F32),
                        pltpu.VMEM((HEAD_DIM, QCOLS), F32)],
        compiler_params=_cparams(("arbitrary", "arbitrary")),
        name="nsa_prompt",
    )(q, kvc, kvc, kvs, kvw, kvst, kvwt, gates, cw["w1"], cw["pos"], cw["b1"], cw["w2"], cw["b2"],
      cmaster, tn, t4, mimp, eblk)


def _s5_kernel(u_ref, h0_ref, v_ref, m_ref, wh_ref, a1_ref, a2_ref, y_ref, ht_ref, z_scr, *, n_chunks, bt):
    gc = u_ref.shape[1]
    sw = 2 * S5_STATE
    for g in range(gc):
        z_scr[:, g * sw:(g + 1) * sw] = _dot(u_ref[0, g], v_ref[g])
    a1 = a1_ref[...]
    a2 = a2_ref[...]

    def step(k, h):
        r0 = pl.multiple_of(k * bt, bt)
        z = z_scr[pl.ds(r0, bt), :]
        z_scr[pl.ds(r0, bt), :] = h
        hsw = jnp.concatenate(
            [pltpu.roll(h[:, g * sw:(g + 1) * sw], S5_STATE, 1) for g in range(gc)], axis=1)
        return a1 * h + a2 * hsw + z

    ht_ref[0] = lax.fori_loop(0, n_chunks, step, h0_ref[0])
    for g in range(gc):
        y_ref[0, g] = (_dot(u_ref[0, g], m_ref[g])
                       + _dot(z_scr[:, g * sw:(g + 1) * sw].astype(BF16), wh_ref[g]))


def _s5_mats(a_re, a_im, log_dt, b_re, b_im, c_re, c_im, chunk):
    hp = lax.Precision.HIGHEST
    dt = jnp.exp(log_dt)[:, None]
    mag = jnp.exp(dt * a_re)
    ab_re, ab_im = mag * jnp.cos(dt * a_im), mag * jnp.sin(dt * a_im)
    den = a_re * a_re + a_im * a_im
    x_re, x_im = ab_re - 1.0, ab_im
    k_re = (x_re * a_re + x_im * a_im) / den
    k_im = (x_im * a_re - x_re * a_im) / den
    bb_re = k_re[..., None] * b_re - k_im[..., None] * b_im
    bb_im = k_re[..., None] * b_im + k_im[..., None] * b_re
    pw_re, pw_im = [jnp.ones_like(ab_re)], [jnp.zeros_like(ab_im)]
    for _ in range(chunk):
        pr, pi = pw_re[-1], pw_im[-1]
        pw_re.append(pr * ab_re - pi * ab_im)
        pw_im.append(pr * ab_im + pi * ab_re)
    pw_re, pw_im = jnp.stack(pw_re), jnp.stack(pw_im)
    cl_re = c_re[None] * pw_re[:, :, None, :] - c_im[None] * pw_im[:, :, None, :]
    cl_im = c_re[None] * pw_im[:, :, None, :] + c_im[None] * pw_re[:, :, None, :]
    t = (jnp.einsum('tgcp,gpd->tgcd', cl_re, bb_re, precision=hp)
         - jnp.einsum('tgcp,gpd->tgcd', cl_im, bb_im, precision=hp))
    s_i = np.arange(chunk)[:, None]
    t_i = np.arange(chunk)[None, :]
    lag = np.clip(t_i - s_i, 0, None)
    blk = jnp.where((t_i >= s_i)[:, :, None, None, None], t[lag], 0.0)
    ng = a_re.shape[0]
    m = blk.transpose(2, 0, 4, 1, 3).reshape(ng, chunk * S5_GROUP, chunk * S5_GROUP)
    rev = chunk - 1 - np.arange(chunk)
    vb_re = pw_re[rev][:, :, :, None] * bb_re[None] - pw_im[rev][:, :, :, None] * bb_im[None]
    vb_im = pw_re[rev][:, :, :, None] * bb_im[None] + pw_im[rev][:, :, :, None] * bb_re[None]
    v = jnp.concatenate([vb_re, vb_im], axis=2).transpose(1, 0, 3, 2).reshape(ng, chunk * S5_GROUP, 2 * S5_STATE)
    wh = jnp.concatenate([cl_re[1:], -cl_im[1:]], axis=3)
    wh = wh.transpose(1, 3, 0, 2).reshape(ng, 2 * S5_STATE, chunk * S5_GROUP)
    a1 = jnp.concatenate([pw_re[chunk], pw_re[chunk]], axis=1).reshape(1, ng * 2 * S5_STATE)
    a2 = jnp.concatenate([-pw_im[chunk], pw_im[chunk]], axis=1).reshape(1, ng * 2 * S5_STATE)
    return v.astype(BF16), m.astype(BF16), wh.astype(BF16), a1, a2


def _s5(u, h0_re, h0_im, mats, chunk, bt, gc=8):
    b, t_len, _ = u.shape
    ng = N_SSM_GROUPS
    n_chunks = t_len // chunk
    nbc = b // bt
    lc = chunk * S5_GROUP
    rows = n_chunks * bt
    sw = 2 * S5_STATE
    v, m, wh, a1, a2 = mats
    ur = u.reshape(nbc, bt, n_chunks, chunk, ng, S5_GROUP).transpose(0, 4, 2, 1, 3, 5)
    ur = ur.reshape(nbc, ng, rows, lc).astype(BF16)
    h0 = jnp.concatenate([h0_re, h0_im], axis=-1).reshape(nbc, bt, ng * sw)
    kern = functools.partial(_s5_kernel, n_chunks=n_chunks, bt=bt)
    y, ht = pl.pallas_call(
        kern,
        grid=(nbc, ng // gc),
        in_specs=[pl.BlockSpec((1, gc, rows, lc), lambda i, j: (i, j, 0, 0)),
                  pl.BlockSpec((1, bt, gc * sw), lambda i, j: (i, 0, j)),
                  pl.BlockSpec((gc, lc, sw), lambda i, j: (j, 0, 0)),
                  pl.BlockSpec((gc, lc, lc), lambda i, j: (j, 0, 0)),
                  pl.BlockSpec((gc, sw, lc), lambda i, j: (j, 0, 0)),
                  pl.BlockSpec((1, gc * sw), lambda i, j: (0, j)),
                  pl.BlockSpec((1, gc * sw), lambda i, j: (0, j))],
        out_specs=[pl.BlockSpec((1, gc, rows, lc), lambda i, j: (i, j, 0, 0)),
                   pl.BlockSpec((1, bt, gc * sw), lambda i, j: (i, 0, j))],
        out_shape=[jax.ShapeDtypeStruct((nbc, ng, rows, lc), F32),
                   jax.ShapeDtypeStruct((nbc, bt, ng * sw), F32)],
        scratch_shapes=[pltpu.VMEM((rows, gc * sw), F32)],
        compiler_params=_cparams(("parallel", "parallel")),
        name="s5_scan",
    )(ur, h0, v, m, wh, a1, a2)
    y = y.reshape(nbc, ng, n_chunks, bt, chunk, S5_GROUP).transpose(0, 3, 2, 4, 1, 5).reshape(b, t_len, D_SSM)
    ht = ht.reshape(b, ng, sw)
    return y, ht[..., :S5_STATE], ht[..., S5_STATE:]


def _s5_nat_kernel(u_ref, h0_ref, v_ref, m_ref, wh_ref, ar_ref, ai_ref, y_ref, ht_ref, uf_scr, z_scr,
                   *, chunk, n_chunks, bt):
    rows = bt * n_chunks
    half = S5_LANE_GROUPS * S5_STATE
    n_tiles = 2 * half // LANES
    for t in range(chunk):
        piece = u_ref[:, pl.ds(t, n_chunks, stride=chunk), :]
        uf_scr[:, t * LANES:(t + 1) * LANES] = piece.reshape(rows, LANES).astype(BF16)
    uf = uf_scr[...]
    z = _dot(uf, v_ref[0])
    for j in range(n_tiles):
        z_scr[j] = z[:, j * LANES:(j + 1) * LANES]
    ar = ar_ref[...]
    ai = ai_ref[...]

    def step(k, h):
        idx = pl.ds(k, bt, stride=n_chunks)
        zk = jnp.concatenate([z_scr[j, idx, :] for j in range(n_tiles)], axis=1)
        for j in range(n_tiles):
            z_scr[j, idx, :] = h[:, j * LANES:(j + 1) * LANES]
        hre, him = h[:, :half], h[:, half:]
        return jnp.concatenate([ar * hre - ai * him, ar * him + ai * hre], axis=1) + zk

    ht_ref[0] = lax.fori_loop(0, n_chunks, step, h0_ref[0])
    hs = jnp.concatenate([z_scr[j] for j in range(n_tiles)], axis=1).astype(BF16)
    y = _dot(uf, m_ref[0]) + _dot(hs, wh_ref[0])
    for t in range(chunk):
        y_ref[:, pl.ds(t, n_chunks, stride=chunk), :] = y[:, t * LANES:(t + 1) * LANES].reshape(bt, n_chunks, LANES)


def _s5_nat_mats(mats, chunk):
    v, m, wh, a1, a2 = mats
    ng = v.shape[0]
    nx = ng // S5_LANE_GROUPS
    eye = jnp.eye(S5_LANE_GROUPS, dtype=v.dtype)
    lk = chunk * LANES
    gw = S5_LANE_GROUPS * 2 * S5_STATE
    v6 = v.reshape(nx, S5_LANE_GROUPS, chunk, S5_GROUP, 2, S5_STATE)
    v8 = jnp.einsum('xgsiqp,gh->xsgiqhp', v6, eye).reshape(nx, lk, gw)
    m6 = m.reshape(nx, S5_LANE_GROUPS, chunk, S5_GROUP, chunk, S5_GROUP)
    m8 = jnp.einsum('xgsito,gh->xsgitho', m6, eye).reshape(nx, lk, lk)
    w6 = wh.reshape(nx, S5_LANE_GROUPS, 2, S5_STATE, chunk, S5_GROUP)
    w8 = jnp.einsum('xgqpto,gh->xqgptho', w6, eye).reshape(nx, gw, lk)
    ar = a1.reshape(nx, S5_LANE_GROUPS, 2, S5_STATE)[:, :, 0].reshape(1, ng * S5_STATE)
    ai = a2.reshape(nx, S5_LANE_GROUPS, 2, S5_STATE)[:, :, 1].reshape(1, ng * S5_STATE)
    return v8, m8, w8, ar, ai


def _s5_nat(u, h0_re, h0_im, mats, chunk, bt):
    b, t_len, _ = u.shape
    n_chunks = t_len // chunk
    nbc = b // bt
    nx = D_SSM // LANES
    half = S5_LANE_GROUPS * S5_STATE
    gw = 2 * half
    lk = chunk * LANES
    rows = bt * n_chunks
    v8, m8, w8, ar, ai = _s5_nat_mats(mats, chunk)
    planar = lambda a: a.reshape(nbc, bt, nx, 1, half)
    h0 = jnp.concatenate([planar(h0_re), planar(h0_im)], axis=3).reshape(nbc, bt, nx * gw)
    y, ht = pl.pallas_call(
        functools.partial(_s5_nat_kernel, chunk=chunk, n_chunks=n_chunks, bt=bt),
        grid=(nbc, nx),
        in_specs=[pl.BlockSpec((bt, t_len, LANES), lambda i, j: (i, 0, j)),
                  pl.BlockSpec((1, bt, gw), lambda i, j: (i, 0, j)),
                  pl.BlockSpec((1, lk, gw), lambda i, j: (j, 0, 0)),
                  pl.BlockSpec((1, lk, lk), lambda i, j: (j, 0, 0)),
                  pl.BlockSpec((1, gw, lk), lambda i, j: (j, 0, 0)),
                  pl.BlockSpec((1, half), lambda i, j: (0, j)),
                  pl.BlockSpec((1, half), lambda i, j: (0, j))],
        out_specs=[pl.BlockSpec((bt, t_len, LANES), lambda i, j: (i, 0, j)),
                   pl.BlockSpec((1, bt, gw), lambda i, j: (i, 0, j))],
        out_shape=[jax.ShapeDtypeStruct((b, t_len, D_SSM), F32),
                   jax.ShapeDtypeStruct((nbc, bt, nx * gw), F32)],
        scratch_shapes=[pltpu.VMEM((rows, lk), BF16), pltpu.VMEM((gw // LANES, rows, LANES), F32)],
        compiler_params=_cparams(("parallel", "parallel")),
        name="s5_scan_nat",
    )(u, h0, v8, m8, w8, ar, ai)
    ht = ht.reshape(b, nx, 2, S5_LANE_GROUPS, S5_STATE)
    return (y, ht[:, :, 0].reshape(b, N_SSM_GROUPS, S5_STATE), ht[:, :, 1].reshape(b, N_SSM_GROUPS, S5_STATE))


def _epilogue_kernel(x_ref, o_ref, y_ref, u_ref, wz_ref, d_ref, glu0_ref, glu1_ref, gb0_ref, gb1_ref,
                     wa_ref, ws_ref, wo_ref, lng_ref, lnb_ref, out_ref, *, alpha):
    x = x_ref[...]
    z = _dot(x.astype(BF16), wz_ref[...])
    za = z[:, :D_ATT]
    zs = z[:, D_ATT:D_ATT + D_SSM]
    ga = z[:, D_ATT + D_SSM:D_ATT + D_SSM + D_MODEL]
    gs = z[:, D_ATT + D_SSM + D_MODEL:]
    p_att = _dot((o_ref[...] * jax.nn.silu(za)).astype(BF16), wa_ref[...])
    y = jax.nn.gelu(y_ref[...] + d_ref[...] * u_ref[...]).astype(BF16)
    s = (_dot(y, glu0_ref[...]) + gb0_ref[...]) * jax.nn.sigmoid(_dot(y, glu1_ref[...]) + gb1_ref[...])
    p_ssm = _dot((s * jax.nn.silu(zs)).astype(BF16), ws_ref[...])
    merged = jax.nn.sigmoid(ga) * p_att + jax.nn.sigmoid(gs) * p_ssm
    r = alpha * x + _dot(merged.astype(BF16), wo_ref[...])
    mu = jnp.mean(r, axis=-1, keepdims=True)
    c = r - mu
    var = jnp.mean(c * c, axis=-1, keepdims=True)
    out_ref[...] = c * lax.rsqrt(var + LN_EPS) * lng_ref[...] + lnb_ref[...]


def _epilogue(x2d, o2d, y2d, u2d, ew, alpha):
    m = x2d.shape[0]
    tm = min(256, m)
    names = ("wz", "d", "glu0", "glu1", "gb0", "gb1", "wa", "ws", "wo", "lng", "lnb")
    ws = [ew[n] for n in names]

    def tok(w):
        return pl.BlockSpec((tm, w), lambda i: (i, 0))

    return pl.pallas_call(
        functools.partial(_epilogue_kernel, alpha=alpha),
        grid=(m // tm,),
        in_specs=[tok(D_MODEL), tok(D_ATT), tok(D_SSM), tok(D_SSM)]
        + [pl.BlockSpec(w.shape, lambda i: (0, 0)) for w in ws],
        out_specs=tok(D_MODEL),
        out_shape=jax.ShapeDtypeStruct((m, D_MODEL), F32),
        compiler_params=_cparams(("parallel",)),
        name="epilogue",
    )(x2d, o2d, y2d, u2d, *ws)


def _layer_weights(l, w_in, cmp_w1, cmp_b1, cmp_w2, cmp_b2, cmp_pos, ssm_d, ssm_glu_w, ssm_glu_b,
                   w_att_out, w_ssm_out, w_o, ln_g, ln_b):
    offs = np.concatenate([[0], np.cumsum(IN_WIDTHS)])
    col = lambda i: w_in[l][:, offs[i]:offs[i + 1]]
    perm = _pair_perm()
    gate = jnp.pad(col(7), ((0, 0), (0, GATE_PAD - 3 * N_HEADS)))
    w_seq = jnp.concatenate([col(0)[:, perm], col(1), col(2), col(3), col(4), col(5), col(6), gate, col(9)],
                            axis=1).astype(BF16)
    wz = jnp.concatenate([col(8)[:, perm], col(10), col(11)], axis=1).astype(BF16)

    half = CMP_BLOCK // 2
    eye = jnp.eye(4, dtype=F32)
    eye_g = jnp.eye(N_KV, dtype=F32)
    w1, pos = [], []
    for kv in range(2):
        bd = jnp.einsum('pdh,st->psdth', cmp_w1[l, kv], eye_g)
        bd = bd.reshape(CMP_BLOCK, KV_W, N_KV * CMP_HIDDEN)
        w1.append(jnp.concatenate([bd[:half].reshape(half * KV_W, -1), bd[half:].reshape(half * KV_W, -1)], axis=1))
        pos_g = jnp.broadcast_to(cmp_pos[l, kv][:, None, :], (CMP_BLOCK, N_KV, HEAD_DIM))
        pos.append(jnp.zeros((SUBLANES, half * KV_W), F32)
                   .at[0].set(pos_g[:half].reshape(-1)).at[1].set(pos_g[half:].reshape(-1)))
    w1 = jnp.stack(w1).astype(BF16)
    pos = jnp.stack(pos)
    b1 = jnp.concatenate([cmp_b1[l, 0], cmp_b1[l, 0], cmp_b1[l, 1], cmp_b1[l, 1]]).reshape(1, -1)
    w2s = jnp.stack([cmp_w2[l, 0], cmp_w2[l, 0], cmp_w2[l, 1], cmp_w2[l, 1]])
    w2 = jnp.einsum('shd,st->shtd', w2s, eye).reshape(4 * CMP_HIDDEN, 4 * HEAD_DIM).astype(BF16)
    b2 = jnp.concatenate([cmp_b2[l, 0], cmp_b2[l, 0], cmp_b2[l, 1], cmp_b2[l, 1]]).reshape(1, -1)
    cw = dict(w1=w1, pos=pos, b1=b1, w2=w2, b2=b2)

    ew = dict(wz=wz, d=ssm_d[l].reshape(1, D_SSM),
              glu0=ssm_glu_w[l, 0].astype(BF16), glu1=ssm_glu_w[l, 1].astype(BF16),
              gb0=ssm_glu_b[l, 0].reshape(1, -1), gb1=ssm_glu_b[l, 1].reshape(1, -1),
              wa=w_att_out[l][perm].astype(BF16), ws=w_ssm_out[l].astype(BF16), wo=w_o[l].astype(BF16),
              lng=ln_g[l].reshape(1, -1), lnb=ln_b[l].reshape(1, -1))
    return w_seq, cw, ew


def _prompt_layer(h_p, lw, s5m, tables, alpha):
    b, seq, _ = h_p.shape
    w_seq, cw, ew = lw
    x2d = h_p.reshape(b * seq, D_MODEL)
    q, kvc, kvs, kvw, kvct, kvst, kvwt, gates, u = _inproj(x2d, w_seq, seq)
    r3 = lambda a: a.reshape(b, seq, a.shape[-1])
    o_att = _nsa_prompt(r3(q), r3(kvc), r3(kvs), r3(kvw), kvst, kvwt, r3(gates), cw, tables)
    zero = jnp.zeros((b, N_SSM_GROUPS, S5_STATE), F32)
    y, hr, hi = _s5_nat(r3(u), zero, zero, s5m, S5_CHUNK, bt=min(S5_BATCH_ROWS, b))
    out = _epilogue(x2d, o_att.reshape(b * seq, D_ATT), y.reshape(b * seq, D_SSM), u, ew, alpha)
    return out.reshape(b, seq, D_MODEL), kvct, kvst, kvwt, hr, hi


def _gather_step(page_dma):
    b, c = pl.program_id(0), pl.program_id(1)
    n_c = pl.num_programs(1)
    step = b * n_c + c
    total = pl.num_programs(0) * n_c
    slot = step % 2

    @pl.when(step == 0)
    def _first():
        for j in range(PAGES_PER_STEP):
            page_dma(b, c, j, slot).start()

    @pl.when(step + 1 < total)
    def _prefetch():
        nxt = step + 1
        for j in range(PAGES_PER_STEP):
            page_dma(nxt // n_c, nxt % n_c, j, 1 - slot).start()

    for j in range(PAGES_PER_STEP):
        page_dma(b, c, j, slot).wait()
    return slot


def _sample_queries(q):
    t = q.shape[0]
    lane = lax.broadcasted_iota(jnp.int32, (t, LANES), 1)
    parts = []
    for g in range(N_KV):
        gmask = (lane < HEAD_DIM) if g == 0 else (lane >= HEAD_DIM)
        parts += [jnp.where(gmask, q[:, r * LANES:(r + 1) * LANES], 0.0) for r in range(HPG)]
    return jnp.concatenate(parts, axis=0).astype(BF16)


def _softmax_rows(s):
    m = jnp.max(s, axis=1, keepdims=True)
    valid = m > 0.5 * NEG
    e = jnp.exp2(s - m)
    return jnp.where(valid, e / jnp.sum(e, axis=1, keepdims=True), 0.0)


def _cmp_sample_kernel(pt_ref, cache_hbm, q_ref, perm_ref, w1_ref, pos_ref, b1_ref, w2_ref, b2_ref, biasc_ref,
                       mimp_ref, oc_ref, sel_ref, buf, sem, x_scr, hf_scr, hs_scr,
                       *, layer, t_len, past_len, n_blk):
    c = pl.program_id(1)
    n_c = pl.num_programs(1)
    nh = 4 * CMP_HIDDEN
    cpp = buf.shape[3] // CMP_STRIDE
    rows = PAGES_PER_STEP * cpp

    def page_dma(b, cc, j, slot):
        page = pt_ref[b, cc * PAGES_PER_STEP + j]
        return pltpu.make_async_copy(cache_hbm.at[layer, page], buf.at[slot, j], sem.at[slot])

    slot = _gather_step(page_dma)

    def regroup(j, carry):
        xp = _dot_nt(perm_ref[...], buf[slot, j].astype(BF16))
        r0 = pl.multiple_of(j * cpp, cpp)
        for p in range(CMP_STRIDE):
            for kv in range(2):
                x_scr[kv, pl.ds(r0, cpp), p * KV_W:(p + 1) * KV_W] = xp[p * cpp:(p + 1) * cpp, kv * KV_W:(kv + 1) * KV_W]
        return carry

    lax.fori_loop(0, PAGES_PER_STEP, regroup, 0, unroll=4)
    hf, hs = _compress_hidden(x_scr[0].astype(BF16), x_scr[1].astype(BF16), w1_ref)
    r0 = pl.multiple_of(c * rows, rows)
    hf_scr[pl.ds(r0, rows), :] = hf
    hs_scr[pl.ds(r0, rows), :] = hs

    @pl.when(c == n_c - 1)
    def _finish():
        n_ch = hf_scr.shape[0]
        c1 = _compress_const(pos_ref, w1_ref, b1_ref)
        hs_next = pltpu.roll(hs_scr[...], n_ch - 1, 0)
        kv = _compress_out(hf_scr[...], hs_next, c1, w2_ref, b2_ref)
        q64 = _sample_queries(q_ref[0])
        p = _softmax_rows(_dot_nt(q64, kv[:, :KV_W].astype(BF16)) + biasc_ref[...])
        oc_ref[0] = _dot(p.astype(BF16), kv[:, KV_W:].astype(BF16))
        psum = []
        for g in range(N_KV):
            acc = p[g * HPG * t_len:g * HPG * t_len + t_len]
            for r in range(1, HPG):
                acc = acc + p[(g * HPG + r) * t_len:(g * HPG + r + 1) * t_len]
            psum.append(acc)
        psum = jnp.concatenate(psum, axis=0)
        p1, p2, p3 = _split3(psum)
        imp = _dot(p1, mimp_ref[...]) + _dot(p2, mimp_ref[...]) + _dot(p3, mimp_ref[...])
        shape = imp.shape
        bj = lax.broadcasted_iota(jnp.int32, shape, 1)
        tpos = past_len + lax.broadcasted_iota(jnp.int32, shape, 0) % t_len
        cur = tpos // SLC_BLOCK
        forced = (bj == 0) | (bj == cur) | (bj == cur - 1)
        visible = bj * SLC_BLOCK <= tpos
        score = jnp.where(forced, -NEG, jnp.where(visible, imp, NEG))
        score = jnp.where(bj < n_blk, score, 2.0 * NEG)
        sel = jnp.zeros(shape, F32)
        for _ in range(min(N_SEL, n_blk)):
            best = jnp.max(score, axis=1, keepdims=True)
            first = jnp.min(jnp.where(score == best, bj, shape[1]), axis=1, keepdims=True)
            hit = bj == first
            sel = jnp.where(hit, 1.0, sel)
            score = jnp.where(hit, 3.0 * NEG, score)
        sel_ref[0] = sel


def _slc_sample_kernel(pt_ref, cache_hbm, q_ref, sel_ref, knew_ref, bias_ref, c31_ref, biasn_ref, eblk_ref,
                       os_ref, buf, sem, m_scr, l_scr, acc_scr, *, layer):
    c = pl.program_id(1)
    n_c = pl.num_programs(1)
    page = buf.shape[2] // PAGES_PER_STEP

    def page_dma(b, cc, j, slot):
        pg = pt_ref[b, cc * PAGES_PER_STEP + j]
        return pltpu.make_async_copy(cache_hbm.at[layer, pg], buf.at[slot, :, pl.ds(j * page, page)], sem.at[slot])

    slot = _gather_step(page_dma)
    q64 = _sample_queries(q_ref[0])

    @pl.when(c == 0)
    def _init():
        m_scr[...] = jnp.full(m_scr.shape, NEG, F32)
        l_scr[...] = jnp.zeros(l_scr.shape, F32)
        acc_scr[...] = jnp.zeros(acc_scr.shape, F32)

    def update(s, vt_bf):
        m_old = m_scr[...]
        m_new = jnp.maximum(m_old, jnp.max(s, axis=1, keepdims=True))
        alpha = jnp.exp2(m_old - m_new)
        e = jnp.exp2(s - m_new)
        l_scr[...] = alpha * l_scr[...] + jnp.sum(e, axis=1, keepdims=True)
        acc_scr[...] = alpha * acc_scr[...] + _dot_nt(e.astype(BF16), vt_bf)
        m_scr[...] = m_new

    neg = (_dot(sel_ref[0, 0].astype(BF16), eblk_ref[...]) - 1.0) * (-NEG)
    s = _dot(q64, buf[slot, 0:KV_W, :].astype(BF16)) + neg

    @pl.when(c < n_c - 1)
    def _far():
        update(s + c31_ref[...], buf[slot, KV_W:KV2, :].astype(BF16))

    @pl.when(c == n_c - 1)
    def _last():
        update(s + bias_ref[...], buf[slot, KV_W:KV2, :].astype(BF16))
        kn = knew_ref[0]
        update(_dot(q64, kn[0:KV_W, :].astype(BF16)) + biasn_ref[...], kn[KV_W:KV2, :].astype(BF16))
        valid = m_scr[...] > 0.5 * NEG
        os_ref[0] = jnp.where(valid, acc_scr[...] / jnp.where(valid, l_scr[...], 1.0), 0.0)


def _win_sample_kernel(q_ref, kvw_ref, biasw_ref, g_ref, oc_ref, os_ref, o_ref, *, t_len):
    q64 = _sample_queries(q_ref[0])
    kw = kvw_ref[0]
    p = _softmax_rows(_dot(q64, kw[0:KV_W, :].astype(BF16)) + biasw_ref[...])
    ow = _dot_nt(p.astype(BF16), kw[KV_W:KV2, :].astype(BF16))
    gates = g_ref[0]
    oc, osel = oc_ref[0], os_ref[0]
    lane = lax.broadcasted_iota(jnp.int32, (t_len, LANES), 1)

    def head(g, r):
        h = g * HPG + r
        sl = slice(h * t_len, (h + 1) * t_len)
        return (gates[:, 3 * h:3 * h + 1] * oc[sl] + gates[:, 3 * h + 1:3 * h + 2] * osel[sl]
                + gates[:, 3 * h + 2:3 * h + 3] * ow[sl])

    for r in range(HPG):
        o_ref[0, :, r * LANES:(r + 1) * LANES] = jnp.where(lane < HEAD_DIM, head(0, r), head(1, r))


def _nsa_sample(layer, q, gates, kvs_new_t, all_w_t, cmp_t, slc_t, page_table, cw, tables):
    b, t_len, _ = q.shape
    n_pages = page_table.shape[1]
    page = cmp_t.shape[3]
    past_len = n_pages * page
    n_c = n_pages // PAGES_PER_STEP
    n_ch = past_len // CMP_STRIDE
    n_blk = -(-(past_len + t_len) // SLC_BLOCK)
    blk_pad = -(-n_blk // LANES) * LANES
    rows = PAGES_PER_STEP * page
    blk_step = rows // SLC_BLOCK
    nq = N_HEADS * t_len
    assert past_len % SLC_BLOCK == 0 and t_len <= SLC_BLOCK and (past_len + t_len) // CMP_STRIDE == n_ch
    biasc, biass, c31, biasn, biasw = tables
    c31 = c31[:, :1]
    mimp = jnp.asarray(_imp_matrix(n_ch, blk_pad, (past_len + t_len) // CMP_STRIDE - 1), BF16)
    cpp = page // CMP_STRIDE
    pm = np.zeros((page, page), np.float32)
    for p in range(CMP_STRIDE):
        for cc in range(cpp):
            pm[p * cpp + cc, cc * CMP_STRIDE + p] = 1.0
    pm = jnp.asarray(pm, BF16)

    def full(a):
        nd = a.ndim
        return pl.BlockSpec(a.shape, lambda i, j, pt, nd=nd: (0,) * nd)

    q_spec = pl.BlockSpec((1, t_len, D_ATT), lambda i, j, pt: (i, 0, 0))
    o_spec = pl.BlockSpec((1, nq, KV_W), lambda i, j, pt: (i, 0, 0))
    any_spec = pl.BlockSpec(memory_space=pl.ANY)
    oc, sel = pl.pallas_call(
        functools.partial(_cmp_sample_kernel, layer=layer, t_len=t_len, past_len=past_len, n_blk=n_blk),
        grid_spec=pltpu.PrefetchScalarGridSpec(
            num_scalar_prefetch=1, grid=(b, n_c),
            in_specs=[any_spec, q_spec, full(pm), full(cw["w1"]), full(cw["pos"]), full(cw["b1"]), full(cw["w2"]),
                      full(cw["b2"]), full(biasc), full(mimp)],
            out_specs=[o_spec, pl.BlockSpec((1, N_KV * t_len, blk_pad), lambda i, j, pt: (i, 0, 0))],
            scratch_shapes=[pltpu.VMEM((2, PAGES_PER_STEP, KV2, page), F32),
                            pltpu.SemaphoreType.DMA((2,)),
                            pltpu.VMEM((2, PAGES_PER_STEP * cpp, CMP_STRIDE * KV_W), F32),
                            pltpu.VMEM((n_ch, 4 * CMP_HIDDEN), F32), pltpu.VMEM((n_ch, 4 * CMP_HIDDEN), F32)]),
        out_shape=[jax.ShapeDtypeStruct((b, nq, KV_W), F32),
                   jax.ShapeDtypeStruct((b, N_KV * t_len, blk_pad), F32)],
        compiler_params=_cparams(("arbitrary", "arbitrary")),
        name="nsa_sample_cmp",
    )(page_table, cmp_t, q, pm, cw["w1"], cw["pos"], cw["b1"], cw["w2"], cw["b2"], biasc, mimp)

    sel_c = sel[:, :, :n_c * blk_step].reshape(b, N_KV, 1, t_len, n_c, blk_step)
    sel_c = jnp.broadcast_to(sel_c, (b, N_KV, HPG, t_len, n_c, blk_step))
    sel_c = sel_c.transpose(0, 4, 1, 2, 3, 5).reshape(b, n_c, nq, blk_step)
    sel_c = jnp.pad(sel_c, ((0, 0), (0, 0), (0, 0), (0, LANES - blk_step)))
    eblk = jnp.asarray(np.arange(LANES)[:, None] == (np.arange(rows)[None, :] // SLC_BLOCK), BF16)
    o_s = pl.pallas_call(
        functools.partial(_slc_sample_kernel, layer=layer),
        grid_spec=pltpu.PrefetchScalarGridSpec(
            num_scalar_prefetch=1, grid=(b, n_c),
            in_specs=[any_spec, q_spec,
                      pl.BlockSpec((1, 1, nq, LANES), lambda i, j, pt: (i, j, 0, 0)),
                      pl.BlockSpec((1, KV2, LANES), lambda i, j, pt: (i, 0, 0)),
                      full(biass), full(c31), full(biasn), full(eblk)],
            out_specs=o_spec,
            scratch_shapes=[pltpu.VMEM((2, KV2, rows), F32),
                            pltpu.SemaphoreType.DMA((2,)),
                            pltpu.VMEM((nq, 1), F32), pltpu.VMEM((nq, 1), F32), pltpu.VMEM((nq, KV_W), F32)]),
        out_shape=jax.ShapeDtypeStruct((b, nq, KV_W), F32),
        compiler_params=_cparams(("arbitrary", "arbitrary")),
        name="nsa_sample_slc",
    )(page_table, slc_t, q, sel_c, kvs_new_t, biass, c31, biasn, eblk)

    w_pad = all_w_t.shape[2]
    return pl.pallas_call(
        functools.partial(_win_sample_kernel, t_len=t_len),
        grid=(b,),
        in_specs=[pl.BlockSpec((1, t_len, D_ATT), lambda i: (i, 0, 0)),
                  pl.BlockSpec((1, KV2, w_pad), lambda i: (i, 0, 0)),
                  pl.BlockSpec(biasw.shape, lambda i: (0, 0)),
                  pl.BlockSpec((1, t_len, GATE_PAD), lambda i: (i, 0, 0)),
                  pl.BlockSpec((1, nq, KV_W), lambda i: (i, 0, 0)),
                  pl.BlockSpec((1, nq, KV_W), lambda i: (i, 0, 0))],
        out_specs=pl.BlockSpec((1, t_len, D_ATT), lambda i: (i, 0, 0)),
        out_shape=jax.ShapeDtypeStruct((b, t_len, D_ATT), F32),
        compiler_params=_cparams(("parallel",)),
        name="nsa_sample_win",
    )(q, all_w_t, biasw, gates, oc, o_s)


def _sample_layer(layer, h_s, lw, s5m, tables, alpha, cmp_t, slc_t, win_t, h0_re, h0_im, page_table, w_pad):
    b, t_len, _ = h_s.shape
    w_seq, cw, ew = lw
    m = b * t_len
    x2d = h_s.reshape(m, D_MODEL)
    q, kvc, kvs, kvw, kvct, kvst, kvwt, gates, u = _inproj(x2d, w_seq, m)
    r3 = lambda a: a.reshape(b, t_len, a.shape[-1])
    per_row = lambda a: a.reshape(KV2, b, t_len).transpose(1, 0, 2)
    w_buf = win_t.shape[2]
    all_w_t = jnp.concatenate([win_t, per_row(kvwt)], axis=2)
    all_w_pad = jnp.pad(all_w_t, ((0, 0), (0, 0), (0, w_pad - (w_buf + t_len))))
    kvs_new_t = jnp.pad(per_row(kvst), ((0, 0), (0, 0), (0, LANES - t_len)))
    o_att = _nsa_sample(layer, r3(q), r3(gates), kvs_new_t, all_w_pad, cmp_t, slc_t, page_table, cw, tables)
    y, hr, hi = _s5(r3(u), h0_re, h0_im, s5m, t_len, bt=b)
    out = _epilogue(x2d, o_att.reshape(m, D_ATT), y.reshape(m, D_SSM), u, ew, alpha)
    return out.reshape(b, t_len, D_MODEL), r3(kvc), r3(kvs), all_w_t[:, :, t_len:], hr, hi


def _feature_major(a):
    nd = a.ndim
    perm = tuple(range(nd - 4)) + (nd - 3, nd - 2, nd - 1, nd - 4)
    t = a.transpose(perm)
    return t.reshape(t.shape[:nd - 4] + (KV2, a.shape[nd - 4]))


def _row_major(a):
    nd = a.ndim
    t = a.reshape(a.shape[:nd - 2] + (2, N_KV, HEAD_DIM, a.shape[-1]))
    perm = tuple(range(nd - 2)) + (nd + 1, nd - 2, nd - 1, nd)
    return t.transpose(perm)


def kernel(x_prompt, x_sample, cache_kv_cmp, cache_kv_slc, state_win_kv, state_ssm_re, state_ssm_im,
           page_table, rel_bias, w_in, cmp_w1, cmp_b1, cmp_w2, cmp_b2, cmp_pos,
           ssm_a_re, ssm_a_im, ssm_log_dt, ssm_b_re, ssm_b_im, ssm_c_re, ssm_c_im, ssm_d,
           ssm_glu_w, ssm_glu_b, w_att_out, w_ssm_out, w_o, ln_g, ln_b):
    depth = w_in.shape[0]
    alpha = (2 * depth) ** 0.25
    bp, seq = x_prompt.shape[:2]
    bs, t_len = x_sample.shape[:2]
    page = cache_kv_cmp.shape[2]
    past_len = page_table.shape[1] * page
    w_buf = state_win_kv.shape[2]
    w_pad = -(-(w_buf + t_len) // LANES) * LANES
    n_win = min(WINDOW, seq)
    p_tables = _prompt_tables(rel_bias, seq)
    s_tables = _sample_tables(rel_bias, past_len, t_len, w_buf, w_pad, PAGES_PER_STEP * page)
    cmp_t, slc_t, win_t = _feature_major(cache_kv_cmp), _feature_major(cache_kv_slc), _feature_major(state_win_kv)
    h_p, h_s = x_prompt, x_sample
    outs_p, outs_s = [], []
    for l in range(depth):
        lw = _layer_weights(l, w_in, cmp_w1, cmp_b1, cmp_w2, cmp_b2, cmp_pos, ssm_d, ssm_glu_w, ssm_glu_b,
                            w_att_out, w_ssm_out, w_o, ln_g, ln_b)
        ssm = (ssm_a_re[l], ssm_a_im[l], ssm_log_dt[l], ssm_b_re[l], ssm_b_im[l], ssm_c_re[l], ssm_c_im[l])
        h_p, kvct, kvst, kvwt, hr, hi = _prompt_layer(h_p, lw, _s5_mats(*ssm, S5_CHUNK), p_tables, alpha)
        outs_p.append((kvct, kvst, kvwt[:, :, seq - n_win:], hr, hi))
        h_s, kvc, kvs, win, hr, hi = _sample_layer(
            l, h_s, lw, _s5_mats(*ssm, t_len), s_tables, alpha, cmp_t, slc_t, win_t[l],
            state_ssm_re[l], state_ssm_im[l], page_table, w_pad)
        outs_s.append((kvc.reshape(bs, t_len, 2, N_KV, HEAD_DIM), kvs.reshape(bs, t_len, 2, N_KV, HEAD_DIM),
                       win, hr, hi))
    stack = lambda outs, i: jnp.stack([o[i] for o in outs])
    return ((h_p, h_s)
            + tuple(_row_major(stack(outs_p, i)) for i in range(3)) + (stack(outs_p, 3), stack(outs_p, 4))
            + (stack(outs_s, 0), stack(outs_s, 1), _row_major(stack(outs_s, 2)), stack(outs_s, 3), stack(outs_s, 4)))
```

```python
import functools
import math

import numpy as np
import jax
import jax.numpy as jnp
from jax import lax
from jax.experimental import pallas as pl
from jax.experimental.pallas import tpu as pltpu

F32 = jnp.float32
BF16 = jnp.bfloat16

D_MODEL = 1024
N_HEADS = 8
HEAD_DIM = 64
N_KV = 2
HPG = N_HEADS // N_KV
D_ATT = N_HEADS * HEAD_DIM
KV_W = N_KV * HEAD_DIM
CMP_BLOCK = 32
CMP_STRIDE = 16
CMP_HIDDEN = 128
SLC_BLOCK = 64
N_SEL = 16
WINDOW = 512
S5_GROUP = 16
S5_STATE = 64
D_SSM = 512
N_SSM_GROUPS = D_SSM // S5_GROUP
N_BUCKETS = 32
MAX_DISTANCE = 128
LN_EPS = 1e-5
IN_WIDTHS = (D_ATT, KV_W, KV_W, KV_W, KV_W, KV_W, KV_W, 3 * N_HEADS, D_ATT, D_SSM, D_SSM, 2 * D_MODEL)

LANES = 128
SUBLANES = 8
VMEM_LIMIT = 56 * 1024 * 1024
NEG = -1e30
LOG2E = 1.4426950408889634
VROWS = HEAD_DIM + 16
TQ = 128
TK = 128
FAR_TILES = 4
INPROJ_ROWS = 1024
EPILOGUE_ROWS = 512
KV2 = 2 * KV_W
GATE_PAD = LANES
S5_CHUNK = 8
S5_LANE_GROUPS = LANES // S5_GROUP
S5_BATCH_ROWS = 4
SEQ_COLS = D_ATT + 3 * KV2 + GATE_PAD + D_SSM
QCOLS = N_HEADS * TQ
GCOLS = HPG * TQ
PAGES_PER_STEP = 32


def _cparams(sem):
    return pltpu.CompilerParams(dimension_semantics=sem, vmem_limit_bytes=VMEM_LIMIT)


def _dot(a, b):
    return jnp.dot(a, b, preferred_element_type=F32)


def _dot_nt(a, b):
    return lax.dot_general(a, b, (((1,), (1,)), ((), ())), preferred_element_type=F32)


def _split3(x):
    h1 = x.astype(BF16)
    r1 = x - h1.astype(F32)
    h2 = r1.astype(BF16)
    h3 = (r1 - h2.astype(F32)).astype(BF16)
    return h1, h2, h3


def _bucket_np(dist):
    n = np.maximum(dist, 0)
    max_exact = N_BUCKETS // 2
    nf = np.maximum(n, 1).astype(np.float32)
    val = (np.log(nf / np.float32(max_exact)) / np.float32(math.log(MAX_DISTANCE / max_exact))
           * np.float32(N_BUCKETS - max_exact))
    large = max_exact + val.astype(np.int32)
    return np.where(n < max_exact, n, np.minimum(large, N_BUCKETS - 1)).astype(np.int32)


def _pair_perm():
    cols = []
    for r in range(HPG):
        cols += list(range(r * HEAD_DIM, (r + 1) * HEAD_DIM))
        cols += list(range((HPG + r) * HEAD_DIM, (HPG + r + 1) * HEAD_DIM))
    return np.asarray(cols, np.int32)


def _imp_matrix(n_cmp_pad, n_blk_pad, n_cmp):
    ratio = SLC_BLOCK // CMP_STRIDE
    n = np.arange(n_cmp_pad)[:, None]
    j = np.arange(n_blk_pad)[None, :]
    m = ((n >= ratio * j - 1) & (n <= ratio * j + ratio - 2)).astype(np.float32)
    m += ((n >= ratio * j) & (n <= ratio * j + ratio - 1)).astype(np.float32)
    m *= (n < n_cmp)
    return m


def _bucket_starts():
    b = _bucket_np(np.arange(2 * MAX_DISTANCE))
    assert (np.diff(b) >= 0).all() and b.max() == N_BUCKETS - 1
    starts = [int(np.argmax(b >= k)) for k in range(1, N_BUCKETS)]
    assert all(b[s] == k for k, s in zip(range(1, N_BUCKETS), starts))
    return starts


def _bias_of_dist(d, bucket_value):
    val = bucket_value(0)
    for k, start in zip(range(1, N_BUCKETS), _bucket_starts()):
        val = jnp.where(d >= start, bucket_value(k), val)
    return val


def _prompt_tables_kernel(rb_ref, tn_ref, t4_ref, cm_ref, *, row_shift):
    bias = lambda d: _bias_of_dist(d, lambda k: rb_ref[k:k + 1, :])
    last = rb_ref[N_BUCKETS - 1:N_BUCKETS, :]
    shape = (TK, QCOLS)
    d0 = lax.broadcasted_iota(jnp.int32, shape, 1) % TQ - lax.broadcasted_iota(jnp.int32, shape, 0)
    tn_ref[0:TK, :] = (bias(TK + d0) - last) * LOG2E
    tn_ref[TK:2 * TK, :] = jnp.where(d0 >= 0, (bias(d0) - last) * LOG2E, NEG)
    t4_ref[...] = jnp.where(d0 <= 0, 0.0, NEG)
    shape = cm_ref.shape
    nn = lax.broadcasted_iota(jnp.int32, shape, 0) - row_shift
    dc = lax.broadcasted_iota(jnp.int32, shape, 1) % TQ - (nn * CMP_STRIDE + CMP_BLOCK - 1)
    cm_ref[...] = jnp.where(dc >= 0, bias(dc) * LOG2E, NEG)


def _prompt_tables(rel_bias, seq):
    assert WINDOW == 4 * TK and TQ == TK and _bucket_starts()[-1] <= TK
    n_qt = seq // TQ
    n_ch = seq // CMP_STRIDE
    row_shift = (n_qt - 1) * (TQ // CMP_STRIDE)
    tile = jax.ShapeDtypeStruct((TK, QCOLS), F32)
    return pl.pallas_call(
        functools.partial(_prompt_tables_kernel, row_shift=row_shift),
        out_shape=[jax.ShapeDtypeStruct((2 * TK, QCOLS), F32), tile,
                   jax.ShapeDtypeStruct((n_ch + row_shift, QCOLS), F32)],
        compiler_params=pltpu.CompilerParams(vmem_limit_bytes=VMEM_LIMIT),
        name="prompt_tables",
    )(jnp.repeat(rel_bias, TQ, axis=1))


def _sample_tables_kernel(rb_ref, bc_ref, bs_ref, c31_ref, bn_ref, bw_ref, *, past_len, t_len, w_buf):
    bias = lambda d: _bias_of_dist(d, lambda k: rb_ref[:, k:k + 1]) * LOG2E
    n_tok = past_len + t_len

    def grid(ref):
        qpos = past_len + lax.broadcasted_iota(jnp.int32, ref.shape, 0) % t_len
        return qpos, lax.broadcasted_iota(jnp.int32, ref.shape, 1)

    qpos, nn = grid(bc_ref)
    dc = qpos - (nn * CMP_STRIDE + CMP_BLOCK - 1)
    bc_ref[...] = jnp.where((dc >= 0) & (nn < n_tok // CMP_STRIDE - 1), bias(dc), NEG)
    qpos, ll = grid(bs_ref)
    bs_ref[...] = bias(qpos - (past_len - bs_ref.shape[1] + ll))
    c31_ref[...] = jnp.broadcast_to(rb_ref[:, N_BUCKETS - 1:N_BUCKETS] * LOG2E, c31_ref.shape)
    qpos, ll = grid(bn_ref)
    dn = qpos - (past_len + ll)
    bn_ref[...] = jnp.where((dn >= 0) & (ll < t_len), bias(dn), NEG)
    qpos, ii = grid(bw_ref)
    w_pos = n_tok - (w_buf + t_len) + ii
    dw = qpos - w_pos
    bw_ref[...] = jnp.where((dw >= 0) & (dw <= WINDOW) & (w_pos >= 0) & (ii < w_buf + t_len), bias(dw), NEG)


def _sample_tables(rel_bias, past_len, t_len, w_buf, w_pad, last_rows):
    assert _bucket_starts()[-1] <= last_rows
    nq = N_HEADS * t_len
    shp = lambda n: jax.ShapeDtypeStruct((nq, n), F32)
    return pl.pallas_call(
        functools.partial(_sample_tables_kernel, past_len=past_len, t_len=t_len, w_buf=w_buf),
        out_shape=[shp(past_len // CMP_STRIDE), shp(last_rows), shp(LANES), shp(LANES), shp(w_pad)],
        compiler_params=pltpu.CompilerParams(vmem_limit_bytes=VMEM_LIMIT),
        name="sample_tables",
    )(jnp.repeat(rel_bias.T, t_len, axis=0))


def _inproj_kernel(x_ref, w_ref, q_ref, kvc_ref, kvs_ref, kvw_ref, kvct_ref, kvst_ref, kvwt_ref, g_ref, u_ref):
    h = _dot(x_ref[...].astype(BF16), w_ref[...])
    o = 0
    q_ref[...] = h[:, o:o + D_ATT] * (HEAD_DIM ** -0.5 * LOG2E)
    o += D_ATT
    for row_ref, t_ref in ((kvc_ref, kvct_ref), (kvs_ref, kvst_ref), (kvw_ref, kvwt_ref)):
        kv = h[:, o:o + KV2]
        row_ref[...] = kv
        t_ref[0] = kv.T
        o += KV2
    g_ref[...] = jax.nn.sigmoid(h[:, o:o + GATE_PAD])
    o += GATE_PAD
    u_ref[...] = h[:, o:o + D_SSM]


def _inproj(x2d, w_seq, rows_per_seq):
    m = x2d.shape[0]
    tm = min(INPROJ_ROWS, m)
    tps = rows_per_seq // tm
    tok = lambda w: pl.BlockSpec((tm, w), lambda i: (i, 0))
    ft = pl.BlockSpec((1, KV2, tm), lambda i: (i // tps, 0, i % tps))
    tok_shape = lambda w: jax.ShapeDtypeStruct((m, w), F32)
    ft_shape = jax.ShapeDtypeStruct((m // rows_per_seq, KV2, rows_per_seq), F32)
    return pl.pallas_call(
        _inproj_kernel,
        grid=(m // tm,),
        in_specs=[tok(D_MODEL), pl.BlockSpec((D_MODEL, SEQ_COLS), lambda i: (0, 0))],
        out_specs=[tok(D_ATT), tok(KV2), tok(KV2), tok(KV2), ft, ft, ft, tok(GATE_PAD), tok(D_SSM)],
        out_shape=[tok_shape(D_ATT), tok_shape(KV2), tok_shape(KV2), tok_shape(KV2), ft_shape, ft_shape, ft_shape,
                   tok_shape(GATE_PAD), tok_shape(D_SSM)],
        compiler_params=_cparams(("parallel",)),
        name="inproj",
    )(x2d, w_seq)


def _compress_out(hf, hs_next, c1, w2_ref, b2_ref):
    h = hf + hs_next + c1
    return _dot(jax.nn.silu(h).astype(BF16), w2_ref[...]) + b2_ref[...]


def _compress_hidden(xk, xv, w1_ref):
    hk = _dot(xk, w1_ref[0])
    hv = _dot(xv, w1_ref[1])
    nh = N_KV * CMP_HIDDEN
    return (jnp.concatenate([hk[:, :nh], hv[:, :nh]], axis=1), jnp.concatenate([hk[:, nh:], hv[:, nh:]], axis=1))


def _compress_const(pos_ref, w1_ref, b1_ref):
    hf, hs = _compress_hidden(pos_ref[0].astype(BF16), pos_ref[1].astype(BF16), w1_ref)
    return hf[0:1] + hs[1:2] + b1_ref[...]


def _value_rows(v_scr, g, vt):
    n = v_scr.shape[2]
    pad = lax.broadcasted_iota(jnp.int32, (VROWS - HEAD_DIM, n), 0) == 0
    v_scr[g, 0:HEAD_DIM, :] = vt.astype(BF16)
    v_scr[g, HEAD_DIM:VROWS, :] = jnp.where(pad, 1.0, 0.0).astype(BF16)


def _flash_init(m_ref, acc_ref):
    m_ref[...] = jnp.full(m_ref.shape, NEG, F32)
    acc_ref[...] = jnp.zeros(acc_ref.shape, F32)


def _flash_block(k_scr, v_scr, qb_scr, k0, tk, head_ref, tail_ref, neg_scr, m_scr, acc_scr):
    st = _dot_nt(k_scr[pl.ds(k0, tk), :], qb_scr[...])
    if tail_ref is not None:
        nt = min(tail_ref.shape[0], tk)
        tail = st[tk - nt:] + tail_ref[tail_ref.shape[0] - nt:, :]
        st = tail if nt == tk else jnp.concatenate([st[:tk - nt], tail], axis=0)
    if head_ref is not None:
        nh = head_ref.shape[0]
        st = jnp.concatenate([st[:nh] + head_ref[...], st[nh:]], axis=0)
    if neg_scr is not None:
        ng = neg_scr[pl.ds(k0, tk), :]
        st = st + jnp.concatenate([ng[:, :TQ]] * HPG + [ng[:, TQ:]] * HPG, axis=1)
    m_old = m_scr[...]
    m_new = jnp.maximum(m_old, jnp.max(st, axis=0, keepdims=True))
    e = jnp.exp2(st - m_new).astype(BF16)
    pv = jnp.concatenate([_dot(v_scr[g, :, pl.ds(k0, tk)], e[:, g * GCOLS:(g + 1) * GCOLS])
                          for g in range(N_KV)], axis=1)
    acc_scr[...] = jnp.exp2(m_old - m_new) * acc_scr[...] + pv
    m_scr[...] = m_new


def _flash_add(tot_scr, m_scr, acc_scr, gate):
    valid = m_scr[...] > 0.5 * NEG
    coef = jnp.where(valid, 1.0 / jnp.where(valid, acc_scr[HEAD_DIM:HEAD_DIM + 1, :], 1.0), 0.0)
    tot_scr[...] = tot_scr[...] + acc_scr[0:HEAD_DIM, :] * (coef * gate)


def _nsa_prompt_kernel(q_ref, kck_ref, kcv_ref, ks_ref, kw_ref, vst_ref, vwt_ref, g_ref,
                       w1_ref, pos_ref, b1_ref, w2_ref, b2_ref,
                       cmaster_ref, tn_ref, t4_ref, mimp_ref, eblk_ref,
                       o_ref,
                       kc_scr, vc_scr, ks_scr, vs_scr, kw_scr, vw_scr, qb_scr, neg_scr, m_scr, acc_scr, tot_scr):
    qt = pl.program_id(1)
    n_qt = pl.num_programs(1)
    seq = ks_ref.shape[1]
    n_ch = seq // CMP_STRIDE
    n_blk = seq // SLC_BLOCK
    nh = 4 * CMP_HIDDEN

    @pl.when(qt == 0)
    def _per_batch():
        flat = lambda ref: jnp.concatenate(
            [ref[0, pl.ds(p, n_ch, stride=CMP_STRIDE), :] for p in range(CMP_STRIDE)], axis=1).astype(BF16)
        hf, hs = _compress_hidden(flat(kck_ref), flat(kcv_ref), w1_ref)
        c1 = _compress_const(pos_ref, w1_ref, b1_ref)
        hs_next = pltpu.roll(hs, n_ch - 1, 0)
        kv = _compress_out(hf, hs_next, c1, w2_ref, b2_ref)
        kc_scr[...] = kv[:, :KV_W].astype(BF16)
        vct = kv[:, KV_W:].T
        ks_scr[...] = ks_ref[0].astype(BF16)
        kw_scr[...] = kw_ref[0].astype(BF16)
        for g in range(N_KV):
            rows = slice(g * HEAD_DIM, (g + 1) * HEAD_DIM)
            _value_rows(vc_scr, g, vct[rows, :])
            _value_rows(vs_scr, g, vst_ref[0, rows, :])
            _value_rows(vw_scr, g, vwt_ref[0, rows, :])

    q = q_ref[0]
    gt = g_ref[0].T
    lane = lax.broadcasted_iota(jnp.int32, (TQ, LANES), 1)
    for h in range(N_HEADS):
        g, r = divmod(h, HPG)
        gmask = (lane < HEAD_DIM) if g == 0 else (lane >= HEAD_DIM)
        qb_scr[h * TQ:(h + 1) * TQ, :] = jnp.where(gmask, q[:, r * LANES:(r + 1) * LANES], 0.0).astype(BF16)
    q0 = qt * TQ

    def gate_cols(j):
        return jnp.concatenate([gt[3 * h + j:3 * h + j + 1, :] for h in range(N_HEADS)], axis=1)

    r0 = pl.multiple_of((n_qt - 1 - qt) * (TQ // CMP_STRIDE), SUBLANES)
    sc = _dot_nt(kc_scr[...], qb_scr[...]) + cmaster_ref[pl.ds(r0, n_ch), :]
    m = jnp.max(sc, axis=0, keepdims=True)
    e = jnp.exp2(sc - m)
    pt = jnp.where(m > 0.5 * NEG, e / jnp.sum(e, axis=0, keepdims=True), 0.0)
    pb = pt.astype(BF16)
    oc = jnp.concatenate([_dot(vc_scr[g, 0:HEAD_DIM, :], pb[:, g * GCOLS:(g + 1) * GCOLS]) for g in range(N_KV)],
                         axis=1)
    tot_scr[...] = oc * gate_cols(0)

    @pl.when(qt == 0)
    def _no_mask():
        neg_scr[...] = jnp.zeros(neg_scr.shape, F32)

    @pl.when((qt + 1) * TQ > N_SEL * SLC_BLOCK)
    def _select():
        psum = []
        for g in range(N_KV):
            acc = pt[:, g * GCOLS:g * GCOLS + TQ]
            for r in range(1, HPG):
                acc = acc + pt[:, g * GCOLS + r * TQ:g * GCOLS + (r + 1) * TQ]
            psum.append(acc)
        p1, p2, p3 = _split3(jnp.concatenate(psum, axis=1))
        imp = _dot(mimp_ref[...], p1) + _dot(mimp_ref[...], p2) + _dot(mimp_ref[...], p3)
        shape = (n_blk, N_KV * TQ)
        bj = lax.broadcasted_iota(jnp.int32, shape, 0)
        tpos = q0 + lax.broadcasted_iota(jnp.int32, shape, 1) % TQ
        cur = tpos // SLC_BLOCK
        forced = (bj == 0) | (bj == cur) | (bj == cur - 1)
        visible = bj * SLC_BLOCK <= tpos
        score = jnp.where(forced, -NEG, jnp.where(visible, imp, NEG))
        rank = jnp.zeros(shape, F32)
        for i in range(n_blk):
            si = score[i:i + 1, :]
            ahead = (si > score) | ((si == score) & (bj > i))
            rank = rank + jnp.where(ahead, 1.0, 0.0)
        unsel = jnp.where(rank < float(N_SEL), 0.0, 1.0).astype(BF16)
        neg_scr[...] = _dot(eblk_ref[...], unsel)

    _flash_init(m_scr, acc_scr)

    def sel_block(k0, n_tiles, tail_ref):
        _flash_block(ks_scr, vs_scr, qb_scr, k0, n_tiles * TK, None, tail_ref, neg_scr, m_scr, acc_scr)

    n_far = jnp.maximum(qt - 1, 0)

    def far_body(j, carry):
        sel_block(pl.multiple_of(j * FAR_TILES * TK, FAR_TILES * TK), FAR_TILES, None)
        return carry

    lax.fori_loop(0, n_far // FAR_TILES, far_body, 0)
    for rem in range(FAR_TILES):
        @pl.when((qt >= 1) & (n_far % FAR_TILES == rem))
        def _sel_last(rem=rem):
            sel_block(pl.multiple_of((qt - 1 - rem) * TK, TK), rem + 2, tn_ref)

    @pl.when(qt == 0)
    def _sel_first():
        sel_block(0, 1, tn_ref)

    _flash_add(tot_scr, m_scr, acc_scr, gate_cols(1))

    _flash_init(m_scr, acc_scr)
    n_win = WINDOW // TK
    for n_prev in range(n_win + 1):
        cond = (qt == n_prev) if n_prev < n_win else (qt >= n_win)

        @pl.when(cond)
        def _win(n_prev=n_prev):
            _flash_block(kw_scr, vw_scr, qb_scr, pl.multiple_of((qt - n_prev) * TK, TK), (n_prev + 1) * TK,
                         t4_ref if n_prev == n_win else None, tn_ref, None, m_scr, acc_scr)

    _flash_add(tot_scr, m_scr, acc_scr, gate_cols(2))
    tot = tot_scr[...]
    for r in range(HPG):
        blk = jnp.concatenate([tot[:, r * TQ:(r + 1) * TQ], tot[:, GCOLS + r * TQ:GCOLS + (r + 1) * TQ]], axis=0)
        o_ref[0, :, r * LANES:(r + 1) * LANES] = blk.T


def _nsa_prompt(q, kvc, kvs, kvw, kvst, kvwt, gates, cw, tables):
    b, seq, _ = q.shape
    n_qt = seq // TQ
    n_ch = seq // CMP_STRIDE
    n_blk = seq // SLC_BLOCK
    tn, t4, cmaster = tables
    mimp = jnp.asarray(_imp_matrix(n_ch, n_blk, n_ch - 1).T, BF16)
    in_blk = np.arange(seq)[:, None] // SLC_BLOCK == np.arange(n_blk)[None, :]
    eblk = jnp.asarray(np.where(in_blk, NEG, 0.0), BF16)

    def full(a):
        nd = a.ndim
        return pl.BlockSpec(a.shape, lambda i, j, nd=nd: (0,) * nd)

    k_spec = pl.BlockSpec((1, seq, KV_W), lambda i, j: (i, 0, 0))
    vt_spec = pl.BlockSpec((1, KV_W, seq), lambda i, j: (i, 1, 0))
    return pl.pallas_call(
        _nsa_prompt_kernel,
        grid=(b, n_qt),
        in_specs=[pl.BlockSpec((1, TQ, D_ATT), lambda i, j: (i, j, 0)),
                  k_spec, pl.BlockSpec((1, seq, KV_W), lambda i, j: (i, 0, 1)),
                  k_spec, k_spec, vt_spec, vt_spec,
                  pl.BlockSpec((1, TQ, GATE_PAD), lambda i, j: (i, j, 0)),
                  full(cw["w1"]), full(cw["pos"]), full(cw["b1"]), full(cw["w2"]), full(cw["b2"]),
                  full(cmaster), full(tn), full(t4), full(mimp), full(eblk)],
        out_specs=pl.BlockSpec((1, TQ, D_ATT), lambda i, j: (i, j, 0)),
        out_shape=jax.ShapeDtypeStruct((b, seq, D_ATT), F32),
        scratch_shapes=[pltpu.VMEM((n_ch, KV_W), BF16), pltpu.VMEM((N_KV, VROWS, n_ch), BF16),
                        pltpu.VMEM((seq, KV_W), BF16), pltpu.VMEM((N_KV, VROWS, seq), BF16),
                        pltpu.VMEM((seq, KV_W), BF16), pltpu.VMEM((N_KV, VROWS, seq), BF16),
                        pltpu.VMEM((QCOLS, LANES), BF16),
                        pltpu.VMEM((seq, N_KV * TQ), F32),
                        pltpu.VMEM((1, QCOLS), F32), pltpu.VMEM((VROWS, QCOLS), F32),
                        pltpu.VMEM((HEAD_DIM, QCOLS), F32)],
        compiler_params=_cparams(("arbitrary", "arbitrary")),
        name="nsa_prompt",
    )(q, kvc, kvc, kvs, kvw, kvst, kvwt, gates, cw["w1"], cw["pos"], cw["b1"], cw["w2"], cw["b2"],
      cmaster, tn, t4, mimp, eblk)


def _s5_kernel(u_ref, h0_ref, v_ref, m_ref, wh_ref, a1_ref, a2_ref, y_ref, ht_ref, z_scr, *, n_chunks, bt):
    gc = u_ref.shape[1]
    sw = 2 * S5_STATE
    for g in range(gc):
        z_scr[:, g * sw:(g + 1) * sw] = _dot(u_ref[0, g], v_ref[g])
    a1 = a1_ref[...]
    a2 = a2_ref[...]

    def step(k, h):
        r0 = pl.multiple_of(k * bt, bt)
        z = z_scr[pl.ds(r0, bt), :]
        z_scr[pl.ds(r0, bt), :] = h
        hsw = jnp.concatenate(
            [pltpu.roll(h[:, g * sw:(g + 1) * sw], S5_STATE, 1) for g in range(gc)], axis=1)
        return a1 * h + a2 * hsw + z

    ht_ref[0] = lax.fori_loop(0, n_chunks, step, h0_ref[0])
    for g in range(gc):
        y_ref[0, g] = (_dot(u_ref[0, g], m_ref[g])
                       + _dot(z_scr[:, g * sw:(g + 1) * sw].astype(BF16), wh_ref[g]))


def _s5_mats(a_re, a_im, log_dt, b_re, b_im, c_re, c_im, chunk):
    hp = lax.Precision.HIGHEST
    dt = jnp.exp(log_dt)[:, None]
    mag = jnp.exp(dt * a_re)
    ab_re, ab_im = mag * jnp.cos(dt * a_im), mag * jnp.sin(dt * a_im)
    den = a_re * a_re + a_im * a_im
    x_re, x_im = ab_re - 1.0, ab_im
    k_re = (x_re * a_re + x_im * a_im) / den
    k_im = (x_im * a_re - x_re * a_im) / den
    bb_re = k_re[..., None] * b_re - k_im[..., None] * b_im
    bb_im = k_re[..., None] * b_im + k_im[..., None] * b_re
    pw_re, pw_im = [jnp.ones_like(ab_re)], [jnp.zeros_like(ab_im)]
    for _ in range(chunk):
        pr, pi = pw_re[-1], pw_im[-1]
        pw_re.append(pr * ab_re - pi * ab_im)
        pw_im.append(pr * ab_im + pi * ab_re)
    pw_re, pw_im = jnp.stack(pw_re), jnp.stack(pw_im)
    cl_re = c_re[None] * pw_re[:, :, None, :] - c_im[None] * pw_im[:, :, None, :]
    cl_im = c_re[None] * pw_im[:, :, None, :] + c_im[None] * pw_re[:, :, None, :]
    t = (jnp.einsum('tgcp,gpd->tgcd', cl_re, bb_re, precision=hp)
         - jnp.einsum('tgcp,gpd->tgcd', cl_im, bb_im, precision=hp))
    s_i = np.arange(chunk)[:, None]
    t_i = np.arange(chunk)[None, :]
    lag = np.clip(t_i - s_i, 0, None)
    blk = jnp.where((t_i >= s_i)[:, :, None, None, None], t[lag], 0.0)
    ng = a_re.shape[0]
    m = blk.transpose(2, 0, 4, 1, 3).reshape(ng, chunk * S5_GROUP, chunk * S5_GROUP)
    rev = chunk - 1 - np.arange(chunk)
    vb_re = pw_re[rev][:, :, :, None] * bb_re[None] - pw_im[rev][:, :, :, None] * bb_im[None]
    vb_im = pw_re[rev][:, :, :, None] * bb_im[None] + pw_im[rev][:, :, :, None] * bb_re[None]
    v = jnp.concatenate([vb_re, vb_im], axis=2).transpose(1, 0, 3, 2).reshape(ng, chunk * S5_GROUP, 2 * S5_STATE)
    wh = jnp.concatenate([cl_re[1:], -cl_im[1:]], axis=3)
    wh = wh.transpose(1, 3, 0, 2).reshape(ng, 2 * S5_STATE, chunk * S5_GROUP)
    a1 = jnp.concatenate([pw_re[chunk], pw_re[chunk]], axis=1).reshape(1, ng * 2 * S5_STATE)
    a2 = jnp.concatenate([-pw_im[chunk], pw_im[chunk]], axis=1).reshape(1, ng * 2 * S5_STATE)
    return v.astype(BF16), m.astype(BF16), wh.astype(BF16), a1, a2


def _s5(u, h0_re, h0_im, mats, chunk, bt, gc=8):
    b, t_len, _ = u.shape
    ng = N_SSM_GROUPS
    n_chunks = t_len // chunk
    nbc = b // bt
    lc = chunk * S5_GROUP
    rows = n_chunks * bt
    sw = 2 * S5_STATE
    v, m, wh, a1, a2 = mats
    ur = u.reshape(nbc, bt, n_chunks, chunk, ng, S5_GROUP).transpose(0, 4, 2, 1, 3, 5)
    ur = ur.reshape(nbc, ng, rows, lc).astype(BF16)
    h0 = jnp.concatenate([h0_re, h0_im], axis=-1).reshape(nbc, bt, ng * sw)
    kern = functools.partial(_s5_kernel, n_chunks=n_chunks, bt=bt)
    y, ht = pl.pallas_call(
        kern,
        grid=(nbc, ng // gc),
        in_specs=[pl.BlockSpec((1, gc, rows, lc), lambda i, j: (i, j, 0, 0)),
                  pl.BlockSpec((1, bt, gc * sw), lambda i, j: (i, 0, j)),
                  pl.BlockSpec((gc, lc, sw), lambda i, j: (j, 0, 0)),
                  pl.BlockSpec((gc, lc, lc), lambda i, j: (j, 0, 0)),
                  pl.BlockSpec((gc, sw, lc), lambda i, j: (j, 0, 0)),
                  pl.BlockSpec((1, gc * sw), lambda i, j: (0, j)),
                  pl.BlockSpec((1, gc * sw), lambda i, j: (0, j))],
        out_specs=[pl.BlockSpec((1, gc, rows, lc), lambda i, j: (i, j, 0, 0)),
                   pl.BlockSpec((1, bt, gc * sw), lambda i, j: (i, 0, j))],
        out_shape=[jax.ShapeDtypeStruct((nbc, ng, rows, lc), F32),
                   jax.ShapeDtypeStruct((nbc, bt, ng * sw), F32)],
        scratch_shapes=[pltpu.VMEM((rows, gc * sw), F32)],
        compiler_params=_cparams(("parallel", "parallel")),
        name="s5_scan",
    )(ur, h0, v, m, wh, a1, a2)
    y = y.reshape(nbc, ng, n_chunks, bt, chunk, S5_GROUP).transpose(0, 3, 2, 4, 1, 5).reshape(b, t_len, D_SSM)
    ht = ht.reshape(b, ng, sw)
    return y, ht[..., :S5_STATE], ht[..., S5_STATE:]


def _s5_nat_kernel(u_ref, h0_ref, v_ref, m_ref, wh_ref, ar_ref, ai_ref, y_ref, ht_ref, uf_scr, z_scr,
                   *, chunk, n_chunks, bt):
    rows = bt * n_chunks
    half = S5_LANE_GROUPS * S5_STATE
    n_tiles = 2 * half // LANES
    for t in range(chunk):
        piece = u_ref[:, pl.ds(t, n_chunks, stride=chunk), :]
        uf_scr[:, t * LANES:(t + 1) * LANES] = piece.reshape(rows, LANES).astype(BF16)
    uf = uf_scr[...]
    z = _dot(uf, v_ref[0])
    for j in range(n_tiles):
        z_scr[j] = z[:, j * LANES:(j + 1) * LANES]
    ar = ar_ref[...]
    ai = ai_ref[...]

    def step(k, h):
        idx = pl.ds(k, bt, stride=n_chunks)
        zk = jnp.concatenate([z_scr[j, idx, :] for j in range(n_tiles)], axis=1)
        for j in range(n_tiles):
            z_scr[j, idx, :] = h[:, j * LANES:(j + 1) * LANES]
        hre, him = h[:, :half], h[:, half:]
        return jnp.concatenate([ar * hre - ai * him, ar * him + ai * hre], axis=1) + zk

    ht_ref[0] = lax.fori_loop(0, n_chunks, step, h0_ref[0])
    hs = jnp.concatenate([z_scr[j] for j in range(n_tiles)], axis=1).astype(BF16)
    y = _dot(uf, m_ref[0]) + _dot(hs, wh_ref[0])
    for t in range(chunk):
        y_ref[:, pl.ds(t, n_chunks, stride=chunk), :] = y[:, t * LANES:(t + 1) * LANES].reshape(bt, n_chunks, LANES)


def _s5_nat_mats(mats, chunk):
    v, m, wh, a1, a2 = mats
    ng = v.shape[0]
    nx = ng // S5_LANE_GROUPS

    def place(n_outer, n_inner):
        sel = np.zeros((S5_LANE_GROUPS, n_outer * n_inner, n_outer * S5_LANE_GROUPS * n_inner), np.float32)
        a, b = np.meshgrid(np.arange(n_outer), np.arange(n_inner), indexing='ij')
        for g in range(S5_LANE_GROUPS):
            sel[g, (a * n_inner + b).ravel(), ((a * S5_LANE_GROUPS + g) * n_inner + b).ravel()] = 1.0
        return jnp.asarray(sel, BF16)

    def block_diag(w, rows, cols):
        w4 = w.reshape(nx, S5_LANE_GROUPS, w.shape[1], w.shape[2])
        right = jnp.einsum('xgrc,gcd->xgrd', w4, cols, preferred_element_type=BF16)
        return jnp.einsum('grs,xgrd->xsd', rows, right, preferred_element_type=BF16)

    io = place(chunk, S5_GROUP)
    st = place(2, S5_STATE)
    v8 = block_diag(v, io, st)
    m8 = block_diag(m, io, io)
    w8 = block_diag(wh, st, io)
    ar = a1.reshape(nx, S5_LANE_GROUPS, 2, S5_STATE)[:, :, 0].reshape(1, ng * S5_STATE)
    ai = a2.reshape(nx, S5_LANE_GROUPS, 2, S5_STATE)[:, :, 1].reshape(1, ng * S5_STATE)
    return v8, m8, w8, ar, ai


def _s5_nat(u, h0_re, h0_im, mats, chunk, bt):
    b, t_len, _ = u.shape
    n_chunks = t_len // chunk
    nbc = b // bt
    nx = D_SSM // LANES
    half = S5_LANE_GROUPS * S5_STATE
    gw = 2 * half
    lk = chunk * LANES
    rows = bt * n_chunks
    v8, m8, w8, ar, ai = _s5_nat_mats(mats, chunk)
    planar = lambda a: a.reshape(nbc, bt, nx, 1, half)
    h0 = jnp.concatenate([planar(h0_re), planar(h0_im)], axis=3).reshape(nbc, bt, nx * gw)
    y, ht = pl.pallas_call(
        functools.partial(_s5_nat_kernel, chunk=chunk, n_chunks=n_chunks, bt=bt),
        grid=(nbc, nx),
        in_specs=[pl.BlockSpec((bt, t_len, LANES), lambda i, j: (i, 0, j)),
                  pl.BlockSpec((1, bt, gw), lambda i, j: (i, 0, j)),
                  pl.BlockSpec((1, lk, gw), lambda i, j: (j, 0, 0)),
                  pl.BlockSpec((1, lk, lk), lambda i, j: (j, 0, 0)),
                  pl.BlockSpec((1, gw, lk), lambda i, j: (j, 0, 0)),
                  pl.BlockSpec((1, half), lambda i, j: (0, j)),
                  pl.BlockSpec((1, half), lambda i, j: (0, j))],
        out_specs=[pl.BlockSpec((bt, t_len, LANES), lambda i, j: (i, 0, j)),
                   pl.BlockSpec((1, bt, gw), lambda i, j: (i, 0, j))],
        out_shape=[jax.ShapeDtypeStruct((b, t_len, D_SSM), F32),
                   jax.ShapeDtypeStruct((nbc, bt, nx * gw), F32)],
        scratch_shapes=[pltpu.VMEM((rows, lk), BF16), pltpu.VMEM((gw // LANES, rows, LANES), F32)],
        compiler_params=_cparams(("parallel", "parallel")),
        name="s5_scan_nat",
    )(u, h0, v8, m8, w8, ar, ai)
    ht = ht.reshape(b, nx, 2, S5_LANE_GROUPS, S5_STATE)
    return (y, ht[:, :, 0].reshape(b, N_SSM_GROUPS, S5_STATE), ht[:, :, 1].reshape(b, N_SSM_GROUPS, S5_STATE))


def _epilogue_kernel(x_ref, o_ref, y_ref, u_ref, wz_ref, d_ref, glu0_ref, glu1_ref, gb0_ref, gb1_ref,
                     wa_ref, ws_ref, wo_ref, lng_ref, lnb_ref, out_ref, *, alpha):
    x = x_ref[...]
    z = _dot(x.astype(BF16), wz_ref[...])
    za = z[:, :D_ATT]
    zs = z[:, D_ATT:D_ATT + D_SSM]
    ga = z[:, D_ATT + D_SSM:D_ATT + D_SSM + D_MODEL]
    gs = z[:, D_ATT + D_SSM + D_MODEL:]
    p_att = _dot((o_ref[...] * jax.nn.silu(za)).astype(BF16), wa_ref[...])
    y = jax.nn.gelu(y_ref[...] + d_ref[...] * u_ref[...]).astype(BF16)
    s = (_dot(y, glu0_ref[...]) + gb0_ref[...]) * jax.nn.sigmoid(_dot(y, glu1_ref[...]) + gb1_ref[...])
    p_ssm = _dot((s * jax.nn.silu(zs)).astype(BF16), ws_ref[...])
    merged = jax.nn.sigmoid(ga) * p_att + jax.nn.sigmoid(gs) * p_ssm
    r = alpha * x + _dot(merged.astype(BF16), wo_ref[...])
    mu = jnp.mean(r, axis=-1, keepdims=True)
    c = r - mu
    var = jnp.mean(c * c, axis=-1, keepdims=True)
    out_ref[...] = c * lax.rsqrt(var + LN_EPS) * lng_ref[...] + lnb_ref[...]


def _epilogue(x2d, o2d, y2d, u2d, ew, alpha):
    m = x2d.shape[0]
    tm = min(EPILOGUE_ROWS, m)
    names = ("wz", "d", "glu0", "glu1", "gb0", "gb1", "wa", "ws", "wo", "lng", "lnb")
    ws = [ew[n] for n in names]

    def tok(w):
        return pl.BlockSpec((tm, w), lambda i: (i, 0))

    return pl.pallas_call(
        functools.partial(_epilogue_kernel, alpha=alpha),
        grid=(m // tm,),
        in_specs=[tok(D_MODEL), tok(D_ATT), tok(D_SSM), tok(D_SSM)]
        + [pl.BlockSpec(w.shape, lambda i: (0, 0)) for w in ws],
        out_specs=tok(D_MODEL),
        out_shape=jax.ShapeDtypeStruct((m, D_MODEL), F32),
        compiler_params=_cparams(("parallel",)),
        name="epilogue",
    )(x2d, o2d, y2d, u2d, *ws)


def _layer_weights(l, w_in, cmp_w1, cmp_b1, cmp_w2, cmp_b2, cmp_pos, ssm_d, ssm_glu_w, ssm_glu_b,
                   w_att_out, w_ssm_out, w_o, ln_g, ln_b):
    offs = np.concatenate([[0], np.cumsum(IN_WIDTHS)])
    col = lambda i: w_in[l][:, offs[i]:offs[i + 1]]
    perm = _pair_perm()
    gate = jnp.pad(col(7), ((0, 0), (0, GATE_PAD - 3 * N_HEADS)))
    w_seq = jnp.concatenate([col(0)[:, perm], col(1), col(2), col(3), col(4), col(5), col(6), gate, col(9)],
                            axis=1).astype(BF16)
    wz = jnp.concatenate([col(8)[:, perm], col(10), col(11)], axis=1).astype(BF16)

    half = CMP_BLOCK // 2
    eye = jnp.eye(4, dtype=F32)
    eye_g = jnp.eye(N_KV, dtype=F32)
    w1, pos = [], []
    for kv in range(2):
        bd = jnp.einsum('pdh,st->psdth', cmp_w1[l, kv], eye_g)
        bd = bd.reshape(CMP_BLOCK, KV_W, N_KV * CMP_HIDDEN)
        w1.append(jnp.concatenate([bd[:half].reshape(half * KV_W, -1), bd[half:].reshape(half * KV_W, -1)], axis=1))
        pos_g = jnp.broadcast_to(cmp_pos[l, kv][:, None, :], (CMP_BLOCK, N_KV, HEAD_DIM))
        pos.append(jnp.zeros((SUBLANES, half * KV_W), F32)
                   .at[0].set(pos_g[:half].reshape(-1)).at[1].set(pos_g[half:].reshape(-1)))
    w1 = jnp.stack(w1).astype(BF16)
    pos = jnp.stack(pos)
    b1 = jnp.concatenate([cmp_b1[l, 0], cmp_b1[l, 0], cmp_b1[l, 1], cmp_b1[l, 1]]).reshape(1, -1)
    w2s = jnp.stack([cmp_w2[l, 0], cmp_w2[l, 0], cmp_w2[l, 1], cmp_w2[l, 1]])
    w2 = jnp.einsum('shd,st->shtd', w2s, eye).reshape(4 * CMP_HIDDEN, 4 * HEAD_DIM).astype(BF16)
    b2 = jnp.concatenate([cmp_b2[l, 0], cmp_b2[l, 0], cmp_b2[l, 1], cmp_b2[l, 1]]).reshape(1, -1)
    cw = dict(w1=w1, pos=pos, b1=b1, w2=w2, b2=b2)

    ew = dict(wz=wz, d=ssm_d[l].reshape(1, D_SSM),
              glu0=ssm_glu_w[l, 0].astype(BF16), glu1=ssm_glu_w[l, 1].astype(BF16),
              gb0=ssm_glu_b[l, 0].reshape(1, -1), gb1=ssm_glu_b[l, 1].reshape(1, -1),
              wa=w_att_out[l][perm].astype(BF16), ws=w_ssm_out[l].astype(BF16), wo=w_o[l].astype(BF16),
              lng=ln_g[l].reshape(1, -1), lnb=ln_b[l].reshape(1, -1))
    return w_seq, cw, ew


def _prompt_layer(h_p, lw, s5m, tables, alpha):
    b, seq, _ = h_p.shape
    w_seq, cw, ew = lw
    x2d = h_p.reshape(b * seq, D_MODEL)
    q, kvc, kvs, kvw, kvct, kvst, kvwt, gates, u = _inproj(x2d, w_seq, seq)
    r3 = lambda a: a.reshape(b, seq, a.shape[-1])
    o_att = _nsa_prompt(r3(q), r3(kvc), r3(kvs), r3(kvw), kvst, kvwt, r3(gates), cw, tables)
    zero = jnp.zeros((b, N_SSM_GROUPS, S5_STATE), F32)
    y, hr, hi = _s5_nat(r3(u), zero, zero, s5m, S5_CHUNK, bt=min(S5_BATCH_ROWS, b))
    out = _epilogue(x2d, o_att.reshape(b * seq, D_ATT), y.reshape(b * seq, D_SSM), u, ew, alpha)
    return out.reshape(b, seq, D_MODEL), kvct, kvst, kvwt, hr, hi


def _gather_step(page_dma):
    b, c = pl.program_id(0), pl.program_id(1)
    n_c = pl.num_programs(1)
    step = b * n_c + c
    total = pl.num_programs(0) * n_c
    slot = step % 2

    @pl.when(step == 0)
    def _first():
        for j in range(PAGES_PER_STEP):
            page_dma(b, c, j, slot).start()

    @pl.when(step + 1 < total)
    def _prefetch():
        nxt = step + 1
        for j in range(PAGES_PER_STEP):
            page_dma(nxt // n_c, nxt % n_c, j, 1 - slot).start()

    for j in range(PAGES_PER_STEP):
        page_dma(b, c, j, slot).wait()
    return slot


def _sample_queries(q):
    t = q.shape[0]
    lane = lax.broadcasted_iota(jnp.int32, (t, LANES), 1)
    parts = []
    for g in range(N_KV):
        gmask = (lane < HEAD_DIM) if g == 0 else (lane >= HEAD_DIM)
        parts += [jnp.where(gmask, q[:, r * LANES:(r + 1) * LANES], 0.0) for r in range(HPG)]
    return jnp.concatenate(parts, axis=0).astype(BF16)


def _softmax_rows(s):
    m = jnp.max(s, axis=1, keepdims=True)
    valid = m > 0.5 * NEG
    e = jnp.exp2(s - m)
    return jnp.where(valid, e / jnp.sum(e, axis=1, keepdims=True), 0.0)


def _cmp_sample_kernel(pt_ref, cache_hbm, q_ref, perm_ref, w1_ref, pos_ref, b1_ref, w2_ref, b2_ref, biasc_ref,
                       mimp_ref, oc_ref, sel_ref, buf, sem, x_scr, hf_scr, hs_scr,
                       *, layer, t_len, past_len, n_blk):
    c = pl.program_id(1)
    n_c = pl.num_programs(1)
    nh = 4 * CMP_HIDDEN
    cpp = buf.shape[3] // CMP_STRIDE
    rows = PAGES_PER_STEP * cpp

    def page_dma(b, cc, j, slot):
        page = pt_ref[b, cc * PAGES_PER_STEP + j]
        return pltpu.make_async_copy(cache_hbm.at[layer, page], buf.at[slot, j], sem.at[slot])

    slot = _gather_step(page_dma)

    def regroup(j, carry):
        xp = _dot_nt(perm_ref[...], buf[slot, j].astype(BF16))
        r0 = pl.multiple_of(j * cpp, cpp)
        for p in range(CMP_STRIDE):
            for kv in range(2):
                x_scr[kv, pl.ds(r0, cpp), p * KV_W:(p + 1) * KV_W] = xp[p * cpp:(p + 1) * cpp, kv * KV_W:(kv + 1) * KV_W]
        return carry

    lax.fori_loop(0, PAGES_PER_STEP, regroup, 0, unroll=8)
    hf, hs = _compress_hidden(x_scr[0].astype(BF16), x_scr[1].astype(BF16), w1_ref)
    r0 = pl.multiple_of(c * rows, rows)
    hf_scr[pl.ds(r0, rows), :] = hf
    hs_scr[pl.ds(r0, rows), :] = hs

    @pl.when(c == n_c - 1)
    def _finish():
        n_ch = hf_scr.shape[0]
        c1 = _compress_const(pos_ref, w1_ref, b1_ref)
        hs_next = pltpu.roll(hs_scr[...], n_ch - 1, 0)
        kv = _compress_out(hf_scr[...], hs_next, c1, w2_ref, b2_ref)
        q64 = _sample_queries(q_ref[0])
        p = _softmax_rows(_dot_nt(q64, kv[:, :KV_W].astype(BF16)) + biasc_ref[...])
        oc_ref[0] = _dot(p.astype(BF16), kv[:, KV_W:].astype(BF16))
        psum = []
        for g in range(N_KV):
            acc = p[g * HPG * t_len:g * HPG * t_len + t_len]
            for r in range(1, HPG):
                acc = acc + p[(g * HPG + r) * t_len:(g * HPG + r + 1) * t_len]
            psum.append(acc)
        psum = jnp.concatenate(psum, axis=0)
        p1, p2, p3 = _split3(psum)
        imp = _dot(p1, mimp_ref[...]) + _dot(p2, mimp_ref[...]) + _dot(p3, mimp_ref[...])
        shape = imp.shape
        bj = lax.broadcasted_iota(jnp.int32, shape, 1)
        tpos = past_len + lax.broadcasted_iota(jnp.int32, shape, 0) % t_len
        cur = tpos // SLC_BLOCK
        forced = (bj == 0) | (bj == cur) | (bj == cur - 1)
        visible = bj * SLC_BLOCK <= tpos
        score = jnp.where(forced, -NEG, jnp.where(visible, imp, NEG))
        score = jnp.where(bj < n_blk, score, 2.0 * NEG)
        sel = jnp.zeros(shape, F32)
        for _ in range(min(N_SEL, n_blk)):
            best = jnp.max(score, axis=1, keepdims=True)
            first = jnp.min(jnp.where(score == best, bj, shape[1]), axis=1, keepdims=True)
            hit = bj == first
            sel = jnp.where(hit, 1.0, sel)
            score = jnp.where(hit, 3.0 * NEG, score)
        sel_ref[0] = sel


def _slc_sample_kernel(pt_ref, cache_hbm, q_ref, sel_ref, knew_ref, bias_ref, c31_ref, biasn_ref, eblk_ref,
                       os_ref, buf, sem, m_scr, l_scr, acc_scr, *, layer):
    c = pl.program_id(1)
    n_c = pl.num_programs(1)
    page = buf.shape[2] // PAGES_PER_STEP

    def page_dma(b, cc, j, slot):
        pg = pt_ref[b, cc * PAGES_PER_STEP + j]
        return pltpu.make_async_copy(cache_hbm.at[layer, pg], buf.at[slot, :, pl.ds(j * page, page)], sem.at[slot])

    slot = _gather_step(page_dma)
    q64 = _sample_queries(q_ref[0])

    @pl.when(c == 0)
    def _init():
        m_scr[...] = jnp.full(m_scr.shape, NEG, F32)
        l_scr[...] = jnp.zeros(l_scr.shape, F32)
        acc_scr[...] = jnp.zeros(acc_scr.shape, F32)

    def update(s, vt_bf):
        m_old = m_scr[...]
        m_new = jnp.maximum(m_old, jnp.max(s, axis=1, keepdims=True))
        alpha = jnp.exp2(m_old - m_new)
        e = jnp.exp2(s - m_new)
        l_scr[...] = alpha * l_scr[...] + jnp.sum(e, axis=1, keepdims=True)
        acc_scr[...] = alpha * acc_scr[...] + _dot_nt(e.astype(BF16), vt_bf)
        m_scr[...] = m_new

    neg = (_dot(sel_ref[0, 0].astype(BF16), eblk_ref[...]) - 1.0) * (-NEG)
    s = _dot(q64, buf[slot, 0:KV_W, :].astype(BF16)) + neg

    @pl.when(c < n_c - 1)
    def _far():
        update(s + c31_ref[...], buf[slot, KV_W:KV2, :].astype(BF16))

    @pl.when(c == n_c - 1)
    def _last():
        update(s + bias_ref[...], buf[slot, KV_W:KV2, :].astype(BF16))
        kn = knew_ref[0]
        update(_dot(q64, kn[0:KV_W, :].astype(BF16)) + biasn_ref[...], kn[KV_W:KV2, :].astype(BF16))
        valid = m_scr[...] > 0.5 * NEG
        os_ref[0] = jnp.where(valid, acc_scr[...] / jnp.where(valid, l_scr[...], 1.0), 0.0)


def _win_sample_kernel(q_ref, kvw_ref, biasw_ref, g_ref, oc_ref, os_ref, o_ref, *, t_len):
    q64 = _sample_queries(q_ref[0])
    kw = kvw_ref[0]
    p = _softmax_rows(_dot(q64, kw[0:KV_W, :].astype(BF16)) + biasw_ref[...])
    ow = _dot_nt(p.astype(BF16), kw[KV_W:KV2, :].astype(BF16))
    gates = g_ref[0]
    oc, osel = oc_ref[0], os_ref[0]
    lane = lax.broadcasted_iota(jnp.int32, (t_len, LANES), 1)

    def head(g, r):
        h = g * HPG + r
        sl = slice(h * t_len, (h + 1) * t_len)
        return (gates[:, 3 * h:3 * h + 1] * oc[sl] + gates[:, 3 * h + 1:3 * h + 2] * osel[sl]
                + gates[:, 3 * h + 2:3 * h + 3] * ow[sl])

    for r in range(HPG):
        o_ref[0, :, r * LANES:(r + 1) * LANES] = jnp.where(lane < HEAD_DIM, head(0, r), head(1, r))


def _nsa_sample(layer, q, gates, kvs_new_t, all_w_t, cmp_t, slc_t, page_table, cw, tables):
    b, t_len, _ = q.shape
    n_pages = page_table.shape[1]
    page = cmp_t.shape[3]
    past_len = n_pages * page
    n_c = n_pages // PAGES_PER_STEP
    n_ch = past_len // CMP_STRIDE
    n_blk = -(-(past_len + t_len) // SLC_BLOCK)
    blk_pad = -(-n_blk // LANES) * LANES
    rows = PAGES_PER_STEP * page
    blk_step = rows // SLC_BLOCK
    nq = N_HEADS * t_len
    assert past_len % SLC_BLOCK == 0 and t_len <= SLC_BLOCK and (past_len + t_len) // CMP_STRIDE == n_ch
    biasc, biass, c31, biasn, biasw = tables
    c31 = c31[:, :1]
    mimp = jnp.asarray(_imp_matrix(n_ch, blk_pad, (past_len + t_len) // CMP_STRIDE - 1), BF16)
    cpp = page // CMP_STRIDE
    pm = np.zeros((page, page), np.float32)
    for p in range(CMP_STRIDE):
        for cc in range(cpp):
            pm[p * cpp + cc, cc * CMP_STRIDE + p] = 1.0
    pm = jnp.asarray(pm, BF16)

    def full(a):
        nd = a.ndim
        return pl.BlockSpec(a.shape, lambda i, j, pt, nd=nd: (0,) * nd)

    q_spec = pl.BlockSpec((1, t_len, D_ATT), lambda i, j, pt: (i, 0, 0))
    o_spec = pl.BlockSpec((1, nq, KV_W), lambda i, j, pt: (i, 0, 0))
    any_spec = pl.BlockSpec(memory_space=pl.ANY)
    oc, sel = pl.pallas_call(
        functools.partial(_cmp_sample_kernel, layer=layer, t_len=t_len, past_len=past_len, n_blk=n_blk),
        grid_spec=pltpu.PrefetchScalarGridSpec(
            num_scalar_prefetch=1, grid=(b, n_c),
            in_specs=[any_spec, q_spec, full(pm), full(cw["w1"]), full(cw["pos"]), full(cw["b1"]), full(cw["w2"]),
                      full(cw["b2"]), full(biasc), full(mimp)],
            out_specs=[o_spec, pl.BlockSpec((1, N_KV * t_len, blk_pad), lambda i, j, pt: (i, 0, 0))],
            scratch_shapes=[pltpu.VMEM((2, PAGES_PER_STEP, KV2, page), F32),
                            pltpu.SemaphoreType.DMA((2,)),
                            pltpu.VMEM((2, PAGES_PER_STEP * cpp, CMP_STRIDE * KV_W), F32),
                            pltpu.VMEM((n_ch, 4 * CMP_HIDDEN), F32), pltpu.VMEM((n_ch, 4 * CMP_HIDDEN), F32)]),
        out_shape=[jax.ShapeDtypeStruct((b, nq, KV_W), F32),
                   jax.ShapeDtypeStruct((b, N_KV * t_len, blk_pad), F32)],
        compiler_params=_cparams(("arbitrary", "arbitrary")),
        name="nsa_sample_cmp",
    )(page_table, cmp_t, q, pm, cw["w1"], cw["pos"], cw["b1"], cw["w2"], cw["b2"], biasc, mimp)

    sel_c = sel[:, :, :n_c * blk_step].reshape(b, N_KV, 1, t_len, n_c, blk_step)
    sel_c = jnp.broadcast_to(sel_c, (b, N_KV, HPG, t_len, n_c, blk_step))
    sel_c = sel_c.transpose(0, 4, 1, 2, 3, 5).reshape(b, n_c, nq, blk_step)
    sel_c = jnp.pad(sel_c, ((0, 0), (0, 0), (0, 0), (0, LANES - blk_step)))
    eblk = jnp.asarray(np.arange(LANES)[:, None] == (np.arange(rows)[None, :] // SLC_BLOCK), BF16)
    o_s = pl.pallas_call(
        functools.partial(_slc_sample_kernel, layer=layer),
        grid_spec=pltpu.PrefetchScalarGridSpec(
            num_scalar_prefetch=1, grid=(b, n_c),
            in_specs=[any_spec, q_spec,
                      pl.BlockSpec((1, 1, nq, LANES), lambda i, j, pt: (i, j, 0, 0)),
                      pl.BlockSpec((1, KV2, LANES), lambda i, j, pt: (i, 0, 0)),
                      full(biass), full(c31), full(biasn), full(eblk)],
            out_specs=o_spec,
            scratch_shapes=[pltpu.VMEM((2, KV2, rows), F32),
                            pltpu.SemaphoreType.DMA((2,)),
                            pltpu.VMEM((nq, 1), F32), pltpu.VMEM((nq, 1), F32), pltpu.VMEM((nq, KV_W), F32)]),
        out_shape=jax.ShapeDtypeStruct((b, nq, KV_W), F32),
        compiler_params=_cparams(("arbitrary", "arbitrary")),
        name="nsa_sample_slc",
    )(page_table, slc_t, q, sel_c, kvs_new_t, biass, c31, biasn, eblk)

    w_pad = all_w_t.shape[2]
    return pl.pallas_call(
        functools.partial(_win_sample_kernel, t_len=t_len),
        grid=(b,),
        in_specs=[pl.BlockSpec((1, t_len, D_ATT), lambda i: (i, 0, 0)),
                  pl.BlockSpec((1, KV2, w_pad), lambda i: (i, 0, 0)),
                  pl.BlockSpec(biasw.shape, lambda i: (0, 0)),
                  pl.BlockSpec((1, t_len, GATE_PAD), lambda i: (i, 0, 0)),
                  pl.BlockSpec((1, nq, KV_W), lambda i: (i, 0, 0)),
                  pl.BlockSpec((1, nq, KV_W), lambda i: (i, 0, 0))],
        out_specs=pl.BlockSpec((1, t_len, D_ATT), lambda i: (i, 0, 0)),
        out_shape=jax.ShapeDtypeStruct((b, t_len, D_ATT), F32),
        compiler_params=_cparams(("parallel",)),
        name="nsa_sample_win",
    )(q, all_w_t, biasw, gates, oc, o_s)


def _sample_layer(layer, h_s, lw, s5m, tables, alpha, cmp_t, slc_t, win_t, h0_re, h0_im, page_table, w_pad):
    b, t_len, _ = h_s.shape
    w_seq, cw, ew = lw
    m = b * t_len
    x2d = h_s.reshape(m, D_MODEL)
    q, kvc, kvs, kvw, kvct, kvst, kvwt, gates, u = _inproj(x2d, w_seq, m)
    r3 = lambda a: a.reshape(b, t_len, a.shape[-1])
    per_row = lambda a: a.reshape(KV2, b, t_len).transpose(1, 0, 2)
    w_buf = win_t.shape[2]
    all_w_t = jnp.concatenate([win_t, per_row(kvwt)], axis=2)
    all_w_pad = jnp.pad(all_w_t, ((0, 0), (0, 0), (0, w_pad - (w_buf + t_len))))
    kvs_new_t = jnp.pad(per_row(kvst), ((0, 0), (0, 0), (0, LANES - t_len)))
    o_att = _nsa_sample(layer, r3(q), r3(gates), kvs_new_t, all_w_pad, cmp_t, slc_t, page_table, cw, tables)
    y, hr, hi = _s5(r3(u), h0_re, h0_im, s5m, t_len, bt=b)
    out = _epilogue(x2d, o_att.reshape(m, D_ATT), y.reshape(m, D_SSM), u, ew, alpha)
    return out.reshape(b, t_len, D_MODEL), r3(kvc), r3(kvs), all_w_t[:, :, t_len:], hr, hi


def _feature_major(a):
    nd = a.ndim
    perm = tuple(range(nd - 4)) + (nd - 3, nd - 2, nd - 1, nd - 4)
    t = a.transpose(perm)
    return t.reshape(t.shape[:nd - 4] + (KV2, a.shape[nd - 4]))


def _row_major(a):
    nd = a.ndim
    t = a.reshape(a.shape[:nd - 2] + (2, N_KV, HEAD_DIM, a.shape[-1]))
    perm = tuple(range(nd - 2)) + (nd + 1, nd - 2, nd - 1, nd)
    return t.transpose(perm)


def kernel(x_prompt, x_sample, cache_kv_cmp, cache_kv_slc, state_win_kv, state_ssm_re, state_ssm_im,
           page_table, rel_bias, w_in, cmp_w1, cmp_b1, cmp_w2, cmp_b2, cmp_pos,
           ssm_a_re, ssm_a_im, ssm_log_dt, ssm_b_re, ssm_b_im, ssm_c_re, ssm_c_im, ssm_d,
           ssm_glu_w, ssm_glu_b, w_att_out, w_ssm_out, w_o, ln_g, ln_b):
    depth = w_in.shape[0]
    alpha = (2 * depth) ** 0.25
    bp, seq = x_prompt.shape[:2]
    bs, t_len = x_sample.shape[:2]
    page = cache_kv_cmp.shape[2]
    past_len = page_table.shape[1] * page
    w_buf = state_win_kv.shape[2]
    w_pad = -(-(w_buf + t_len) // LANES) * LANES
    n_win = min(WINDOW, seq)
    p_tables = _prompt_tables(rel_bias, seq)
    s_tables = _sample_tables(rel_bias, past_len, t_len, w_buf, w_pad, PAGES_PER_STEP * page)
    cmp_t, slc_t, win_t = _feature_major(cache_kv_cmp), _feature_major(cache_kv_slc), _feature_major(state_win_kv)
    h_p, h_s = x_prompt, x_sample
    outs_p, outs_s = [], []
    for l in range(depth):
        lw = _layer_weights(l, w_in, cmp_w1, cmp_b1, cmp_w2, cmp_b2, cmp_pos, ssm_d, ssm_glu_w, ssm_glu_b,
                            w_att_out, w_ssm_out, w_o, ln_g, ln_b)
        ssm = (ssm_a_re[l], ssm_a_im[l], ssm_log_dt[l], ssm_b_re[l], ssm_b_im[l], ssm_c_re[l], ssm_c_im[l])
        h_p, kvct, kvst, kvwt, hr, hi = _prompt_layer(h_p, lw, _s5_mats(*ssm, S5_CHUNK), p_tables, alpha)
        outs_p.append((kvct, kvst, kvwt[:, :, seq - n_win:], hr, hi))
        h_s, kvc, kvs, win, hr, hi = _sample_layer(
            l, h_s, lw, _s5_mats(*ssm, t_len), s_tables, alpha, cmp_t, slc_t, win_t[l],
            state_ssm_re[l], state_ssm_im[l], page_table, w_pad)
        outs_s.append((kvc.reshape(bs, t_len, 2, N_KV, HEAD_DIM), kvs.reshape(bs, t_len, 2, N_KV, HEAD_DIM),
                       win, hr, hi))
    stack = lambda outs, i: jnp.stack([o[i] for o in outs])
    return ((h_p, h_s)
            + tuple(_row_major(stack(outs_p, i)) for i in range(3)) + (stack(outs_p, 3), stack(outs_p, 4))
            + (stack(outs_s, 0), stack(outs_s, 1), _row_major(stack(outs_s, 2)), stack(outs_s, 3), stack(outs_s, 4)))
```

```python
import functools
import math

import numpy as np
import jax
import jax.numpy as jnp
from jax import lax
from jax.experimental import pallas as pl
from jax.experimental.pallas import tpu as pltpu

F32 = jnp.float32
BF16 = jnp.bfloat16

D_MODEL = 1024
N_HEADS = 8
HEAD_DIM = 64
N_KV = 2
HPG = N_HEADS // N_KV
D_ATT = N_HEADS * HEAD_DIM
KV_W = N_KV * HEAD_DIM
CMP_BLOCK = 32
CMP_STRIDE = 16
CMP_HIDDEN = 128
SLC_BLOCK = 64
N_SEL = 16
WINDOW = 512
S5_GROUP = 16
S5_STATE = 64
D_SSM = 512
N_SSM_GROUPS = D_SSM // S5_GROUP
N_BUCKETS = 32
MAX_DISTANCE = 128
LN_EPS = 1e-5
IN_WIDTHS = (D_ATT, KV_W, KV_W, KV_W, KV_W, KV_W, KV_W, 3 * N_HEADS, D_ATT, D_SSM, D_SSM, 2 * D_MODEL)

LANES = 128
SUBLANES = 8
VMEM_LIMIT = 56 * 1024 * 1024
NEG = -1e30
LOG2E = 1.4426950408889634
VROWS = HEAD_DIM + 16
TQ = 128
TK = 128
FAR_TILES = 8
INPROJ_ROWS = 1024
EPILOGUE_ROWS = 512
KV2 = 2 * KV_W
GATE_PAD = LANES
S5_CHUNK = 8
S5_LANE_GROUPS = LANES // S5_GROUP
S5_BATCH_ROWS = 4
SEQ_COLS = D_ATT + 3 * KV2 + GATE_PAD + D_SSM
QCOLS = N_HEADS * TQ
GCOLS = HPG * TQ
PAGES_PER_STEP = 64


def _cparams(sem):
    return pltpu.CompilerParams(dimension_semantics=sem, vmem_limit_bytes=VMEM_LIMIT)


def _dot(a, b):
    return jnp.dot(a, b, preferred_element_type=F32)


def _dot_nt(a, b):
    return lax.dot_general(a, b, (((1,), (1,)), ((), ())), preferred_element_type=F32)


def _split3(x):
    h1 = x.astype(BF16)
    r1 = x - h1.astype(F32)
    h2 = r1.astype(BF16)
    h3 = (r1 - h2.astype(F32)).astype(BF16)
    return h1, h2, h3


def _bucket_np(dist):
    n = np.maximum(dist, 0)
    max_exact = N_BUCKETS // 2
    nf = np.maximum(n, 1).astype(np.float32)
    val = (np.log(nf / np.float32(max_exact)) / np.float32(math.log(MAX_DISTANCE / max_exact))
           * np.float32(N_BUCKETS - max_exact))
    large = max_exact + val.astype(np.int32)
    return np.where(n < max_exact, n, np.minimum(large, N_BUCKETS - 1)).astype(np.int32)


def _pair_perm():
    cols = []
    for r in range(HPG):
        cols += list(range(r * HEAD_DIM, (r + 1) * HEAD_DIM))
        cols += list(range((HPG + r) * HEAD_DIM, (HPG + r + 1) * HEAD_DIM))
    return np.asarray(cols, np.int32)


def _imp_matrix(n_cmp_pad, n_blk_pad, n_cmp):
    ratio = SLC_BLOCK // CMP_STRIDE
    n = np.arange(n_cmp_pad)[:, None]
    j = np.arange(n_blk_pad)[None, :]
    m = ((n >= ratio * j - 1) & (n <= ratio * j + ratio - 2)).astype(np.float32)
    m += ((n >= ratio * j) & (n <= ratio * j + ratio - 1)).astype(np.float32)
    m *= (n < n_cmp)
    return m


def _bucket_starts():
    b = _bucket_np(np.arange(2 * MAX_DISTANCE))
    assert (np.diff(b) >= 0).all() and b.max() == N_BUCKETS - 1
    starts = [int(np.argmax(b >= k)) for k in range(1, N_BUCKETS)]
    assert all(b[s] == k for k, s in zip(range(1, N_BUCKETS), starts))
    return starts


def _bias_of_dist(d, bucket_value):
    val = bucket_value(0)
    for k, start in zip(range(1, N_BUCKETS), _bucket_starts()):
        val = jnp.where(d >= start, bucket_value(k), val)
    return val


def _prompt_tables_kernel(rb_ref, tn_ref, t4_ref, cm_ref, *, row_shift):
    bias = lambda d: _bias_of_dist(d, lambda k: rb_ref[k:k + 1, :])
    last = rb_ref[N_BUCKETS - 1:N_BUCKETS, :]
    shape = (TK, QCOLS)
    d0 = lax.broadcasted_iota(jnp.int32, shape, 1) % TQ - lax.broadcasted_iota(jnp.int32, shape, 0)
    tn_ref[0:TK, :] = (bias(TK + d0) - last) * LOG2E
    tn_ref[TK:2 * TK, :] = jnp.where(d0 >= 0, (bias(d0) - last) * LOG2E, NEG)
    t4_ref[...] = jnp.where(d0 <= 0, 0.0, NEG)
    shape = cm_ref.shape
    nn = lax.broadcasted_iota(jnp.int32, shape, 0) - row_shift
    dc = lax.broadcasted_iota(jnp.int32, shape, 1) % TQ - (nn * CMP_STRIDE + CMP_BLOCK - 1)
    cm_ref[...] = jnp.where(dc >= 0, bias(dc) * LOG2E, NEG)


def _prompt_tables(rel_bias, seq):
    assert WINDOW == 4 * TK and TQ == TK and _bucket_starts()[-1] <= TK
    n_qt = seq // TQ
    n_ch = seq // CMP_STRIDE
    row_shift = (n_qt - 1) * (TQ // CMP_STRIDE)
    tile = jax.ShapeDtypeStruct((TK, QCOLS), F32)
    return pl.pallas_call(
        functools.partial(_prompt_tables_kernel, row_shift=row_shift),
        out_shape=[jax.ShapeDtypeStruct((2 * TK, QCOLS), F32), tile,
                   jax.ShapeDtypeStruct((n_ch + row_shift, QCOLS), F32)],
        compiler_params=pltpu.CompilerParams(vmem_limit_bytes=VMEM_LIMIT),
        name="prompt_tables",
    )(jnp.repeat(rel_bias, TQ, axis=1))


def _sample_tables_kernel(rb_ref, bc_ref, bs_ref, c31_ref, bn_ref, bw_ref, *, past_len, t_len, w_buf):
    bias = lambda d: _bias_of_dist(d, lambda k: rb_ref[:, k:k + 1]) * LOG2E
    n_tok = past_len + t_len

    def grid(ref):
        qpos = past_len + lax.broadcasted_iota(jnp.int32, ref.shape, 0) % t_len
        return qpos, lax.broadcasted_iota(jnp.int32, ref.shape, 1)

    qpos, nn = grid(bc_ref)
    dc = qpos - (nn * CMP_STRIDE + CMP_BLOCK - 1)
    bc_ref[...] = jnp.where((dc >= 0) & (nn < n_tok // CMP_STRIDE - 1), bias(dc), NEG)
    qpos, ll = grid(bs_ref)
    bs_ref[...] = bias(qpos - (past_len - bs_ref.shape[1] + ll))
    c31_ref[...] = jnp.broadcast_to(rb_ref[:, N_BUCKETS - 1:N_BUCKETS] * LOG2E, c31_ref.shape)
    qpos, ll = grid(bn_ref)
    dn = qpos - (past_len + ll)
    bn_ref[...] = jnp.where((dn >= 0) & (ll < t_len), bias(dn), NEG)
    qpos, ii = grid(bw_ref)
    w_pos = n_tok - (w_buf + t_len) + ii
    dw = qpos - w_pos
    bw_ref[...] = jnp.where((dw >= 0) & (dw <= WINDOW) & (w_pos >= 0) & (ii < w_buf + t_len), bias(dw), NEG)


def _sample_tables(rel_bias, past_len, t_len, w_buf, w_pad, last_rows):
    assert _bucket_starts()[-1] <= last_rows
    nq = N_HEADS * t_len
    shp = lambda n: jax.ShapeDtypeStruct((nq, n), F32)
    return pl.pallas_call(
        functools.partial(_sample_tables_kernel, past_len=past_len, t_len=t_len, w_buf=w_buf),
        out_shape=[shp(past_len // CMP_STRIDE), shp(last_rows), shp(LANES), shp(LANES), shp(w_pad)],
        compiler_params=pltpu.CompilerParams(vmem_limit_bytes=VMEM_LIMIT),
        name="sample_tables",
    )(jnp.repeat(rel_bias.T, t_len, axis=0))


def _inproj_kernel(x_ref, w_ref, q_ref, kvc_ref, kvs_ref, kvw_ref, kvct_ref, kvst_ref, kvwt_ref, g_ref, u_ref):
    h = _dot(x_ref[...].astype(BF16), w_ref[...])
    o = 0
    q_ref[...] = h[:, o:o + D_ATT] * (HEAD_DIM ** -0.5 * LOG2E)
    o += D_ATT
    for row_ref, t_ref in ((kvc_ref, kvct_ref), (kvs_ref, kvst_ref), (kvw_ref, kvwt_ref)):
        kv = h[:, o:o + KV2]
        row_ref[...] = kv
        t_ref[0] = kv.T
        o += KV2
    g_ref[...] = jax.nn.sigmoid(h[:, o:o + GATE_PAD])
    o += GATE_PAD
    u_ref[...] = h[:, o:o + D_SSM]


def _inproj(x2d, w_seq, rows_per_seq):
    m = x2d.shape[0]
    tm = min(INPROJ_ROWS, m)
    tps = rows_per_seq // tm
    tok = lambda w: pl.BlockSpec((tm, w), lambda i: (i, 0))
    ft = pl.BlockSpec((1, KV2, tm), lambda i: (i // tps, 0, i % tps))
    tok_shape = lambda w: jax.ShapeDtypeStruct((m, w), F32)
    ft_shape = jax.ShapeDtypeStruct((m // rows_per_seq, KV2, rows_per_seq), F32)
    return pl.pallas_call(
        _inproj_kernel,
        grid=(m // tm,),
        in_specs=[tok(D_MODEL), pl.BlockSpec((D_MODEL, SEQ_COLS), lambda i: (0, 0))],
        out_specs=[tok(D_ATT), tok(KV2), tok(KV2), tok(KV2), ft, ft, ft, tok(GATE_PAD), tok(D_SSM)],
        out_shape=[tok_shape(D_ATT), tok_shape(KV2), tok_shape(KV2), tok_shape(KV2), ft_shape, ft_shape, ft_shape,
                   tok_shape(GATE_PAD), tok_shape(D_SSM)],
        compiler_params=_cparams(("parallel",)),
        name="inproj",
    )(x2d, w_seq)


def _compress_out(hf, hs_next, c1, w2_ref, b2_ref):
    h = hf + hs_next + c1
    return _dot(jax.nn.silu(h).astype(BF16), w2_ref[...]) + b2_ref[...]


def _compress_hidden(xk, xv, w1_ref):
    hk = _dot(xk, w1_ref[0])
    hv = _dot(xv, w1_ref[1])
    nh = N_KV * CMP_HIDDEN
    return (jnp.concatenate([hk[:, :nh], hv[:, :nh]], axis=1), jnp.concatenate([hk[:, nh:], hv[:, nh:]], axis=1))


def _compress_const(pos_ref, w1_ref, b1_ref):
    hf, hs = _compress_hidden(pos_ref[0].astype(BF16), pos_ref[1].astype(BF16), w1_ref)
    return hf[0:1] + hs[1:2] + b1_ref[...]


def _value_rows(v_scr, g, vt):
    n = v_scr.shape[2]
    pad = lax.broadcasted_iota(jnp.int32, (VROWS - HEAD_DIM, n), 0) == 0
    v_scr[g, 0:HEAD_DIM, :] = vt.astype(BF16)
    v_scr[g, HEAD_DIM:VROWS, :] = jnp.where(pad, 1.0, 0.0).astype(BF16)


def _flash_init(m_ref, acc_ref):
    m_ref[...] = jnp.full(m_ref.shape, NEG, F32)
    acc_ref[...] = jnp.zeros(acc_ref.shape, F32)


def _flash_block(k_scr, v_scr, qb_scr, k0, tk, head_ref, tail_ref, neg_scr, m_scr, acc_scr):
    st = _dot_nt(k_scr[pl.ds(k0, tk), :], qb_scr[...])
    if tail_ref is not None:
        nt = min(tail_ref.shape[0], tk)
        tail = st[tk - nt:] + tail_ref[tail_ref.shape[0] - nt:, :]
        st = tail if nt == tk else jnp.concatenate([st[:tk - nt], tail], axis=0)
    if head_ref is not None:
        nh = head_ref.shape[0]
        st = jnp.concatenate([st[:nh] + head_ref[...], st[nh:]], axis=0)
    if neg_scr is not None:
        ng = neg_scr[pl.ds(k0, tk), :]
        st = st + jnp.concatenate([ng[:, :TQ]] * HPG + [ng[:, TQ:]] * HPG, axis=1)
    m_old = m_scr[...]
    m_new = jnp.maximum(m_old, jnp.max(st, axis=0, keepdims=True))
    e = jnp.exp2(st - m_new).astype(BF16)
    pv = jnp.concatenate([_dot(v_scr[g, :, pl.ds(k0, tk)], e[:, g * GCOLS:(g + 1) * GCOLS])
                          for g in range(N_KV)], axis=1)
    acc_scr[...] = jnp.exp2(m_old - m_new) * acc_scr[...] + pv
    m_scr[...] = m_new


def _flash_add(tot_scr, m_scr, acc_scr, gate):
    valid = m_scr[...] > 0.5 * NEG
    coef = jnp.where(valid, 1.0 / jnp.where(valid, acc_scr[HEAD_DIM:HEAD_DIM + 1, :], 1.0), 0.0)
    tot_scr[...] = tot_scr[...] + acc_scr[0:HEAD_DIM, :] * (coef * gate)


def _nsa_prompt_kernel(q_ref, kck_ref, kcv_ref, ks_ref, kw_ref, vst_ref, vwt_ref, g_ref,
                       w1_ref, pos_ref, b1_ref, w2_ref, b2_ref,
                       cmaster_ref, tn_ref, t4_ref, mimp_ref, eblk_ref,
                       o_ref,
                       kc_scr, vc_scr, ks_scr, vs_scr, kw_scr, vw_scr, qb_scr, neg_scr, m_scr, acc_scr, tot_scr):
    qt = pl.program_id(1)
    n_qt = pl.num_programs(1)
    seq = ks_ref.shape[1]
    n_ch = seq // CMP_STRIDE
    n_blk = seq // SLC_BLOCK
    nh = 4 * CMP_HIDDEN

    @pl.when(qt == 0)
    def _per_batch():
        flat = lambda ref: jnp.concatenate(
            [ref[0, pl.ds(p, n_ch, stride=CMP_STRIDE), :] for p in range(CMP_STRIDE)], axis=1).astype(BF16)
        hf, hs = _compress_hidden(flat(kck_ref), flat(kcv_ref), w1_ref)
        c1 = _compress_const(pos_ref, w1_ref, b1_ref)
        hs_next = pltpu.roll(hs, n_ch - 1, 0)
        kv = _compress_out(hf, hs_next, c1, w2_ref, b2_ref)
        kc_scr[...] = kv[:, :KV_W].astype(BF16)
        vct = kv[:, KV_W:].T
        ks_scr[...] = ks_ref[0].astype(BF16)
        kw_scr[...] = kw_ref[0].astype(BF16)
        for g in range(N_KV):
            rows = slice(g * HEAD_DIM, (g + 1) * HEAD_DIM)
            _value_rows(vc_scr, g, vct[rows, :])
            _value_rows(vs_scr, g, vst_ref[0, rows, :])
            _value_rows(vw_scr, g, vwt_ref[0, rows, :])

    q = q_ref[0]
    gt = g_ref[0].T
    lane = lax.broadcasted_iota(jnp.int32, (TQ, LANES), 1)
    for h in range(N_HEADS):
        g, r = divmod(h, HPG)
        gmask = (lane < HEAD_DIM) if g == 0 else (lane >= HEAD_DIM)
        qb_scr[h * TQ:(h + 1) * TQ, :] = jnp.where(gmask, q[:, r * LANES:(r + 1) * LANES], 0.0).astype(BF16)
    q0 = qt * TQ

    def gate_cols(j):
        return jnp.concatenate([gt[3 * h + j:3 * h + j + 1, :] for h in range(N_HEADS)], axis=1)

    r0 = pl.multiple_of((n_qt - 1 - qt) * (TQ // CMP_STRIDE), SUBLANES)
    sc = _dot_nt(kc_scr[...], qb_scr[...]) + cmaster_ref[pl.ds(r0, n_ch), :]
    m = jnp.max(sc, axis=0, keepdims=True)
    e = jnp.exp2(sc - m)
    pt = jnp.where(m > 0.5 * NEG, e / jnp.sum(e, axis=0, keepdims=True), 0.0)
    pb = pt.astype(BF16)
    oc = jnp.concatenate([_dot(vc_scr[g, 0:HEAD_DIM, :], pb[:, g * GCOLS:(g + 1) * GCOLS]) for g in range(N_KV)],
                         axis=1)
    tot_scr[...] = oc * gate_cols(0)

    @pl.when(qt == 0)
    def _no_mask():
        neg_scr[...] = jnp.zeros(neg_scr.shape, F32)

    @pl.when((qt + 1) * TQ > N_SEL * SLC_BLOCK)
    def _select():
        psum = []
        for g in range(N_KV):
            acc = pt[:, g * GCOLS:g * GCOLS + TQ]
            for r in range(1, HPG):
                acc = acc + pt[:, g * GCOLS + r * TQ:g * GCOLS + (r + 1) * TQ]
            psum.append(acc)
        p1, p2, p3 = _split3(jnp.concatenate(psum, axis=1))
        imp = _dot(mimp_ref[...], p1) + _dot(mimp_ref[...], p2) + _dot(mimp_ref[...], p3)
        shape = (n_blk, N_KV * TQ)
        bj = lax.broadcasted_iota(jnp.int32, shape, 0)
        tpos = q0 + lax.broadcasted_iota(jnp.int32, shape, 1) % TQ
        cur = tpos // SLC_BLOCK
        forced = (bj == 0) | (bj == cur) | (bj == cur - 1)
        visible = bj * SLC_BLOCK <= tpos
        score = jnp.where(forced, -NEG, jnp.where(visible, imp, NEG))
        rank = jnp.zeros(shape, F32)
        for i in range(n_blk):
            si = score[i:i + 1, :]
            ahead = (si > score) | ((si == score) & (bj > i))
            rank = rank + jnp.where(ahead, 1.0, 0.0)
        unsel = jnp.where(rank < float(N_SEL), 0.0, 1.0).astype(BF16)
        neg_scr[...] = _dot(eblk_ref[...], unsel)

    _flash_init(m_scr, acc_scr)

    def sel_block(k0, n_tiles, tail_ref):
        _flash_block(ks_scr, vs_scr, qb_scr, k0, n_tiles * TK, None, tail_ref, neg_scr, m_scr, acc_scr)

    n_far = jnp.maximum(qt - 1, 0)

    def far_body(j, carry):
        sel_block(pl.multiple_of(j * FAR_TILES * TK, FAR_TILES * TK), FAR_TILES, None)
        return carry

    lax.fori_loop(0, n_far // FAR_TILES, far_body, 0)
    for rem in range(FAR_TILES):
        @pl.when((qt >= 1) & (n_far % FAR_TILES == rem))
        def _sel_last(rem=rem):
            sel_block(pl.multiple_of((qt - 1 - rem) * TK, TK), rem + 2, tn_ref)

    @pl.when(qt == 0)
    def _sel_first():
        sel_block(0, 1, tn_ref)

    _flash_add(tot_scr, m_scr, acc_scr, gate_cols(1))

    _flash_init(m_scr, acc_scr)
    n_win = WINDOW // TK
    for n_prev in range(n_win + 1):
        cond = (qt == n_prev) if n_prev < n_win else (qt >= n_win)

        @pl.when(cond)
        def _win(n_prev=n_prev):
            _flash_block(kw_scr, vw_scr, qb_scr, pl.multiple_of((qt - n_prev) * TK, TK), (n_prev + 1) * TK,
                         t4_ref if n_prev == n_win else None, tn_ref, None, m_scr, acc_scr)

    _flash_add(tot_scr, m_scr, acc_scr, gate_cols(2))
    tot = tot_scr[...]
    for r in range(HPG):
        blk = jnp.concatenate([tot[:, r * TQ:(r + 1) * TQ], tot[:, GCOLS + r * TQ:GCOLS + (r + 1) * TQ]], axis=0)
        o_ref[0, :, r * LANES:(r + 1) * LANES] = blk.T


def _nsa_prompt(q, kvc, kvs, kvw, kvst, kvwt, gates, cw, tables):
    b, seq, _ = q.shape
    n_qt = seq // TQ
    n_ch = seq // CMP_STRIDE
    n_blk = seq // SLC_BLOCK
    tn, t4, cmaster = tables
    mimp = jnp.asarray(_imp_matrix(n_ch, n_blk, n_ch - 1).T, BF16)
    in_blk = np.arange(seq)[:, None] // SLC_BLOCK == np.arange(n_blk)[None, :]
    eblk = jnp.asarray(np.where(in_blk, NEG, 0.0), BF16)

    def full(a):
        nd = a.ndim
        return pl.BlockSpec(a.shape, lambda i, j, nd=nd: (0,) * nd)

    k_spec = pl.BlockSpec((1, seq, KV_W), lambda i, j: (i, 0, 0))
    vt_spec = pl.BlockSpec((1, KV_W, seq), lambda i, j: (i, 1, 0))
    return pl.pallas_call(
        _nsa_prompt_kernel,
        grid=(b, n_qt),
        in_specs=[pl.BlockSpec((1, TQ, D_ATT), lambda i, j: (i, j, 0)),
                  k_spec, pl.BlockSpec((1, seq, KV_W), lambda i, j: (i, 0, 1)),
                  k_spec, k_spec, vt_spec, vt_spec,
                  pl.BlockSpec((1, TQ, GATE_PAD), lambda i, j: (i, j, 0)),
                  full(cw["w1"]), full(cw["pos"]), full(cw["b1"]), full(cw["w2"]), full(cw["b2"]),
                  full(cmaster), full(tn), full(t4), full(mimp), full(eblk)],
        out_specs=pl.BlockSpec((1, TQ, D_ATT), lambda i, j: (i, j, 0)),
        out_shape=jax.ShapeDtypeStruct((b, seq, D_ATT), F32),
        scratch_shapes=[pltpu.VMEM((n_ch, KV_W), BF16), pltpu.VMEM((N_KV, VROWS, n_ch), BF16),
                        pltpu.VMEM((seq, KV_W), BF16), pltpu.VMEM((N_KV, VROWS, seq), BF16),
                        pltpu.VMEM((seq, KV_W), BF16), pltpu.VMEM((N_KV, VROWS, seq), BF16),
                        pltpu.VMEM((QCOLS, LANES), BF16),
                        pltpu.VMEM((seq, N_KV * TQ), F32),
                        pltpu.VMEM((1, QCOLS), F32), pltpu.VMEM((VROWS, QCOLS), F32),
                        pltpu.VMEM((HEAD_DIM, QCOLS), F32)],
        compiler_params=_cparams(("arbitrary", "arbitrary")),
        name="nsa_prompt",
    )(q, kvc, kvc, kvs, kvw, kvst, kvwt, gates, cw["w1"], cw["pos"], cw["b1"], cw["w2"], cw["b2"],
      cmaster, tn, t4, mimp, eblk)


def _s5_kernel(u_ref, h0_ref, v_ref, m_ref, wh_ref, a1_ref, a2_ref, y_ref, ht_ref, z_scr, *, n_chunks, bt):
    gc = u_ref.shape[1]
    sw = 2 * S5_STATE
    for g in range(gc):
        z_scr[:, g * sw:(g + 1) * sw] = _dot(u_ref[0, g], v_ref[g])
    a1 = a1_ref[...]
    a2 = a2_ref[...]

    def step(k, h):
        r0 = pl.multiple_of(k * bt, bt)
        z = z_scr[pl.ds(r0, bt), :]
        z_scr[pl.ds(r0, bt), :] = h
        hsw = jnp.concatenate(
            [pltpu.roll(h[:, g * sw:(g + 1) * sw], S5_STATE, 1) for g in range(gc)], axis=1)
        return a1 * h + a2 * hsw + z

    ht_ref[0] = lax.fori_loop(0, n_chunks, step, h0_ref[0])
    for g in range(gc):
        y_ref[0, g] = (_dot(u_ref[0, g], m_ref[g])
                       + _dot(z_scr[:, g * sw:(g + 1) * sw].astype(BF16), wh_ref[g]))


def _s5_mats(a_re, a_im, log_dt, b_re, b_im, c_re, c_im, chunk):
    hp = lax.Precision.HIGHEST
    dt = jnp.exp(log_dt)[:, None]
    mag = jnp.exp(dt * a_re)
    ab_re, ab_im = mag * jnp.cos(dt * a_im), mag * jnp.sin(dt * a_im)
    den = a_re * a_re + a_im * a_im
    x_re, x_im = ab_re - 1.0, ab_im
    k_re = (x_re * a_re + x_im * a_im) / den
    k_im = (x_im * a_re - x_re * a_im) / den
    bb_re = k_re[..., None] * b_re - k_im[..., None] * b_im
    bb_im = k_re[..., None] * b_im + k_im[..., None] * b_re
    pw_re, pw_im = [jnp.ones_like(ab_re)], [jnp.zeros_like(ab_im)]
    for _ in range(chunk):
        pr, pi = pw_re[-1], pw_im[-1]
        pw_re.append(pr * ab_re - pi * ab_im)
        pw_im.append(pr * ab_im + pi * ab_re)
    pw_re, pw_im = jnp.stack(pw_re), jnp.stack(pw_im)
    cl_re = c_re[None] * pw_re[:, :, None, :] - c_im[None] * pw_im[:, :, None, :]
    cl_im = c_re[None] * pw_im[:, :, None, :] + c_im[None] * pw_re[:, :, None, :]
    t = (jnp.einsum('tgcp,gpd->tgcd', cl_re, bb_re, precision=hp)
         - jnp.einsum('tgcp,gpd->tgcd', cl_im, bb_im, precision=hp))
    s_i = np.arange(chunk)[:, None]
    t_i = np.arange(chunk)[None, :]
    lag = np.clip(t_i - s_i, 0, None)
    blk = jnp.where((t_i >= s_i)[:, :, None, None, None], t[lag], 0.0)
    ng = a_re.shape[0]
    m = blk.transpose(2, 0, 4, 1, 3).reshape(ng, chunk * S5_GROUP, chunk * S5_GROUP)
    rev = chunk - 1 - np.arange(chunk)
    vb_re = pw_re[rev][:, :, :, None] * bb_re[None] - pw_im[rev][:, :, :, None] * bb_im[None]
    vb_im = pw_re[rev][:, :, :, None] * bb_im[None] + pw_im[rev][:, :, :, None] * bb_re[None]
    v = jnp.concatenate([vb_re, vb_im], axis=2).transpose(1, 0, 3, 2).reshape(ng, chunk * S5_GROUP, 2 * S5_STATE)
    wh = jnp.concatenate([cl_re[1:], -cl_im[1:]], axis=3)
    wh = wh.transpose(1, 3, 0, 2).reshape(ng, 2 * S5_STATE, chunk * S5_GROUP)
    a1 = jnp.concatenate([pw_re[chunk], pw_re[chunk]], axis=1).reshape(1, ng * 2 * S5_STATE)
    a2 = jnp.concatenate([-pw_im[chunk], pw_im[chunk]], axis=1).reshape(1, ng * 2 * S5_STATE)
    return v.astype(BF16), m.astype(BF16), wh.astype(BF16), a1, a2


def _s5(u, h0_re, h0_im, mats, chunk, bt, gc=8):
    b, t_len, _ = u.shape
    ng = N_SSM_GROUPS
    n_chunks = t_len // chunk
    nbc = b // bt
    lc = chunk * S5_GROUP
    rows = n_chunks * bt
    sw = 2 * S5_STATE
    v, m, wh, a1, a2 = mats
    ur = u.reshape(nbc, bt, n_chunks, chunk, ng, S5_GROUP).transpose(0, 4, 2, 1, 3, 5)
    ur = ur.reshape(nbc, ng, rows, lc).astype(BF16)
    h0 = jnp.concatenate([h0_re, h0_im], axis=-1).reshape(nbc, bt, ng * sw)
    kern = functools.partial(_s5_kernel, n_chunks=n_chunks, bt=bt)
    y, ht = pl.pallas_call(
        kern,
        grid=(nbc, ng // gc),
        in_specs=[pl.BlockSpec((1, gc, rows, lc), lambda i, j: (i, j, 0, 0)),
                  pl.BlockSpec((1, bt, gc * sw), lambda i, j: (i, 0, j)),
                  pl.BlockSpec((gc, lc, sw), lambda i, j: (j, 0, 0)),
                  pl.BlockSpec((gc, lc, lc), lambda i, j: (j, 0, 0)),
                  pl.BlockSpec((gc, sw, lc), lambda i, j: (j, 0, 0)),
                  pl.BlockSpec((1, gc * sw), lambda i, j: (0, j)),
                  pl.BlockSpec((1, gc * sw), lambda i, j: (0, j))],
        out_specs=[pl.BlockSpec((1, gc, rows, lc), lambda i, j: (i, j, 0, 0)),
                   pl.BlockSpec((1, bt, gc * sw), lambda i, j: (i, 0, j))],
        out_shape=[jax.ShapeDtypeStruct((nbc, ng, rows, lc), F32),
                   jax.ShapeDtypeStruct((nbc, bt, ng * sw), F32)],
        scratch_shapes=[pltpu.VMEM((rows, gc * sw), F32)],
        compiler_params=_cparams(("parallel", "parallel")),
        name="s5_scan",
    )(ur, h0, v, m, wh, a1, a2)
    y = y.reshape(nbc, ng, n_chunks, bt, chunk, S5_GROUP).transpose(0, 3, 2, 4, 1, 5).reshape(b, t_len, D_SSM)
    ht = ht.reshape(b, ng, sw)
    return y, ht[..., :S5_STATE], ht[..., S5_STATE:]


def _s5_nat_kernel(u_ref, h0_ref, v_ref, m_ref, wh_ref, ar_ref, ai_ref, y_ref, ht_ref, uf_scr, z_scr,
                   *, chunk, n_chunks, bt):
    rows = bt * n_chunks
    half = S5_LANE_GROUPS * S5_STATE
    n_tiles = 2 * half // LANES
    for t in range(chunk):
        piece = u_ref[:, pl.ds(t, n_chunks, stride=chunk), :]
        uf_scr[:, t * LANES:(t + 1) * LANES] = piece.reshape(rows, LANES).astype(BF16)
    uf = uf_scr[...]
    z = _dot(uf, v_ref[0])
    for j in range(n_tiles):
        z_scr[j] = z[:, j * LANES:(j + 1) * LANES]
    ar = ar_ref[...]
    ai = ai_ref[...]

    def step(k, h):
        idx = pl.ds(k, bt, stride=n_chunks)
        zk = jnp.concatenate([z_scr[j, idx, :] for j in range(n_tiles)], axis=1)
        for j in range(n_tiles):
            z_scr[j, idx, :] = h[:, j * LANES:(j + 1) * LANES]
        hre, him = h[:, :half], h[:, half:]
        return jnp.concatenate([ar * hre - ai * him, ar * him + ai * hre], axis=1) + zk

    ht_ref[0] = lax.fori_loop(0, n_chunks, step, h0_ref[0])
    hs = jnp.concatenate([z_scr[j] for j in range(n_tiles)], axis=1).astype(BF16)
    y = _dot(uf, m_ref[0]) + _dot(hs, wh_ref[0])
    for t in range(chunk):
        y_ref[:, pl.ds(t, n_chunks, stride=chunk), :] = y[:, t * LANES:(t + 1) * LANES].reshape(bt, n_chunks, LANES)


def _s5_nat_mats(mats, chunk):
    v, m, wh, a1, a2 = mats
    ng = v.shape[0]
    nx = ng // S5_LANE_GROUPS

    def place(n_outer, n_inner):
        sel = np.zeros((S5_LANE_GROUPS, n_outer * n_inner, n_outer * S5_LANE_GROUPS * n_inner), np.float32)
        a, b = np.meshgrid(np.arange(n_outer), np.arange(n_inner), indexing='ij')
        for g in range(S5_LANE_GROUPS):
            sel[g, (a * n_inner + b).ravel(), ((a * S5_LANE_GROUPS + g) * n_inner + b).ravel()] = 1.0
        return jnp.asarray(sel, BF16)

    def block_diag(w, rows, cols):
        w4 = w.reshape(nx, S5_LANE_GROUPS, w.shape[1], w.shape[2])
        right = jnp.einsum('xgrc,gcd->xgrd', w4, cols, preferred_element_type=BF16)
        return jnp.einsum('grs,xgrd->xsd', rows, right, preferred_element_type=BF16)

    io = place(chunk, S5_GROUP)
    st = place(2, S5_STATE)
    v8 = block_diag(v, io, st)
    m8 = block_diag(m, io, io)
    w8 = block_diag(wh, st, io)
    ar = a1.reshape(nx, S5_LANE_GROUPS, 2, S5_STATE)[:, :, 0].reshape(1, ng * S5_STATE)
    ai = a2.reshape(nx, S5_LANE_GROUPS, 2, S5_STATE)[:, :, 1].reshape(1, ng * S5_STATE)
    return v8, m8, w8, ar, ai


def _s5_nat(u, h0_re, h0_im, mats, chunk, bt):
    b, t_len, _ = u.shape
    n_chunks = t_len // chunk
    nbc = b // bt
    nx = D_SSM // LANES
    half = S5_LANE_GROUPS * S5_STATE
    gw = 2 * half
    lk = chunk * LANES
    rows = bt * n_chunks
    v8, m8, w8, ar, ai = _s5_nat_mats(mats, chunk)
    planar = lambda a: a.reshape(nbc, bt, nx, 1, half)
    h0 = jnp.concatenate([planar(h0_re), planar(h0_im)], axis=3).reshape(nbc, bt, nx * gw)
    y, ht = pl.pallas_call(
        functools.partial(_s5_nat_kernel, chunk=chunk, n_chunks=n_chunks, bt=bt),
        grid=(nbc, nx),
        in_specs=[pl.BlockSpec((bt, t_len, LANES), lambda i, j: (i, 0, j)),
                  pl.BlockSpec((1, bt, gw), lambda i, j: (i, 0, j)),
                  pl.BlockSpec((1, lk, gw), lambda i, j: (j, 0, 0)),
                  pl.BlockSpec((1, lk, lk), lambda i, j: (j, 0, 0)),
                  pl.BlockSpec((1, gw, lk), lambda i, j: (j, 0, 0)),
                  pl.BlockSpec((1, half), lambda i, j: (0, j)),
                  pl.BlockSpec((1, half), lambda i, j: (0, j))],
        out_specs=[pl.BlockSpec((bt, t_len, LANES), lambda i, j: (i, 0, j)),
                   pl.BlockSpec((1, bt, gw), lambda i, j: (i, 0, j))],
        out_shape=[jax.ShapeDtypeStruct((b, t_len, D_SSM), F32),
                   jax.ShapeDtypeStruct((nbc, bt, nx * gw), F32)],
        scratch_shapes=[pltpu.VMEM((rows, lk), BF16), pltpu.VMEM((gw // LANES, rows, LANES), F32)],
        compiler_params=_cparams(("parallel", "parallel")),
        name="s5_scan_nat",
    )(u, h0, v8, m8, w8, ar, ai)
    ht = ht.reshape(b, nx, 2, S5_LANE_GROUPS, S5_STATE)
    return (y, ht[:, :, 0].reshape(b, N_SSM_GROUPS, S5_STATE), ht[:, :, 1].reshape(b, N_SSM_GROUPS, S5_STATE))


def _epilogue_kernel(x_ref, o_ref, y_ref, u_ref, wz_ref, d_ref, glu0_ref, glu1_ref, gb0_ref, gb1_ref,
                     wa_ref, ws_ref, wo_ref, lng_ref, lnb_ref, out_ref, *, alpha):
    x = x_ref[...]
    z = _dot(x.astype(BF16), wz_ref[...])
    za = z[:, :D_ATT]
    zs = z[:, D_ATT:D_ATT + D_SSM]
    ga = z[:, D_ATT + D_SSM:D_ATT + D_SSM + D_MODEL]
    gs = z[:, D_ATT + D_SSM + D_MODEL:]
    p_att = _dot((o_ref[...] * jax.nn.silu(za)).astype(BF16), wa_ref[...])
    y = jax.nn.gelu(y_ref[...] + d_ref[...] * u_ref[...]).astype(BF16)
    s = (_dot(y, glu0_ref[...]) + gb0_ref[...]) * jax.nn.sigmoid(_dot(y, glu1_ref[...]) + gb1_ref[...])
    p_ssm = _dot((s * jax.nn.silu(zs)).astype(BF16), ws_ref[...])
    merged = jax.nn.sigmoid(ga) * p_att + jax.nn.sigmoid(gs) * p_ssm
    r = alpha * x + _dot(merged.astype(BF16), wo_ref[...])
    mu = jnp.mean(r, axis=-1, keepdims=True)
    c = r - mu
    var = jnp.mean(c * c, axis=-1, keepdims=True)
    out_ref[...] = c * lax.rsqrt(var + LN_EPS) * lng_ref[...] + lnb_ref[...]


def _epilogue(x2d, o2d, y2d, u2d, ew, alpha):
    m = x2d.shape[0]
    tm = min(EPILOGUE_ROWS, m)
    names = ("wz", "d", "glu0", "glu1", "gb0", "gb1", "wa", "ws", "wo", "lng", "lnb")
    ws = [ew[n] for n in names]

    def tok(w):
        return pl.BlockSpec((tm, w), lambda i: (i, 0))

    return pl.pallas_call(
        functools.partial(_epilogue_kernel, alpha=alpha),
        grid=(m // tm,),
        in_specs=[tok(D_MODEL), tok(D_ATT), tok(D_SSM), tok(D_SSM)]
        + [pl.BlockSpec(w.shape, lambda i: (0, 0)) for w in ws],
        out_specs=tok(D_MODEL),
        out_shape=jax.ShapeDtypeStruct((m, D_MODEL), F32),
        compiler_params=_cparams(("parallel",)),
        name="epilogue",
    )(x2d, o2d, y2d, u2d, *ws)


def _layer_weights(l, w_in, cmp_w1, cmp_b1, cmp_w2, cmp_b2, cmp_pos, ssm_d, ssm_glu_w, ssm_glu_b,
                   w_att_out, w_ssm_out, w_o, ln_g, ln_b):
    offs = np.concatenate([[0], np.cumsum(IN_WIDTHS)])
    col = lambda i: w_in[l][:, offs[i]:offs[i + 1]]
    perm = _pair_perm()
    gate = jnp.pad(col(7), ((0, 0), (0, GATE_PAD - 3 * N_HEADS)))
    w_seq = jnp.concatenate([col(0)[:, perm], col(1), col(2), col(3), col(4), col(5), col(6), gate, col(9)],
                            axis=1).astype(BF16)
    wz = jnp.concatenate([col(8)[:, perm], col(10), col(11)], axis=1).astype(BF16)

    half = CMP_BLOCK // 2
    eye = jnp.eye(4, dtype=F32)
    eye_g = jnp.eye(N_KV, dtype=F32)
    w1, pos = [], []
    for kv in range(2):
        bd = jnp.einsum('pdh,st->psdth', cmp_w1[l, kv], eye_g)
        bd = bd.reshape(CMP_BLOCK, KV_W, N_KV * CMP_HIDDEN)
        w1.append(jnp.concatenate([bd[:half].reshape(half * KV_W, -1), bd[half:].reshape(half * KV_W, -1)], axis=1))
        pos_g = jnp.broadcast_to(cmp_pos[l, kv][:, None, :], (CMP_BLOCK, N_KV, HEAD_DIM))
        pos.append(jnp.zeros((SUBLANES, half * KV_W), F32)
                   .at[0].set(pos_g[:half].reshape(-1)).at[1].set(pos_g[half:].reshape(-1)))
    w1 = jnp.stack(w1).astype(BF16)
    pos = jnp.stack(pos)
    b1 = jnp.concatenate([cmp_b1[l, 0], cmp_b1[l, 0], cmp_b1[l, 1], cmp_b1[l, 1]]).reshape(1, -1)
    w2s = jnp.stack([cmp_w2[l, 0], cmp_w2[l, 0], cmp_w2[l, 1], cmp_w2[l, 1]])
    w2 = jnp.einsum('shd,st->shtd', w2s, eye).reshape(4 * CMP_HIDDEN, 4 * HEAD_DIM).astype(BF16)
    b2 = jnp.concatenate([cmp_b2[l, 0], cmp_b2[l, 0], cmp_b2[l, 1], cmp_b2[l, 1]]).reshape(1, -1)
    cw = dict(w1=w1, pos=pos, b1=b1, w2=w2, b2=b2)

    ew = dict(wz=wz, d=ssm_d[l].reshape(1, D_SSM),
              glu0=ssm_glu_w[l, 0].astype(BF16), glu1=ssm_glu_w[l, 1].astype(BF16),
              gb0=ssm_glu_b[l, 0].reshape(1, -1), gb1=ssm_glu_b[l, 1].reshape(1, -1),
              wa=w_att_out[l][perm].astype(BF16), ws=w_ssm_out[l].astype(BF16), wo=w_o[l].astype(BF16),
              lng=ln_g[l].reshape(1, -1), lnb=ln_b[l].reshape(1, -1))
    return w_seq, cw, ew


def _prompt_layer(h_p, lw, s5m, tables, alpha):
    b, seq, _ = h_p.shape
    w_seq, cw, ew = lw
    x2d = h_p.reshape(b * seq, D_MODEL)
    q, kvc, kvs, kvw, kvct, kvst, kvwt, gates, u = _inproj(x2d, w_seq, seq)
    r3 = lambda a: a.reshape(b, seq, a.shape[-1])
    o_att = _nsa_prompt(r3(q), r3(kvc), r3(kvs), r3(kvw), kvst, kvwt, r3(gates), cw, tables)
    zero = jnp.zeros((b, N_SSM_GROUPS, S5_STATE), F32)
    y, hr, hi = _s5_nat(r3(u), zero, zero, s5m, S5_CHUNK, bt=min(S5_BATCH_ROWS, b))
    out = _epilogue(x2d, o_att.reshape(b * seq, D_ATT), y.reshape(b * seq, D_SSM), u, ew, alpha)
    return out.reshape(b, seq, D_MODEL), kvct, kvst, kvwt, hr, hi


def _gather_step(page_dma):
    b, c = pl.program_id(0), pl.program_id(1)
    n_c = pl.num_programs(1)
    step = b * n_c + c
    total = pl.num_programs(0) * n_c
    slot = step % 2

    @pl.when(step == 0)
    def _first():
        for j in range(PAGES_PER_STEP):
            page_dma(b, c, j, slot).start()

    @pl.when(step + 1 < total)
    def _prefetch():
        nxt = step + 1
        for j in range(PAGES_PER_STEP):
            page_dma(nxt // n_c, nxt % n_c, j, 1 - slot).start()

    for j in range(PAGES_PER_STEP):
        page_dma(b, c, j, slot).wait()
    return slot


def _sample_queries(q):
    t = q.shape[0]
    lane = lax.broadcasted_iota(jnp.int32, (t, LANES), 1)
    parts = []
    for g in range(N_KV):
        gmask = (lane < HEAD_DIM) if g == 0 else (lane >= HEAD_DIM)
        parts += [jnp.where(gmask, q[:, r * LANES:(r + 1) * LANES], 0.0) for r in range(HPG)]
    return jnp.concatenate(parts, axis=0).astype(BF16)


def _softmax_rows(s):
    m = jnp.max(s, axis=1, keepdims=True)
    valid = m > 0.5 * NEG
    e = jnp.exp2(s - m)
    return jnp.where(valid, e / jnp.sum(e, axis=1, keepdims=True), 0.0)


def _cmp_sample_kernel(pt_ref, cache_hbm, q_ref, perm_ref, w1_ref, pos_ref, b1_ref, w2_ref, b2_ref, biasc_ref,
                       mimp_ref, oc_ref, sel_ref, buf, sem, x_scr, hf_scr, hs_scr,
                       *, layer, t_len, past_len, n_blk):
    c = pl.program_id(1)
    n_c = pl.num_programs(1)
    nh = 4 * CMP_HIDDEN
    cpp = buf.shape[3] // CMP_STRIDE
    rows = PAGES_PER_STEP * cpp

    def page_dma(b, cc, j, slot):
        page = pt_ref[b, cc * PAGES_PER_STEP + j]
        return pltpu.make_async_copy(cache_hbm.at[layer, page], buf.at[slot, j], sem.at[slot])

    slot = _gather_step(page_dma)

    def regroup(j, carry):
        xp = _dot_nt(perm_ref[...], buf[slot, j].astype(BF16))
        r0 = pl.multiple_of(j * cpp, cpp)
        for p in range(CMP_STRIDE):
            for kv in range(2):
                x_scr[kv, pl.ds(r0, cpp), p * KV_W:(p + 1) * KV_W] = xp[p * cpp:(p + 1) * cpp, kv * KV_W:(kv + 1) * KV_W]
        return carry

    lax.fori_loop(0, PAGES_PER_STEP, regroup, 0, unroll=8)
    hf, hs = _compress_hidden(x_scr[0].astype(BF16), x_scr[1].astype(BF16), w1_ref)
    r0 = pl.multiple_of(c * rows, rows)
    hf_scr[pl.ds(r0, rows), :] = hf
    hs_scr[pl.ds(r0, rows), :] = hs

    @pl.when(c == n_c - 1)
    def _finish():
        n_ch = hf_scr.shape[0]
        c1 = _compress_const(pos_ref, w1_ref, b1_ref)
        hs_next = pltpu.roll(hs_scr[...], n_ch - 1, 0)
        kv = _compress_out(hf_scr[...], hs_next, c1, w2_ref, b2_ref)
        q64 = _sample_queries(q_ref[0])
        p = _softmax_rows(_dot_nt(q64, kv[:, :KV_W].astype(BF16)) + biasc_ref[...])
        oc_ref[0] = _dot(p.astype(BF16), kv[:, KV_W:].astype(BF16))
        psum = []
        for g in range(N_KV):
            acc = p[g * HPG * t_len:g * HPG * t_len + t_len]
            for r in range(1, HPG):
                acc = acc + p[(g * HPG + r) * t_len:(g * HPG + r + 1) * t_len]
            psum.append(acc)
        psum = jnp.concatenate(psum, axis=0)
        p1, p2, p3 = _split3(psum)
        imp = _dot(p1, mimp_ref[...]) + _dot(p2, mimp_ref[...]) + _dot(p3, mimp_ref[...])
        shape = imp.shape
        bj = lax.broadcasted_iota(jnp.int32, shape, 1)
        tpos = past_len + lax.broadcasted_iota(jnp.int32, shape, 0) % t_len
        cur = tpos // SLC_BLOCK
        forced = (bj == 0) | (bj == cur) | (bj == cur - 1)
        visible = bj * SLC_BLOCK <= tpos
        score = jnp.where(forced, -NEG, jnp.where(visible, imp, NEG))
        score = jnp.where(bj < n_blk, score, 2.0 * NEG)
        sel = jnp.zeros(shape, F32)
        for _ in range(min(N_SEL, n_blk)):
            best = jnp.max(score, axis=1, keepdims=True)
            first = jnp.min(jnp.where(score == best, bj, shape[1]), axis=1, keepdims=True)
            hit = bj == first
            sel = jnp.where(hit, 1.0, sel)
            score = jnp.where(hit, 3.0 * NEG, score)
        sel_ref[0] = sel


def _slc_sample_kernel(pt_ref, cache_hbm, q_ref, sel_ref, knew_ref, bias_ref, c31_ref, biasn_ref, eblk_ref,
                       os_ref, buf, sem, m_scr, l_scr, acc_scr, *, layer):
    c = pl.program_id(1)
    n_c = pl.num_programs(1)
    page = buf.shape[2] // PAGES_PER_STEP

    def page_dma(b, cc, j, slot):
        pg = pt_ref[b, cc * PAGES_PER_STEP + j]
        return pltpu.make_async_copy(cache_hbm.at[layer, pg], buf.at[slot, :, pl.ds(j * page, page)], sem.at[slot])

    slot = _gather_step(page_dma)
    q64 = _sample_queries(q_ref[0])

    @pl.when(c == 0)
    def _init():
        m_scr[...] = jnp.full(m_scr.shape, NEG, F32)
        l_scr[...] = jnp.zeros(l_scr.shape, F32)
        acc_scr[...] = jnp.zeros(acc_scr.shape, F32)

    def update(s, vt_bf):
        m_old = m_scr[...]
        m_new = jnp.maximum(m_old, jnp.max(s, axis=1, keepdims=True))
        alpha = jnp.exp2(m_old - m_new)
        e = jnp.exp2(s - m_new)
        l_scr[...] = alpha * l_scr[...] + jnp.sum(e, axis=1, keepdims=True)
        acc_scr[...] = alpha * acc_scr[...] + _dot_nt(e.astype(BF16), vt_bf)
        m_scr[...] = m_new

    neg = (_dot(sel_ref[0, 0].astype(BF16), eblk_ref[...]) - 1.0) * (-NEG)
    s = _dot(q64, buf[slot, 0:KV_W, :].astype(BF16)) + neg

    @pl.when(c < n_c - 1)
    def _far():
        update(s + c31_ref[...], buf[slot, KV_W:KV2, :].astype(BF16))

    @pl.when(c == n_c - 1)
    def _last():
        update(s + bias_ref[...], buf[slot, KV_W:KV2, :].astype(BF16))
        kn = knew_ref[0]
        update(_dot(q64, kn[0:KV_W, :].astype(BF16)) + biasn_ref[...], kn[KV_W:KV2, :].astype(BF16))
        valid = m_scr[...] > 0.5 * NEG
        os_ref[0] = jnp.where(valid, acc_scr[...] / jnp.where(valid, l_scr[...], 1.0), 0.0)


def _win_sample_kernel(q_ref, kvw_ref, biasw_ref, g_ref, oc_ref, os_ref, o_ref, *, t_len):
    q64 = _sample_queries(q_ref[0])
    kw = kvw_ref[0]
    p = _softmax_rows(_dot(q64, kw[0:KV_W, :].astype(BF16)) + biasw_ref[...])
    ow = _dot_nt(p.astype(BF16), kw[KV_W:KV2, :].astype(BF16))
    gates = g_ref[0]
    oc, osel = oc_ref[0], os_ref[0]
    lane = lax.broadcasted_iota(jnp.int32, (t_len, LANES), 1)

    def head(g, r):
        h = g * HPG + r
        sl = slice(h * t_len, (h + 1) * t_len)
        return (gates[:, 3 * h:3 * h + 1] * oc[sl] + gates[:, 3 * h + 1:3 * h + 2] * osel[sl]
                + gates[:, 3 * h + 2:3 * h + 3] * ow[sl])

    for r in range(HPG):
        o_ref[0, :, r * LANES:(r + 1) * LANES] = jnp.where(lane < HEAD_DIM, head(0, r), head(1, r))


def _nsa_sample(layer, q, gates, kvs_new_t, all_w_t, cmp_t, slc_t, page_table, cw, tables):
    b, t_len, _ = q.shape
    n_pages = page_table.shape[1]
    page = cmp_t.shape[3]
    past_len = n_pages * page
    n_c = n_pages // PAGES_PER_STEP
    n_ch = past_len // CMP_STRIDE
    n_blk = -(-(past_len + t_len) // SLC_BLOCK)
    blk_pad = -(-n_blk // LANES) * LANES
    rows = PAGES_PER_STEP * page
    blk_step = rows // SLC_BLOCK
    nq = N_HEADS * t_len
    assert past_len % SLC_BLOCK == 0 and t_len <= SLC_BLOCK and (past_len + t_len) // CMP_STRIDE == n_ch
    biasc, biass, c31, biasn, biasw = tables
    c31 = c31[:, :1]
    mimp = jnp.asarray(_imp_matrix(n_ch, blk_pad, (past_len + t_len) // CMP_STRIDE - 1), BF16)
    cpp = page // CMP_STRIDE
    pm = np.zeros((page, page), np.float32)
    for p in range(CMP_STRIDE):
        for cc in range(cpp):
            pm[p * cpp + cc, cc * CMP_STRIDE + p] = 1.0
    pm = jnp.asarray(pm, BF16)

    def full(a):
        nd = a.ndim
        return pl.BlockSpec(a.shape, lambda i, j, pt, nd=nd: (0,) * nd)

    q_spec = pl.BlockSpec((1, t_len, D_ATT), lambda i, j, pt: (i, 0, 0))
    o_spec = pl.BlockSpec((1, nq, KV_W), lambda i, j, pt: (i, 0, 0))
    any_spec = pl.BlockSpec(memory_space=pl.ANY)
    oc, sel = pl.pallas_call(
        functools.partial(_cmp_sample_kernel, layer=layer, t_len=t_len, past_len=past_len, n_blk=n_blk),
        grid_spec=pltpu.PrefetchScalarGridSpec(
            num_scalar_prefetch=1, grid=(b, n_c),
            in_specs=[any_spec, q_spec, full(pm), full(cw["w1"]), full(cw["pos"]), full(cw["b1"]), full(cw["w2"]),
                      full(cw["b2"]), full(biasc), full(mimp)],
            out_specs=[o_spec, pl.BlockSpec((1, N_KV * t_len, blk_pad), lambda i, j, pt: (i, 0, 0))],
            scratch_shapes=[pltpu.VMEM((2, PAGES_PER_STEP, KV2, page), F32),
                            pltpu.SemaphoreType.DMA((2,)),
                            pltpu.VMEM((2, PAGES_PER_STEP * cpp, CMP_STRIDE * KV_W), F32),
                            pltpu.VMEM((n_ch, 4 * CMP_HIDDEN), F32), pltpu.VMEM((n_ch, 4 * CMP_HIDDEN), F32)]),
        out_shape=[jax.ShapeDtypeStruct((b, nq, KV_W), F32),
                   jax.ShapeDtypeStruct((b, N_KV * t_len, blk_pad), F32)],
        compiler_params=_cparams(("arbitrary", "arbitrary")),
        name="nsa_sample_cmp",
    )(page_table, cmp_t, q, pm, cw["w1"], cw["pos"], cw["b1"], cw["w2"], cw["b2"], biasc, mimp)

    sel_c = sel[:, :, :n_c * blk_step].reshape(b, N_KV, 1, t_len, n_c, blk_step)
    sel_c = jnp.broadcast_to(sel_c, (b, N_KV, HPG, t_len, n_c, blk_step))
    sel_c = sel_c.transpose(0, 4, 1, 2, 3, 5).reshape(b, n_c, nq, blk_step)
    sel_c = jnp.pad(sel_c, ((0, 0), (0, 0), (0, 0), (0, LANES - blk_step)))
    eblk = jnp.asarray(np.arange(LANES)[:, None] == (np.arange(rows)[None, :] // SLC_BLOCK), BF16)
    o_s = pl.pallas_call(
        functools.partial(_slc_sample_kernel, layer=layer),
        grid_spec=pltpu.PrefetchScalarGridSpec(
            num_scalar_prefetch=1, grid=(b, n_c),
            in_specs=[any_spec, q_spec,
                      pl.BlockSpec((1, 1, nq, LANES), lambda i, j, pt: (i, j, 0, 0)),
                      pl.BlockSpec((1, KV2, LANES), lambda i, j, pt: (i, 0, 0)),
                      full(biass), full(c31), full(biasn), full(eblk)],
            out_specs=o_spec,
            scratch_shapes=[pltpu.VMEM((2, KV2, rows), F32),
                            pltpu.SemaphoreType.DMA((2,)),
                            pltpu.VMEM((nq, 1), F32), pltpu.VMEM((nq, 1), F32), pltpu.VMEM((nq, KV_W), F32)]),
        out_shape=jax.ShapeDtypeStruct((b, nq, KV_W), F32),
        compiler_params=_cparams(("arbitrary", "arbitrary")),
        name="nsa_sample_slc",
    )(page_table, slc_t, q, sel_c, kvs_new_t, biass, c31, biasn, eblk)

    w_pad = all_w_t.shape[2]
    return pl.pallas_call(
        functools.partial(_win_sample_kernel, t_len=t_len),
        grid=(b,),
        in_specs=[pl.BlockSpec((1, t_len, D_ATT), lambda i: (i, 0, 0)),
                  pl.BlockSpec((1, KV2, w_pad), lambda i: (i, 0, 0)),
                  pl.BlockSpec(biasw.shape, lambda i: (0, 0)),
                  pl.BlockSpec((1, t_len, GATE_PAD), lambda i: (i, 0, 0)),
                  pl.BlockSpec((1, nq, KV_W), lambda i: (i, 0, 0)),
                  pl.BlockSpec((1, nq, KV_W), lambda i: (i, 0, 0))],
        out_specs=pl.BlockSpec((1, t_len, D_ATT), lambda i: (i, 0, 0)),
        out_shape=jax.ShapeDtypeStruct((b, t_len, D_ATT), F32),
        compiler_params=_cparams(("parallel",)),
        name="nsa_sample_win",
    )(q, all_w_t, biasw, gates, oc, o_s)


def _sample_layer(layer, h_s, lw, s5m, tables, alpha, cmp_t, slc_t, win_t, h0_re, h0_im, page_table, w_pad):
    b, t_len, _ = h_s.shape
    w_seq, cw, ew = lw
    m = b * t_len
    x2d = h_s.reshape(m, D_MODEL)
    q, kvc, kvs, kvw, kvct, kvst, kvwt, gates, u = _inproj(x2d, w_seq, m)
    r3 = lambda a: a.reshape(b, t_len, a.shape[-1])
    per_row = lambda a: a.reshape(KV2, b, t_len).transpose(1, 0, 2)
    w_buf = win_t.shape[2]
    all_w_t = jnp.concatenate([win_t, per_row(kvwt)], axis=2)
    all_w_pad = jnp.pad(all_w_t, ((0, 0), (0, 0), (0, w_pad - (w_buf + t_len))))
    kvs_new_t = jnp.pad(per_row(kvst), ((0, 0), (0, 0), (0, LANES - t_len)))
    o_att = _nsa_sample(layer, r3(q), r3(gates), kvs_new_t, all_w_pad, cmp_t, slc_t, page_table, cw, tables)
    y, hr, hi = _s5(r3(u), h0_re, h0_im, s5m, t_len, bt=b)
    out = _epilogue(x2d, o_att.reshape(m, D_ATT), y.reshape(m, D_SSM), u, ew, alpha)
    return out.reshape(b, t_len, D_MODEL), r3(kvc), r3(kvs), all_w_t[:, :, t_len:], hr, hi


def _feature_major(a):
    nd = a.ndim
    perm = tuple(range(nd - 4)) + (nd - 3, nd - 2, nd - 1, nd - 4)
    t = a.transpose(perm)
    return t.reshape(t.shape[:nd - 4] + (KV2, a.shape[nd - 4]))


def _row_major(a):
    nd = a.ndim
    t = a.reshape(a.shape[:nd - 2] + (2, N_KV, HEAD_DIM, a.shape[-1]))
    perm = tuple(range(nd - 2)) + (nd + 1, nd - 2, nd - 1, nd)
    return t.transpose(perm)


def kernel(x_prompt, x_sample, cache_kv_cmp, cache_kv_slc, state_win_kv, state_ssm_re, state_ssm_im,
           page_table, rel_bias, w_in, cmp_w1, cmp_b1, cmp_w2, cmp_b2, cmp_pos,
           ssm_a_re, ssm_a_im, ssm_log_dt, ssm_b_re, ssm_b_im, ssm_c_re, ssm_c_im, ssm_d,
           ssm_glu_w, ssm_glu_b, w_att_out, w_ssm_out, w_o, ln_g, ln_b):
    depth = w_in.shape[0]
    alpha = (2 * depth) ** 0.25
    bp, seq = x_prompt.shape[:2]
    bs, t_len = x_sample.shape[:2]
    page = cache_kv_cmp.shape[2]
    past_len = page_table.shape[1] * page
    w_buf = state_win_kv.shape[2]
    w_pad = -(-(w_buf + t_len) // LANES) * LANES
    n_win = min(WINDOW, seq)
    p_tables = _prompt_tables(rel_bias, seq)
    s_tables = _sample_tables(rel_bias, past_len, t_len, w_buf, w_pad, PAGES_PER_STEP * page)
    cmp_t, slc_t, win_t = _feature_major(cache_kv_cmp), _feature_major(cache_kv_slc), _feature_major(state_win_kv)
    h_p, h_s = x_prompt, x_sample
    outs_p, outs_s = [], []
    for l in range(depth):
        lw = _layer_weights(l, w_in, cmp_w1, cmp_b1, cmp_w2, cmp_b2, cmp_pos, ssm_d, ssm_glu_w, ssm_glu_b,
                            w_att_out, w_ssm_out, w_o, ln_g, ln_b)
        ssm = (ssm_a_re[l], ssm_a_im[l], ssm_log_dt[l], ssm_b_re[l], ssm_b_im[l], ssm_c_re[l], ssm_c_im[l])
        h_p, kvct, kvst, kvwt, hr, hi = _prompt_layer(h_p, lw, _s5_mats(*ssm, S5_CHUNK), p_tables, alpha)
        outs_p.append((kvct, kvst, kvwt[:, :, seq - n_win:], hr, hi))
        h_s, kvc, kvs, win, hr, hi = _sample_layer(
            l, h_s, lw, _s5_mats(*ssm, t_len), s_tables, alpha, cmp_t, slc_t, win_t[l],
            state_ssm_re[l], state_ssm_im[l], page_table, w_pad)
        outs_s.append((kvc.reshape(bs, t_len, 2, N_KV, HEAD_DIM), kvs.reshape(bs, t_len, 2, N_KV, HEAD_DIM),
                       win, hr, hi))
    stack = lambda outs, i: jnp.stack([o[i] for o in outs])
    return ((h_p, h_s)
            + tuple(_row_major(stack(outs_p, i)) for i in range(3)) + (stack(outs_p, 3), stack(outs_p, 4))
            + (stack(outs_s, 0), stack(outs_s, 1), _row_major(stack(outs_s, 2)), stack(outs_s, 3), stack(outs_s, 4)))
```

```python
import functools
import math

import numpy as np
import jax
import jax.numpy as jnp
from jax import lax
from jax.experimental import pallas as pl
from jax.experimental.pallas import tpu as pltpu

F32 = jnp.float32
BF16 = jnp.bfloat16

D_MODEL = 1024
N_HEADS = 8
HEAD_DIM = 64
N_KV = 2
HPG = N_HEADS // N_KV
D_ATT = N_HEADS * HEAD_DIM
KV_W = N_KV * HEAD_DIM
CMP_BLOCK = 32
CMP_STRIDE = 16
CMP_HIDDEN = 128
SLC_BLOCK = 64
N_SEL = 16
WINDOW = 512
S5_GROUP = 16
S5_STATE = 64
D_SSM = 512
N_SSM_GROUPS = D_SSM // S5_GROUP
N_BUCKETS = 32
MAX_DISTANCE = 128
LN_EPS = 1e-5
IN_WIDTHS = (D_ATT, KV_W, KV_W, KV_W, KV_W, KV_W, KV_W, 3 * N_HEADS, D_ATT, D_SSM, D_SSM, 2 * D_MODEL)

LANES = 128
SUBLANES = 8
VMEM_LIMIT = 56 * 1024 * 1024
NEG = -1e30
LOG2E = 1.4426950408889634
VROWS = HEAD_DIM + 16
TQ = 128
TK = 128
FAR_TILES = 8
INPROJ_ROWS = 1024
EPILOGUE_ROWS = 512
KV2 = 2 * KV_W
GATE_PAD = LANES
S5_CHUNK = 8
S5_LANE_GROUPS = LANES // S5_GROUP
S5_BATCH_ROWS = 4
SEQ_COLS = D_ATT + 3 * KV2 + GATE_PAD + D_SSM
QCOLS = N_HEADS * TQ
GCOLS = HPG * TQ
PAGES_PER_STEP = 64


def _cparams(sem):
    return pltpu.CompilerParams(dimension_semantics=sem, vmem_limit_bytes=VMEM_LIMIT)


def _dot(a, b):
    return jnp.dot(a, b, preferred_element_type=F32)


def _dot_nt(a, b):
    return lax.dot_general(a, b, (((1,), (1,)), ((), ())), preferred_element_type=F32)


def _split3(x):
    h1 = x.astype(BF16)
    r1 = x - h1.astype(F32)
    h2 = r1.astype(BF16)
    h3 = (r1 - h2.astype(F32)).astype(BF16)
    return h1, h2, h3


def _bucket_np(dist):
    n = np.maximum(dist, 0)
    max_exact = N_BUCKETS // 2
    nf = np.maximum(n, 1).astype(np.float32)
    val = (np.log(nf / np.float32(max_exact)) / np.float32(math.log(MAX_DISTANCE / max_exact))
           * np.float32(N_BUCKETS - max_exact))
    large = max_exact + val.astype(np.int32)
    return np.where(n < max_exact, n, np.minimum(large, N_BUCKETS - 1)).astype(np.int32)


def _pair_perm():
    cols = []
    for r in range(HPG):
        cols += list(range(r * HEAD_DIM, (r + 1) * HEAD_DIM))
        cols += list(range((HPG + r) * HEAD_DIM, (HPG + r + 1) * HEAD_DIM))
    return np.asarray(cols, np.int32)


def _imp_matrix(n_cmp_pad, n_blk_pad, n_cmp):
    ratio = SLC_BLOCK // CMP_STRIDE
    n = np.arange(n_cmp_pad)[:, None]
    j = np.arange(n_blk_pad)[None, :]
    m = ((n >= ratio * j - 1) & (n <= ratio * j + ratio - 2)).astype(np.float32)
    m += ((n >= ratio * j) & (n <= ratio * j + ratio - 1)).astype(np.float32)
    m *= (n < n_cmp)
    return m


def _bucket_starts():
    b = _bucket_np(np.arange(2 * MAX_DISTANCE))
    assert (np.diff(b) >= 0).all() and b.max() == N_BUCKETS - 1
    starts = [int(np.argmax(b >= k)) for k in range(1, N_BUCKETS)]
    assert all(b[s] == k for k, s in zip(range(1, N_BUCKETS), starts))
    return starts


def _bias_of_dist(d, bucket_value):
    val = bucket_value(0)
    for k, start in zip(range(1, N_BUCKETS), _bucket_starts()):
        val = jnp.where(d >= start, bucket_value(k), val)
    return val


def _prompt_tables_kernel(rb_ref, tn_ref, t4_ref, cm_ref, *, row_shift):
    bias = lambda d: _bias_of_dist(d, lambda k: rb_ref[k:k + 1, :])
    last = rb_ref[N_BUCKETS - 1:N_BUCKETS, :]
    shape = (TK, QCOLS)
    d0 = lax.broadcasted_iota(jnp.int32, shape, 1) % TQ - lax.broadcasted_iota(jnp.int32, shape, 0)
    tn_ref[0:TK, :] = (bias(TK + d0) - last) * LOG2E
    tn_ref[TK:2 * TK, :] = jnp.where(d0 >= 0, (bias(d0) - last) * LOG2E, NEG)
    t4_ref[...] = jnp.where(d0 <= 0, 0.0, NEG)
    shape = cm_ref.shape
    nn = lax.broadcasted_iota(jnp.int32, shape, 0) - row_shift
    dc = lax.broadcasted_iota(jnp.int32, shape, 1) % TQ - (nn * CMP_STRIDE + CMP_BLOCK - 1)
    cm_ref[...] = jnp.where(dc >= 0, bias(dc) * LOG2E, NEG)


def _prompt_tables(rel_bias, seq):
    assert WINDOW == 4 * TK and TQ == TK and _bucket_starts()[-1] <= TK
    n_qt = seq // TQ
    n_ch = seq // CMP_STRIDE
    row_shift = (n_qt - 1) * (TQ // CMP_STRIDE)
    tile = jax.ShapeDtypeStruct((TK, QCOLS), F32)
    return pl.pallas_call(
        functools.partial(_prompt_tables_kernel, row_shift=row_shift),
        out_shape=[jax.ShapeDtypeStruct((2 * TK, QCOLS), F32), tile,
                   jax.ShapeDtypeStruct((n_ch + row_shift, QCOLS), F32)],
        compiler_params=pltpu.CompilerParams(vmem_limit_bytes=VMEM_LIMIT),
        name="prompt_tables",
    )(jnp.repeat(rel_bias, TQ, axis=1))


def _sample_tables_kernel(rb_ref, bc_ref, bs_ref, c31_ref, bn_ref, bw_ref, *, past_len, t_len, w_buf):
    bias = lambda d: _bias_of_dist(d, lambda k: rb_ref[:, k:k + 1]) * LOG2E
    n_tok = past_len + t_len

    def grid(ref):
        qpos = past_len + lax.broadcasted_iota(jnp.int32, ref.shape, 0) % t_len
        return qpos, lax.broadcasted_iota(jnp.int32, ref.shape, 1)

    qpos, nn = grid(bc_ref)
    dc = qpos - (nn * CMP_STRIDE + CMP_BLOCK - 1)
    bc_ref[...] = jnp.where((dc >= 0) & (nn < n_tok // CMP_STRIDE - 1), bias(dc), NEG)
    qpos, ll = grid(bs_ref)
    bs_ref[...] = bias(qpos - (past_len - bs_ref.shape[1] + ll))
    c31_ref[...] = jnp.broadcast_to(rb_ref[:, N_BUCKETS - 1:N_BUCKETS] * LOG2E, c31_ref.shape)
    qpos, ll = grid(bn_ref)
    dn = qpos - (past_len + ll)
    bn_ref[...] = jnp.where((dn >= 0) & (ll < t_len), bias(dn), NEG)
    qpos, ii = grid(bw_ref)
    w_pos = n_tok - (w_buf + t_len) + ii
    dw = qpos - w_pos
    bw_ref[...] = jnp.where((dw >= 0) & (dw <= WINDOW) & (w_pos >= 0) & (ii < w_buf + t_len), bias(dw), NEG)


def _sample_tables(rel_bias, past_len, t_len, w_buf, w_pad, last_rows):
    assert _bucket_starts()[-1] <= last_rows
    nq = N_HEADS * t_len
    shp = lambda n: jax.ShapeDtypeStruct((nq, n), F32)
    return pl.pallas_call(
        functools.partial(_sample_tables_kernel, past_len=past_len, t_len=t_len, w_buf=w_buf),
        out_shape=[shp(past_len // CMP_STRIDE), shp(last_rows), shp(LANES), shp(LANES), shp(w_pad)],
        compiler_params=pltpu.CompilerParams(vmem_limit_bytes=VMEM_LIMIT),
        name="sample_tables",
    )(jnp.repeat(rel_bias.T, t_len, axis=0))


def _inproj_kernel(x_ref, w_ref, q_ref, kvc_ref, kvs_ref, kvw_ref, kvct_ref, kvst_ref, kvwt_ref, g_ref, u_ref):
    h = _dot(x_ref[...].astype(BF16), w_ref[...])
    o = 0
    q_ref[...] = h[:, o:o + D_ATT] * (HEAD_DIM ** -0.5 * LOG2E)
    o += D_ATT
    for row_ref, t_ref in ((kvc_ref, kvct_ref), (kvs_ref, kvst_ref), (kvw_ref, kvwt_ref)):
        kv = h[:, o:o + KV2]
        row_ref[...] = kv
        t_ref[0] = kv.T
        o += KV2
    g_ref[...] = jax.nn.sigmoid(h[:, o:o + GATE_PAD])
    o += GATE_PAD
    u_ref[...] = h[:, o:o + D_SSM]


def _inproj(x2d, w_seq, rows_per_seq):
    m = x2d.shape[0]
    tm = min(INPROJ_ROWS, m)
    tps = rows_per_seq // tm
    tok = lambda w: pl.BlockSpec((tm, w), lambda i: (i, 0))
    ft = pl.BlockSpec((1, KV2, tm), lambda i: (i // tps, 0, i % tps))
    tok_shape = lambda w: jax.ShapeDtypeStruct((m, w), F32)
    ft_shape = jax.ShapeDtypeStruct((m // rows_per_seq, KV2, rows_per_seq), F32)
    return pl.pallas_call(
        _inproj_kernel,
        grid=(m // tm,),
        in_specs=[tok(D_MODEL), pl.BlockSpec((D_MODEL, SEQ_COLS), lambda i: (0, 0))],
        out_specs=[tok(D_ATT), tok(KV2), tok(KV2), tok(KV2), ft, ft, ft, tok(GATE_PAD), tok(D_SSM)],
        out_shape=[tok_shape(D_ATT), tok_shape(KV2), tok_shape(KV2), tok_shape(KV2), ft_shape, ft_shape, ft_shape,
                   tok_shape(GATE_PAD), tok_shape(D_SSM)],
        compiler_params=_cparams(("parallel",)),
        name="inproj",
    )(x2d, w_seq)


def _compress_out(hf, hs_next, c1, w2_ref, b2_ref):
    h = hf + hs_next + c1
    return _dot(jax.nn.silu(h).astype(BF16), w2_ref[...]) + b2_ref[...]


def _compress_hidden(xk, xv, w1_ref):
    hk = _dot(xk, w1_ref[0])
    hv = _dot(xv, w1_ref[1])
    nh = N_KV * CMP_HIDDEN
    return (jnp.concatenate([hk[:, :nh], hv[:, :nh]], axis=1), jnp.concatenate([hk[:, nh:], hv[:, nh:]], axis=1))


def _compress_const(pos_ref, w1_ref, b1_ref):
    hf, hs = _compress_hidden(pos_ref[0].astype(BF16), pos_ref[1].astype(BF16), w1_ref)
    return hf[0:1] + hs[1:2] + b1_ref[...]


def _value_rows(v_scr, g, vt):
    n = v_scr.shape[2]
    pad = lax.broadcasted_iota(jnp.int32, (VROWS - HEAD_DIM, n), 0) == 0
    v_scr[g, 0:HEAD_DIM, :] = vt.astype(BF16)
    v_scr[g, HEAD_DIM:VROWS, :] = jnp.where(pad, 1.0, 0.0).astype(BF16)


def _flash_init(m_ref, acc_ref):
    m_ref[...] = jnp.full(m_ref.shape, NEG, F32)
    acc_ref[...] = jnp.zeros(acc_ref.shape, F32)


def _flash_block(k_scr, v_scr, qb_scr, k0, tk, head_ref, tail_ref, neg_scr, m_scr, acc_scr):
    st = _dot_nt(k_scr[pl.ds(k0, tk), :], qb_scr[...])
    if tail_ref is not None:
        nt = min(tail_ref.shape[0], tk)
        tail = st[tk - nt:] + tail_ref[tail_ref.shape[0] - nt:, :]
        st = tail if nt == tk else jnp.concatenate([st[:tk - nt], tail], axis=0)
    if head_ref is not None:
        nh = head_ref.shape[0]
        st = jnp.concatenate([st[:nh] + head_ref[...], st[nh:]], axis=0)
    if neg_scr is not None:
        ng = neg_scr[pl.ds(k0, tk), :]
        st = st + jnp.concatenate([ng[:, :TQ]] * HPG + [ng[:, TQ:]] * HPG, axis=1)
    if m_scr is None:
        return st
    _flash_finish(st, v_scr, k0, tk, m_scr, acc_scr)


def _flash_finish(st, v_scr, k0, tk, m_scr, acc_scr):
    m_old = m_scr[...]
    m_new = jnp.maximum(m_old, jnp.max(st, axis=0, keepdims=True))
    e = jnp.exp2(st - m_new).astype(BF16)
    pv = jnp.concatenate([_dot(v_scr[g, :, pl.ds(k0, tk)], e[:, g * GCOLS:(g + 1) * GCOLS])
                          for g in range(N_KV)], axis=1)
    acc_scr[...] = jnp.exp2(m_old - m_new) * acc_scr[...] + pv
    m_scr[...] = m_new


def _flash_add(tot_scr, m_scr, acc_scr, gate):
    valid = m_scr[...] > 0.5 * NEG
    coef = jnp.where(valid, 1.0 / jnp.where(valid, acc_scr[HEAD_DIM:HEAD_DIM + 1, :], 1.0), 0.0)
    tot_scr[...] = tot_scr[...] + acc_scr[0:HEAD_DIM, :] * (coef * gate)


def _nsa_prompt_kernel(q_ref, kck_ref, kcv_ref, ks_ref, kw_ref, vst_ref, vwt_ref, g_ref,
                       w1_ref, pos_ref, b1_ref, w2_ref, b2_ref,
                       cmaster_ref, tn_ref, t4_ref, mimp_ref, eblk_ref,
                       o_ref,
                       kc_scr, vc_scr, ks_scr, vs_scr, kw_scr, vw_scr, qb_scr, neg_scr, m_scr, acc_scr, tot_scr,
                       m2_scr, acc2_scr):
    qt = pl.program_id(1)
    n_qt = pl.num_programs(1)
    seq = ks_ref.shape[1]
    n_ch = seq // CMP_STRIDE
    n_blk = seq // SLC_BLOCK
    nh = 4 * CMP_HIDDEN

    @pl.when(qt == 0)
    def _per_batch():
        flat = lambda ref: jnp.concatenate(
            [ref[0, pl.ds(p, n_ch, stride=CMP_STRIDE), :] for p in range(CMP_STRIDE)], axis=1).astype(BF16)
        hf, hs = _compress_hidden(flat(kck_ref), flat(kcv_ref), w1_ref)
        c1 = _compress_const(pos_ref, w1_ref, b1_ref)
        hs_next = pltpu.roll(hs, n_ch - 1, 0)
        kv = _compress_out(hf, hs_next, c1, w2_ref, b2_ref)
        kc_scr[...] = kv[:, :KV_W].astype(BF16)
        vct = kv[:, KV_W:].T
        ks_scr[...] = ks_ref[0].astype(BF16)
        kw_scr[...] = kw_ref[0].astype(BF16)
        for g in range(N_KV):
            rows = slice(g * HEAD_DIM, (g + 1) * HEAD_DIM)
            _value_rows(vc_scr, g, vct[rows, :])
            _value_rows(vs_scr, g, vst_ref[0, rows, :])
            _value_rows(vw_scr, g, vwt_ref[0, rows, :])

    q = q_ref[0]
    gt = g_ref[0].T
    lane = lax.broadcasted_iota(jnp.int32, (TQ, LANES), 1)
    for h in range(N_HEADS):
        g, r = divmod(h, HPG)
        gmask = (lane < HEAD_DIM) if g == 0 else (lane >= HEAD_DIM)
        qb_scr[h * TQ:(h + 1) * TQ, :] = jnp.where(gmask, q[:, r * LANES:(r + 1) * LANES], 0.0).astype(BF16)
    q0 = qt * TQ

    def gate_cols(j):
        return jnp.concatenate([gt[3 * h + j:3 * h + j + 1, :] for h in range(N_HEADS)], axis=1)

    r0 = pl.multiple_of((n_qt - 1 - qt) * (TQ // CMP_STRIDE), SUBLANES)
    sc = _dot_nt(kc_scr[...], qb_scr[...]) + cmaster_ref[pl.ds(r0, n_ch), :]
    m = jnp.max(sc, axis=0, keepdims=True)
    e = jnp.exp2(sc - m)
    pt = jnp.where(m > 0.5 * NEG, e / jnp.sum(e, axis=0, keepdims=True), 0.0)
    pb = pt.astype(BF16)
    oc = jnp.concatenate([_dot(vc_scr[g, 0:HEAD_DIM, :], pb[:, g * GCOLS:(g + 1) * GCOLS]) for g in range(N_KV)],
                         axis=1)
    tot_scr[...] = oc * gate_cols(0)

    @pl.when(qt == 0)
    def _no_mask():
        neg_scr[...] = jnp.zeros(neg_scr.shape, F32)

    @pl.when((qt + 1) * TQ > N_SEL * SLC_BLOCK)
    def _select():
        psum = []
        for g in range(N_KV):
            acc = pt[:, g * GCOLS:g * GCOLS + TQ]
            for r in range(1, HPG):
                acc = acc + pt[:, g * GCOLS + r * TQ:g * GCOLS + (r + 1) * TQ]
            psum.append(acc)
        p1, p2, p3 = _split3(jnp.concatenate(psum, axis=1))
        imp = _dot(mimp_ref[...], p1) + _dot(mimp_ref[...], p2) + _dot(mimp_ref[...], p3)
        shape = (n_blk, N_KV * TQ)
        bj = lax.broadcasted_iota(jnp.int32, shape, 0)
        tpos = q0 + lax.broadcasted_iota(jnp.int32, shape, 1) % TQ
        cur = tpos // SLC_BLOCK
        forced = (bj == 0) | (bj == cur) | (bj == cur - 1)
        visible = bj * SLC_BLOCK <= tpos
        score = jnp.where(forced, -NEG, jnp.where(visible, imp, NEG))
        rank = jnp.zeros(shape, F32)
        for i in range(n_blk):
            si = score[i:i + 1, :]
            ahead = (si > score) | ((si == score) & (bj > i))
            rank = rank + jnp.where(ahead, 1.0, 0.0)
        unsel = jnp.where(rank < float(N_SEL), 0.0, 1.0).astype(BF16)
        neg_scr[...] = _dot(eblk_ref[...], unsel)

    _flash_init(m_scr, acc_scr)

    def sel_block(k0, n_tiles, tail_ref):
        _flash_block(ks_scr, vs_scr, qb_scr, k0, n_tiles * TK, None, tail_ref, neg_scr, m_scr, acc_scr)

    n_far = jnp.maximum(qt - 1, 0)
    n_win = WINDOW // TK
    _flash_init(m2_scr, acc2_scr)

    def far_body(j, carry):
        sel_block(pl.multiple_of(j * FAR_TILES * TK, FAR_TILES * TK), FAR_TILES, None)
        return carry

    lax.fori_loop(0, n_far // FAR_TILES, far_body, 0)
    for rem in range(FAR_TILES):
        @pl.when((qt >= 1) & (n_far % FAR_TILES == rem))
        def _sel_last(rem=rem):
            k0s = pl.multiple_of((qt - 1 - rem) * TK, TK)

            @pl.when(qt >= n_win)
            def _pair():
                k0w = pl.multiple_of((qt - n_win) * TK, TK)
                st_s = _flash_block(ks_scr, None, qb_scr, k0s, (rem + 2) * TK, None, tn_ref, neg_scr, None, None)
                st_w = _flash_block(kw_scr, None, qb_scr, k0w, (n_win + 1) * TK, t4_ref, tn_ref, None, None, None)
                _flash_finish(st_s, vs_scr, k0s, (rem + 2) * TK, m_scr, acc_scr)
                _flash_finish(st_w, vw_scr, k0w, (n_win + 1) * TK, m2_scr, acc2_scr)

            @pl.when(qt < n_win)
            def _alone():
                sel_block(k0s, rem + 2, tn_ref)

    @pl.when(qt == 0)
    def _sel_first():
        sel_block(0, 1, tn_ref)

    _flash_add(tot_scr, m_scr, acc_scr, gate_cols(1))

    for n_prev in range(n_win):
        @pl.when(qt == n_prev)
        def _win(n_prev=n_prev):
            _flash_block(kw_scr, vw_scr, qb_scr, pl.multiple_of((qt - n_prev) * TK, TK), (n_prev + 1) * TK,
                         None, tn_ref, None, m2_scr, acc2_scr)

    _flash_add(tot_scr, m2_scr, acc2_scr, gate_cols(2))
    tot = tot_scr[...]
    for r in range(HPG):
        blk = jnp.concatenate([tot[:, r * TQ:(r + 1) * TQ], tot[:, GCOLS + r * TQ:GCOLS + (r + 1) * TQ]], axis=0)
        o_ref[0, :, r * LANES:(r + 1) * LANES] = blk.T


def _nsa_prompt(q, kvc, kvs, kvw, kvst, kvwt, gates, cw, tables):
    b, seq, _ = q.shape
    n_qt = seq // TQ
    n_ch = seq // CMP_STRIDE
    n_blk = seq // SLC_BLOCK
    tn, t4, cmaster = tables
    mimp = jnp.asarray(_imp_matrix(n_ch, n_blk, n_ch - 1).T, BF16)
    in_blk = np.arange(seq)[:, None] // SLC_BLOCK == np.arange(n_blk)[None, :]
    eblk = jnp.asarray(np.where(in_blk, NEG, 0.0), BF16)

    def full(a):
        nd = a.ndim
        return pl.BlockSpec(a.shape, lambda i, j, nd=nd: (0,) * nd)

    k_spec = pl.BlockSpec((1, seq, KV_W), lambda i, j: (i, 0, 0))
    vt_spec = pl.BlockSpec((1, KV_W, seq), lambda i, j: (i, 1, 0))
    return pl.pallas_call(
        _nsa_prompt_kernel,
        grid=(b, n_qt),
        in_specs=[pl.BlockSpec((1, TQ, D_ATT), lambda i, j: (i, j, 0)),
                  k_spec, pl.BlockSpec((1, seq, KV_W), lambda i, j: (i, 0, 1)),
                  k_spec, k_spec, vt_spec, vt_spec,
                  pl.BlockSpec((1, TQ, GATE_PAD), lambda i, j: (i, j, 0)),
                  full(cw["w1"]), full(cw["pos"]), full(cw["b1"]), full(cw["w2"]), full(cw["b2"]),
                  full(cmaster), full(tn), full(t4), full(mimp), full(eblk)],
        out_specs=pl.BlockSpec((1, TQ, D_ATT), lambda i, j: (i, j, 0)),
        out_shape=jax.ShapeDtypeStruct((b, seq, D_ATT), F32),
        scratch_shapes=[pltpu.VMEM((n_ch, KV_W), BF16), pltpu.VMEM((N_KV, VROWS, n_ch), BF16),
                        pltpu.VMEM((seq, KV_W), BF16), pltpu.VMEM((N_KV, VROWS, seq), BF16),
                        pltpu.VMEM((seq, KV_W), BF16), pltpu.VMEM((N_KV, VROWS, seq), BF16),
                        pltpu.VMEM((QCOLS, LANES), BF16),
                        pltpu.VMEM((seq, N_KV * TQ), F32),
                        pltpu.VMEM((1, QCOLS), F32), pltpu.VMEM((VROWS, QCOLS), F32),
                        pltpu.VMEM((HEAD_DIM, QCOLS), F32),
                        pltpu.VMEM((1, QCOLS), F32), pltpu.VMEM((VROWS, QCOLS), F32)],
        compiler_params=_cparams(("arbitrary", "arbitrary")),
        name="nsa_prompt",
    )(q, kvc, kvc, kvs, kvw, kvst, kvwt, gates, cw["w1"], cw["pos"], cw["b1"], cw["w2"], cw["b2"],
      cmaster, tn, t4, mimp, eblk)


def _s5_kernel(u_ref, h0_ref, v_ref, m_ref, wh_ref, a1_ref, a2_ref, y_ref, ht_ref, z_scr, *, n_chunks, bt):
    gc = u_ref.shape[1]
    sw = 2 * S5_STATE
    for g in range(gc):
        z_scr[:, g * sw:(g + 1) * sw] = _dot(u_ref[0, g], v_ref[g])
    a1 = a1_ref[...]
    a2 = a2_ref[...]

    def step(k, h):
        r0 = pl.multiple_of(k * bt, bt)
        z = z_scr[pl.ds(r0, bt), :]
        z_scr[pl.ds(r0, bt), :] = h
        hsw = jnp.concatenate(
            [pltpu.roll(h[:, g * sw:(g + 1) * sw], S5_STATE, 1) for g in range(gc)], axis=1)
        return a1 * h + a2 * hsw + z

    ht_ref[0] = lax.fori_loop(0, n_chunks, step, h0_ref[0])
    for g in range(gc):
        y_ref[0, g] = (_dot(u_ref[0, g], m_ref[g])
                       + _dot(z_scr[:, g * sw:(g + 1) * sw].astype(BF16), wh_ref[g]))


def _s5_mats(a_re, a_im, log_dt, b_re, b_im, c_re, c_im, chunk):
    hp = lax.Precision.HIGHEST
    dt = jnp.exp(log_dt)[:, None]
    mag = jnp.exp(dt * a_re)
    ab_re, ab_im = mag * jnp.cos(dt * a_im), mag * jnp.sin(dt * a_im)
    den = a_re * a_re + a_im * a_im
    x_re, x_im = ab_re - 1.0, ab_im
    k_re = (x_re * a_re + x_im * a_im) / den
    k_im = (x_im * a_re - x_re * a_im) / den
    bb_re = k_re[..., None] * b_re - k_im[..., None] * b_im
    bb_im = k_re[..., None] * b_im + k_im[..., None] * b_re
    pw_re, pw_im = [jnp.ones_like(ab_re)], [jnp.zeros_like(ab_im)]
    for _ in range(chunk):
        pr, pi = pw_re[-1], pw_im[-1]
        pw_re.append(pr * ab_re - pi * ab_im)
        pw_im.append(pr * ab_im + pi * ab_re)
    pw_re, pw_im = jnp.stack(pw_re), jnp.stack(pw_im)
    cl_re = c_re[None] * pw_re[:, :, None, :] - c_im[None] * pw_im[:, :, None, :]
    cl_im = c_re[None] * pw_im[:, :, None, :] + c_im[None] * pw_re[:, :, None, :]
    t = (jnp.einsum('tgcp,gpd->tgcd', cl_re, bb_re, precision=hp)
         - jnp.einsum('tgcp,gpd->tgcd', cl_im, bb_im, precision=hp))
    s_i = np.arange(chunk)[:, None]
    t_i = np.arange(chunk)[None, :]
    lag = np.clip(t_i - s_i, 0, None)
    blk = jnp.where((t_i >= s_i)[:, :, None, None, None], t[lag], 0.0)
    ng = a_re.shape[0]
    m = blk.transpose(2, 0, 4, 1, 3).reshape(ng, chunk * S5_GROUP, chunk * S5_GROUP)
    rev = chunk - 1 - np.arange(chunk)
    vb_re = pw_re[rev][:, :, :, None] * bb_re[None] - pw_im[rev][:, :, :, None] * bb_im[None]
    vb_im = pw_re[rev][:, :, :, None] * bb_im[None] + pw_im[rev][:, :, :, None] * bb_re[None]
    v = jnp.concatenate([vb_re, vb_im], axis=2).transpose(1, 0, 3, 2).reshape(ng, chunk * S5_GROUP, 2 * S5_STATE)
    wh = jnp.concatenate([cl_re[1:], -cl_im[1:]], axis=3)
    wh = wh.transpose(1, 3, 0, 2).reshape(ng, 2 * S5_STATE, chunk * S5_GROUP)
    a1 = jnp.concatenate([pw_re[chunk], pw_re[chunk]], axis=1).reshape(1, ng * 2 * S5_STATE)
    a2 = jnp.concatenate([-pw_im[chunk], pw_im[chunk]], axis=1).reshape(1, ng * 2 * S5_STATE)
    return v.astype(BF16), m.astype(BF16), wh.astype(BF16), a1, a2


def _s5(u, h0_re, h0_im, mats, chunk, bt, gc=8):
    b, t_len, _ = u.shape
    ng = N_SSM_GROUPS
    n_chunks = t_len // chunk
    nbc = b // bt
    lc = chunk * S5_GROUP
    rows = n_chunks * bt
    sw = 2 * S5_STATE
    v, m, wh, a1, a2 = mats
    ur = u.reshape(nbc, bt, n_chunks, chunk, ng, S5_GROUP).transpose(0, 4, 2, 1, 3, 5)
    ur = ur.reshape(nbc, ng, rows, lc).astype(BF16)
    h0 = jnp.concatenate([h0_re, h0_im], axis=-1).reshape(nbc, bt, ng * sw)
    kern = functools.partial(_s5_kernel, n_chunks=n_chunks, bt=bt)
    y, ht = pl.pallas_call(
        kern,
        grid=(nbc, ng // gc),
        in_specs=[pl.BlockSpec((1, gc, rows, lc), lambda i, j: (i, j, 0, 0)),
                  pl.BlockSpec((1, bt, gc * sw), lambda i, j: (i, 0, j)),
                  pl.BlockSpec((gc, lc, sw), lambda i, j: (j, 0, 0)),
                  pl.BlockSpec((gc, lc, lc), lambda i, j: (j, 0, 0)),
                  pl.BlockSpec((gc, sw, lc), lambda i, j: (j, 0, 0)),
                  pl.BlockSpec((1, gc * sw), lambda i, j: (0, j)),
                  pl.BlockSpec((1, gc * sw), lambda i, j: (0, j))],
        out_specs=[pl.BlockSpec((1, gc, rows, lc), lambda i, j: (i, j, 0, 0)),
                   pl.BlockSpec((1, bt, gc * sw), lambda i, j: (i, 0, j))],
        out_shape=[jax.ShapeDtypeStruct((nbc, ng, rows, lc), F32),
                   jax.ShapeDtypeStruct((nbc, bt, ng * sw), F32)],
        scratch_shapes=[pltpu.VMEM((rows, gc * sw), F32)],
        compiler_params=_cparams(("parallel", "parallel")),
        name="s5_scan",
    )(ur, h0, v, m, wh, a1, a2)
    y = y.reshape(nbc, ng, n_chunks, bt, chunk, S5_GROUP).transpose(0, 3, 2, 4, 1, 5).reshape(b, t_len, D_SSM)
    ht = ht.reshape(b, ng, sw)
    return y, ht[..., :S5_STATE], ht[..., S5_STATE:]


def _s5_nat_kernel(u_ref, h0_ref, v_ref, m_ref, wh_ref, ar_ref, ai_ref, y_ref, ht_ref, uf_scr, z_scr,
                   *, chunk, n_chunks, bt):
    rows = bt * n_chunks
    half = S5_LANE_GROUPS * S5_STATE
    n_tiles = 2 * half // LANES
    for t in range(chunk):
        piece = u_ref[:, pl.ds(t, n_chunks, stride=chunk), :]
        uf_scr[:, t * LANES:(t + 1) * LANES] = piece.reshape(rows, LANES).astype(BF16)
    uf = uf_scr[...]
    z = _dot(uf, v_ref[0])
    for j in range(n_tiles):
        z_scr[j] = z[:, j * LANES:(j + 1) * LANES]
    ar = ar_ref[...]
    ai = ai_ref[...]

    def step(k, h):
        idx = pl.ds(k, bt, stride=n_chunks)
        zk = jnp.concatenate([z_scr[j, idx, :] for j in range(n_tiles)], axis=1)
        for j in range(n_tiles):
            z_scr[j, idx, :] = h[:, j * LANES:(j + 1) * LANES]
        hre, him = h[:, :half], h[:, half:]
        return jnp.concatenate([ar * hre - ai * him, ar * him + ai * hre], axis=1) + zk

    ht_ref[0] = lax.fori_loop(0, n_chunks, step, h0_ref[0])
    hs = jnp.concatenate([z_scr[j] for j in range(n_tiles)], axis=1).astype(BF16)
    y = _dot(uf, m_ref[0]) + _dot(hs, wh_ref[0])
    for t in range(chunk):
        y_ref[:, pl.ds(t, n_chunks, stride=chunk), :] = y[:, t * LANES:(t + 1) * LANES].reshape(bt, n_chunks, LANES)


def _s5_nat_mats(mats, chunk):
    v, m, wh, a1, a2 = mats
    ng = v.shape[0]
    nx = ng // S5_LANE_GROUPS

    def place(n_outer, n_inner):
        sel = np.zeros((S5_LANE_GROUPS, n_outer * n_inner, n_outer * S5_LANE_GROUPS * n_inner), np.float32)
        a, b = np.meshgrid(np.arange(n_outer), np.arange(n_inner), indexing='ij')
        for g in range(S5_LANE_GROUPS):
            sel[g, (a * n_inner + b).ravel(), ((a * S5_LANE_GROUPS + g) * n_inner + b).ravel()] = 1.0
        return jnp.asarray(sel, BF16)

    def block_diag(w, rows, cols):
        w4 = w.reshape(nx, S5_LANE_GROUPS, w.shape[1], w.shape[2])
        right = jnp.einsum('xgrc,gcd->xgrd', w4, cols, preferred_element_type=BF16)
        return jnp.einsum('grs,xgrd->xsd', rows, right, preferred_element_type=BF16)

    io = place(chunk, S5_GROUP)
    st = place(2, S5_STATE)
    v8 = block_diag(v, io, st)
    m8 = block_diag(m, io, io)
    w8 = block_diag(wh, st, io)
    ar = a1.reshape(nx, S5_LANE_GROUPS, 2, S5_STATE)[:, :, 0].reshape(1, ng * S5_STATE)
    ai = a2.reshape(nx, S5_LANE_GROUPS, 2, S5_STATE)[:, :, 1].reshape(1, ng * S5_STATE)
    return v8, m8, w8, ar, ai


def _s5_nat(u, h0_re, h0_im, mats, chunk, bt):
    b, t_len, _ = u.shape
    n_chunks = t_len // chunk
    nbc = b // bt
    nx = D_SSM // LANES
    half = S5_LANE_GROUPS * S5_STATE
    gw = 2 * half
    lk = chunk * LANES
    rows = bt * n_chunks
    v8, m8, w8, ar, ai = _s5_nat_mats(mats, chunk)
    planar = lambda a: a.reshape(nbc, bt, nx, 1, half)
    h0 = jnp.concatenate([planar(h0_re), planar(h0_im)], axis=3).reshape(nbc, bt, nx * gw)
    y, ht = pl.pallas_call(
        functools.partial(_s5_nat_kernel, chunk=chunk, n_chunks=n_chunks, bt=bt),
        grid=(nbc, nx),
        in_specs=[pl.BlockSpec((bt, t_len, LANES), lambda i, j: (i, 0, j)),
                  pl.BlockSpec((1, bt, gw), lambda i, j: (i, 0, j)),
                  pl.BlockSpec((1, lk, gw), lambda i, j: (j, 0, 0)),
                  pl.BlockSpec((1, lk, lk), lambda i, j: (j, 0, 0)),
                  pl.BlockSpec((1, gw, lk), lambda i, j: (j, 0, 0)),
                  pl.BlockSpec((1, half), lambda i, j: (0, j)),
                  pl.BlockSpec((1, half), lambda i, j: (0, j))],
        out_specs=[pl.BlockSpec((bt, t_len, LANES), lambda i, j: (i, 0, j)),
                   pl.BlockSpec((1, bt, gw), lambda i, j: (i, 0, j))],
        out_shape=[jax.ShapeDtypeStruct((b, t_len, D_SSM), F32),
                   jax.ShapeDtypeStruct((nbc, bt, nx * gw), F32)],
        scratch_shapes=[pltpu.VMEM((rows, lk), BF16), pltpu.VMEM((gw // LANES, rows, LANES), F32)],
        compiler_params=_cparams(("parallel", "parallel")),
        name="s5_scan_nat",
    )(u, h0, v8, m8, w8, ar, ai)
    ht = ht.reshape(b, nx, 2, S5_LANE_GROUPS, S5_STATE)
    return (y, ht[:, :, 0].reshape(b, N_SSM_GROUPS, S5_STATE), ht[:, :, 1].reshape(b, N_SSM_GROUPS, S5_STATE))


def _epilogue_kernel(x_ref, o_ref, y_ref, u_ref, wz_ref, d_ref, glu0_ref, glu1_ref, gb0_ref, gb1_ref,
                     wa_ref, ws_ref, wo_ref, lng_ref, lnb_ref, out_ref, *, alpha):
    x = x_ref[...]
    z = _dot(x.astype(BF16), wz_ref[...])
    za = z[:, :D_ATT]
    zs = z[:, D_ATT:D_ATT + D_SSM]
    ga = z[:, D_ATT + D_SSM:D_ATT + D_SSM + D_MODEL]
    gs = z[:, D_ATT + D_SSM + D_MODEL:]
    p_att = _dot((o_ref[...] * jax.nn.silu(za)).astype(BF16), wa_ref[...])
    y = jax.nn.gelu(y_ref[...] + d_ref[...] * u_ref[...]).astype(BF16)
    s = (_dot(y, glu0_ref[...]) + gb0_ref[...]) * jax.nn.sigmoid(_dot(y, glu1_ref[...]) + gb1_ref[...])
    p_ssm = _dot((s * jax.nn.silu(zs)).astype(BF16), ws_ref[...])
    merged = jax.nn.sigmoid(ga) * p_att + jax.nn.sigmoid(gs) * p_ssm
    r = alpha * x + _dot(merged.astype(BF16), wo_ref[...])
    mu = jnp.mean(r, axis=-1, keepdims=True)
    c = r - mu
    var = jnp.mean(c * c, axis=-1, keepdims=True)
    out_ref[...] = c * lax.rsqrt(var + LN_EPS) * lng_ref[...] + lnb_ref[...]


def _epilogue(x2d, o2d, y2d, u2d, ew, alpha):
    m = x2d.shape[0]
    tm = min(EPILOGUE_ROWS, m)
    names = ("wz", "d", "glu0", "glu1", "gb0", "gb1", "wa", "ws", "wo", "lng", "lnb")
    ws = [ew[n] for n in names]

    def tok(w):
        return pl.BlockSpec((tm, w), lambda i: (i, 0))

    return pl.pallas_call(
        functools.partial(_epilogue_kernel, alpha=alpha),
        grid=(m // tm,),
        in_specs=[tok(D_MODEL), tok(D_ATT), tok(D_SSM), tok(D_SSM)]
        + [pl.BlockSpec(w.shape, lambda i: (0, 0)) for w in ws],
        out_specs=tok(D_MODEL),
        out_shape=jax.ShapeDtypeStruct((m, D_MODEL), F32),
        compiler_params=_cparams(("parallel",)),
        name="epilogue",
    )(x2d, o2d, y2d, u2d, *ws)


def _layer_weights(l, w_in, cmp_w1, cmp_b1, cmp_w2, cmp_b2, cmp_pos, ssm_d, ssm_glu_w, ssm_glu_b,
                   w_att_out, w_ssm_out, w_o, ln_g, ln_b):
    offs = np.concatenate([[0], np.cumsum(IN_WIDTHS)])
    col = lambda i: w_in[l][:, offs[i]:offs[i + 1]]
    perm = _pair_perm()
    gate = jnp.pad(col(7), ((0, 0), (0, GATE_PAD - 3 * N_HEADS)))
    w_seq = jnp.concatenate([col(0)[:, perm], col(1), col(2), col(3), col(4), col(5), col(6), gate, col(9)],
                            axis=1).astype(BF16)
    wz = jnp.concatenate([col(8)[:, perm], col(10), col(11)], axis=1).astype(BF16)

    half = CMP_BLOCK // 2
    eye = jnp.eye(4, dtype=F32)
    eye_g = jnp.eye(N_KV, dtype=F32)
    w1, pos = [], []
    for kv in range(2):
        bd = jnp.einsum('pdh,st->psdth', cmp_w1[l, kv], eye_g)
        bd = bd.reshape(CMP_BLOCK, KV_W, N_KV * CMP_HIDDEN)
        w1.append(jnp.concatenate([bd[:half].reshape(half * KV_W, -1), bd[half:].reshape(half * KV_W, -1)], axis=1))
        pos_g = jnp.broadcast_to(cmp_pos[l, kv][:, None, :], (CMP_BLOCK, N_KV, HEAD_DIM))
        pos.append(jnp.zeros((SUBLANES, half * KV_W), F32)
                   .at[0].set(pos_g[:half].reshape(-1)).at[1].set(pos_g[half:].reshape(-1)))
    w1 = jnp.stack(w1).astype(BF16)
    pos = jnp.stack(pos)
    b1 = jnp.concatenate([cmp_b1[l, 0], cmp_b1[l, 0], cmp_b1[l, 1], cmp_b1[l, 1]]).reshape(1, -1)
    w2s = jnp.stack([cmp_w2[l, 0], cmp_w2[l, 0], cmp_w2[l, 1], cmp_w2[l, 1]])
    w2 = jnp.einsum('shd,st->shtd', w2s, eye).reshape(4 * CMP_HIDDEN, 4 * HEAD_DIM).astype(BF16)
    b2 = jnp.concatenate([cmp_b2[l, 0], cmp_b2[l, 0], cmp_b2[l, 1], cmp_b2[l, 1]]).reshape(1, -1)
    cw = dict(w1=w1, pos=pos, b1=b1, w2=w2, b2=b2)

    ew = dict(wz=wz, d=ssm_d[l].reshape(1, D_SSM),
              glu0=ssm_glu_w[l, 0].astype(BF16), glu1=ssm_glu_w[l, 1].astype(BF16),
              gb0=ssm_glu_b[l, 0].reshape(1, -1), gb1=ssm_glu_b[l, 1].reshape(1, -1),
              wa=w_att_out[l][perm].astype(BF16), ws=w_ssm_out[l].astype(BF16), wo=w_o[l].astype(BF16),
              lng=ln_g[l].reshape(1, -1), lnb=ln_b[l].reshape(1, -1))
    return w_seq, cw, ew


def _prompt_layer(h_p, lw, s5m, tables, alpha):
    b, seq, _ = h_p.shape
    w_seq, cw, ew = lw
    x2d = h_p.reshape(b * seq, D_MODEL)
    q, kvc, kvs, kvw, kvct, kvst, kvwt, gates, u = _inproj(x2d, w_seq, seq)
    r3 = lambda a: a.reshape(b, seq, a.shape[-1])
    o_att = _nsa_prompt(r3(q), r3(kvc), r3(kvs), r3(kvw), kvst, kvwt, r3(gates), cw, tables)
    zero = jnp.zeros((b, N_SSM_GROUPS, S5_STATE), F32)
    y, hr, hi = _s5_nat(r3(u), zero, zero, s5m, S5_CHUNK, bt=min(S5_BATCH_ROWS, b))
    out = _epilogue(x2d, o_att.reshape(b * seq, D_ATT), y.reshape(b * seq, D_SSM), u, ew, alpha)
    return out.reshape(b, seq, D_MODEL), kvct, kvst, kvwt, hr, hi


def _gather_step(page_dma):
    b, c = pl.program_id(0), pl.program_id(1)
    n_c = pl.num_programs(1)
    step = b * n_c + c
    total = pl.num_programs(0) * n_c
    slot = step % 2

    @pl.when(step == 0)
    def _first():
        for j in range(PAGES_PER_STEP):
            page_dma(b, c, j, slot).start()

    @pl.when(step + 1 < total)
    def _prefetch():
        nxt = step + 1
        for j in range(PAGES_PER_STEP):
            page_dma(nxt // n_c, nxt % n_c, j, 1 - slot).start()

    for j in range(PAGES_PER_STEP):
        page_dma(b, c, j, slot).wait()
    return slot


def _sample_queries(q):
    t = q.shape[0]
    lane = lax.broadcasted_iota(jnp.int32, (t, LANES), 1)
    parts = []
    for g in range(N_KV):
        gmask = (lane < HEAD_DIM) if g == 0 else (lane >= HEAD_DIM)
        parts += [jnp.where(gmask, q[:, r * LANES:(r + 1) * LANES], 0.0) for r in range(HPG)]
    return jnp.concatenate(parts, axis=0).astype(BF16)


def _softmax_rows(s):
    m = jnp.max(s, axis=1, keepdims=True)
    valid = m > 0.5 * NEG
    e = jnp.exp2(s - m)
    return jnp.where(valid, e / jnp.sum(e, axis=1, keepdims=True), 0.0)


def _cmp_sample_kernel(pt_ref, cache_hbm, q_ref, perm_ref, w1_ref, pos_ref, b1_ref, w2_ref, b2_ref, biasc_ref,
                       mimp_ref, oc_ref, sel_ref, buf, sem, x_scr, hf_scr, hs_scr,
                       *, layer, t_len, past_len, n_blk):
    c = pl.program_id(1)
    n_c = pl.num_programs(1)
    nh = 4 * CMP_HIDDEN
    cpp = buf.shape[3] // CMP_STRIDE
    rows = PAGES_PER_STEP * cpp

    def page_dma(b, cc, j, slot):
        page = pt_ref[b, cc * PAGES_PER_STEP + j]
        return pltpu.make_async_copy(cache_hbm.at[layer, page], buf.at[slot, j], sem.at[slot])

    slot = _gather_step(page_dma)

    def regroup(j, carry):
        xp = _dot_nt(perm_ref[...], buf[slot, j].astype(BF16))
        r0 = pl.multiple_of(j * cpp, cpp)
        for p in range(CMP_STRIDE):
            for kv in range(2):
                x_scr[kv, pl.ds(r0, cpp), p * KV_W:(p + 1) * KV_W] = xp[p * cpp:(p + 1) * cpp, kv * KV_W:(kv + 1) * KV_W]
        return carry

    lax.fori_loop(0, PAGES_PER_STEP, regroup, 0, unroll=8)
    hf, hs = _compress_hidden(x_scr[0].astype(BF16), x_scr[1].astype(BF16), w1_ref)
    r0 = pl.multiple_of(c * rows, rows)
    hf_scr[pl.ds(r0, rows), :] = hf
    hs_scr[pl.ds(r0, rows), :] = hs

    @pl.when(c == n_c - 1)
    def _finish():
        n_ch = hf_scr.shape[0]
        c1 = _compress_const(pos_ref, w1_ref, b1_ref)
        hs_next = pltpu.roll(hs_scr[...], n_ch - 1, 0)
        kv = _compress_out(hf_scr[...], hs_next, c1, w2_ref, b2_ref)
        q64 = _sample_queries(q_ref[0])
        p = _softmax_rows(_dot_nt(q64, kv[:, :KV_W].astype(BF16)) + biasc_ref[...])
        oc_ref[0] = _dot(p.astype(BF16), kv[:, KV_W:].astype(BF16))
        psum = []
        for g in range(N_KV):
            acc = p[g * HPG * t_len:g * HPG * t_len + t_len]
            for r in range(1, HPG):
                acc = acc + p[(g * HPG + r) * t_len:(g * HPG + r + 1) * t_len]
            psum.append(acc)
        psum = jnp.concatenate(psum, axis=0)
        p1, p2, p3 = _split3(psum)
        imp = _dot(p1, mimp_ref[...]) + _dot(p2, mimp_ref[...]) + _dot(p3, mimp_ref[...])
        shape = imp.shape
        bj = lax.broadcasted_iota(jnp.int32, shape, 1)
        tpos = past_len + lax.broadcasted_iota(jnp.int32, shape, 0) % t_len
        cur = tpos // SLC_BLOCK
        forced = (bj == 0) | (bj == cur) | (bj == cur - 1)
        visible = bj * SLC_BLOCK <= tpos
        score = jnp.where(forced, -NEG, jnp.where(visible, imp, NEG))
        score = jnp.where(bj < n_blk, score, 2.0 * NEG)
        sel = jnp.zeros(shape, F32)
        for _ in range(min(N_SEL, n_blk)):
            best = jnp.max(score, axis=1, keepdims=True)
            first = jnp.min(jnp.where(score == best, bj, shape[1]), axis=1, keepdims=True)
            hit = bj == first
            sel = jnp.where(hit, 1.0, sel)
            score = jnp.where(hit, 3.0 * NEG, score)
        sel_ref[0] = sel


def _slc_sample_kernel(pt_ref, cache_hbm, q_ref, sel_ref, knew_ref, bias_ref, c31_ref, biasn_ref, eblk_ref,
                       os_ref, buf, sem, m_scr, l_scr, acc_scr, *, layer):
    c = pl.program_id(1)
    n_c = pl.num_programs(1)
    page = buf.shape[2] // PAGES_PER_STEP

    def page_dma(b, cc, j, slot):
        pg = pt_ref[b, cc * PAGES_PER_STEP + j]
        return pltpu.make_async_copy(cache_hbm.at[layer, pg], buf.at[slot, :, pl.ds(j * page, page)], sem.at[slot])

    slot = _gather_step(page_dma)
    q64 = _sample_queries(q_ref[0])

    @pl.when(c == 0)
    def _init():
        m_scr[...] = jnp.full(m_scr.shape, NEG, F32)
        l_scr[...] = jnp.zeros(l_scr.shape, F32)
        acc_scr[...] = jnp.zeros(acc_scr.shape, F32)

    def update(s, vt_bf):
        m_old = m_scr[...]
        m_new = jnp.maximum(m_old, jnp.max(s, axis=1, keepdims=True))
        alpha = jnp.exp2(m_old - m_new)
        e = jnp.exp2(s - m_new)
        l_scr[...] = alpha * l_scr[...] + jnp.sum(e, axis=1, keepdims=True)
        acc_scr[...] = alpha * acc_scr[...] + _dot_nt(e.astype(BF16), vt_bf)
        m_scr[...] = m_new

    neg = (_dot(sel_ref[0, 0].astype(BF16), eblk_ref[...]) - 1.0) * (-NEG)
    s = _dot(q64, buf[slot, 0:KV_W, :].astype(BF16)) + neg

    @pl.when(c < n_c - 1)
    def _far():
        update(s + c31_ref[...], buf[slot, KV_W:KV2, :].astype(BF16))

    @pl.when(c == n_c - 1)
    def _last():
        update(s + bias_ref[...], buf[slot, KV_W:KV2, :].astype(BF16))
        kn = knew_ref[0]
        update(_dot(q64, kn[0:KV_W, :].astype(BF16)) + biasn_ref[...], kn[KV_W:KV2, :].astype(BF16))
        valid = m_scr[...] > 0.5 * NEG
        os_ref[0] = jnp.where(valid, acc_scr[...] / jnp.where(valid, l_scr[...], 1.0), 0.0)


def _win_sample_kernel(q_ref, kvw_ref, biasw_ref, g_ref, oc_ref, os_ref, o_ref, *, t_len):
    q64 = _sample_queries(q_ref[0])
    kw = kvw_ref[0]
    p = _softmax_rows(_dot(q64, kw[0:KV_W, :].astype(BF16)) + biasw_ref[...])
    ow = _dot_nt(p.astype(BF16), kw[KV_W:KV2, :].astype(BF16))
    gates = g_ref[0]
    oc, osel = oc_ref[0], os_ref[0]
    lane = lax.broadcasted_iota(jnp.int32, (t_len, LANES), 1)

    def head(g, r):
        h = g * HPG + r
        sl = slice(h * t_len, (h + 1) * t_len)
        return (gates[:, 3 * h:3 * h + 1] * oc[sl] + gates[:, 3 * h + 1:3 * h + 2] * osel[sl]
                + gates[:, 3 * h + 2:3 * h + 3] * ow[sl])

    for r in range(HPG):
        o_ref[0, :, r * LANES:(r + 1) * LANES] = jnp.where(lane < HEAD_DIM, head(0, r), head(1, r))


def _nsa_sample(layer, q, gates, kvs_new_t, all_w_t, cmp_t, slc_t, page_table, cw, tables):
    b, t_len, _ = q.shape
    n_pages = page_table.shape[1]
    page = cmp_t.shape[3]
    past_len = n_pages * page
    n_c = n_pages // PAGES_PER_STEP
    n_ch = past_len // CMP_STRIDE
    n_blk = -(-(past_len + t_len) // SLC_BLOCK)
    blk_pad = -(-n_blk // LANES) * LANES
    rows = PAGES_PER_STEP * page
    blk_step = rows // SLC_BLOCK
    nq = N_HEADS * t_len
    assert past_len % SLC_BLOCK == 0 and t_len <= SLC_BLOCK and (past_len + t_len) // CMP_STRIDE == n_ch
    biasc, biass, c31, biasn, biasw = tables
    c31 = c31[:, :1]
    mimp = jnp.asarray(_imp_matrix(n_ch, blk_pad, (past_len + t_len) // CMP_STRIDE - 1), BF16)
    cpp = page // CMP_STRIDE
    pm = np.zeros((page, page), np.float32)
    for p in range(CMP_STRIDE):
        for cc in range(cpp):
            pm[p * cpp + cc, cc * CMP_STRIDE + p] = 1.0
    pm = jnp.asarray(pm, BF16)

    def full(a):
        nd = a.ndim
        return pl.BlockSpec(a.shape, lambda i, j, pt, nd=nd: (0,) * nd)

    q_spec = pl.BlockSpec((1, t_len, D_ATT), lambda i, j, pt: (i, 0, 0))
    o_spec = pl.BlockSpec((1, nq, KV_W), lambda i, j, pt: (i, 0, 0))
    any_spec = pl.BlockSpec(memory_space=pl.ANY)
    oc, sel = pl.pallas_call(
        functools.partial(_cmp_sample_kernel, layer=layer, t_len=t_len, past_len=past_len, n_blk=n_blk),
        grid_spec=pltpu.PrefetchScalarGridSpec(
            num_scalar_prefetch=1, grid=(b, n_c),
            in_specs=[any_spec, q_spec, full(pm), full(cw["w1"]), full(cw["pos"]), full(cw["b1"]), full(cw["w2"]),
                      full(cw["b2"]), full(biasc), full(mimp)],
            out_specs=[o_spec, pl.BlockSpec((1, N_KV * t_len, blk_pad), lambda i, j, pt: (i, 0, 0))],
            scratch_shapes=[pltpu.VMEM((2, PAGES_PER_STEP, KV2, page), F32),
                            pltpu.SemaphoreType.DMA((2,)),
                            pltpu.VMEM((2, PAGES_PER_STEP * cpp, CMP_STRIDE * KV_W), F32),
                            pltpu.VMEM((n_ch, 4 * CMP_HIDDEN), F32), pltpu.VMEM((n_ch, 4 * CMP_HIDDEN), F32)]),
        out_shape=[jax.ShapeDtypeStruct((b, nq, KV_W), F32),
                   jax.ShapeDtypeStruct((b, N_KV * t_len, blk_pad), F32)],
        compiler_params=_cparams(("arbitrary", "arbitrary")),
        name="nsa_sample_cmp",
    )(page_table, cmp_t, q, pm, cw["w1"], cw["pos"], cw["b1"], cw["w2"], cw["b2"], biasc, mimp)

    sel_c = sel[:, :, :n_c * blk_step].reshape(b, N_KV, 1, t_len, n_c, blk_step)
    sel_c = jnp.broadcast_to(sel_c, (b, N_KV, HPG, t_len, n_c, blk_step))
    sel_c = sel_c.transpose(0, 4, 1, 2, 3, 5).reshape(b, n_c, nq, blk_step)
    sel_c = jnp.pad(sel_c, ((0, 0), (0, 0), (0, 0), (0, LANES - blk_step)))
    eblk = jnp.asarray(np.arange(LANES)[:, None] == (np.arange(rows)[None, :] // SLC_BLOCK), BF16)
    o_s = pl.pallas_call(
        functools.partial(_slc_sample_kernel, layer=layer),
        grid_spec=pltpu.PrefetchScalarGridSpec(
            num_scalar_prefetch=1, grid=(b, n_c),
            in_specs=[any_spec, q_spec,
                      pl.BlockSpec((1, 1, nq, LANES), lambda i, j, pt: (i, j, 0, 0)),
                      pl.BlockSpec((1, KV2, LANES), lambda i, j, pt: (i, 0, 0)),
                      full(biass), full(c31), full(biasn), full(eblk)],
            out_specs=o_spec,
            scratch_shapes=[pltpu.VMEM((2, KV2, rows), F32),
                            pltpu.SemaphoreType.DMA((2,)),
                            pltpu.VMEM((nq, 1), F32), pltpu.VMEM((nq, 1), F32), pltpu.VMEM((nq, KV_W), F32)]),
        out_shape=jax.ShapeDtypeStruct((b, nq, KV_W), F32),
        compiler_params=_cparams(("arbitrary", "arbitrary")),
        name="nsa_sample_slc",
    )(page_table, slc_t, q, sel_c, kvs_new_t, biass, c31, biasn, eblk)

    w_pad = all_w_t.shape[2]
    return pl.pallas_call(
        functools.partial(_win_sample_kernel, t_len=t_len),
        grid=(b,),
        in_specs=[pl.BlockSpec((1, t_len, D_ATT), lambda i: (i, 0, 0)),
                  pl.BlockSpec((1, KV2, w_pad), lambda i: (i, 0, 0)),
                  pl.BlockSpec(biasw.shape, lambda i: (0, 0)),
                  pl.BlockSpec((1, t_len, GATE_PAD), lambda i: (i, 0, 0)),
                  pl.BlockSpec((1, nq, KV_W), lambda i: (i, 0, 0)),
                  pl.BlockSpec((1, nq, KV_W), lambda i: (i, 0, 0))],
        out_specs=pl.BlockSpec((1, t_len, D_ATT), lambda i: (i, 0, 0)),
        out_shape=jax.ShapeDtypeStruct((b, t_len, D_ATT), F32),
        compiler_params=_cparams(("parallel",)),
        name="nsa_sample_win",
    )(q, all_w_t, biasw, gates, oc, o_s)


def _sample_layer(layer, h_s, lw, s5m, tables, alpha, cmp_t, slc_t, win_t, h0_re, h0_im, page_table, w_pad):
    b, t_len, _ = h_s.shape
    w_seq, cw, ew = lw
    m = b * t_len
    x2d = h_s.reshape(m, D_MODEL)
    q, kvc, kvs, kvw, kvct, kvst, kvwt, gates, u = _inproj(x2d, w_seq, m)
    r3 = lambda a: a.reshape(b, t_len, a.shape[-1])
    per_row = lambda a: a.reshape(KV2, b, t_len).transpose(1, 0, 2)
    w_buf = win_t.shape[2]
    all_w_t = jnp.concatenate([win_t, per_row(kvwt)], axis=2)
    all_w_pad = jnp.pad(all_w_t, ((0, 0), (0, 0), (0, w_pad - (w_buf + t_len))))
    kvs_new_t = jnp.pad(per_row(kvst), ((0, 0), (0, 0), (0, LANES - t_len)))
    o_att = _nsa_sample(layer, r3(q), r3(gates), kvs_new_t, all_w_pad, cmp_t, slc_t, page_table, cw, tables)
    y, hr, hi = _s5(r3(u), h0_re, h0_im, s5m, t_len, bt=b)
    out = _epilogue(x2d, o_att.reshape(m, D_ATT), y.reshape(m, D_SSM), u, ew, alpha)
    return out.reshape(b, t_len, D_MODEL), r3(kvc), r3(kvs), all_w_t[:, :, t_len:], hr, hi


def _feature_major(a):
    nd = a.ndim
    perm = tuple(range(nd - 4)) + (nd - 3, nd - 2, nd - 1, nd - 4)
    t = a.transpose(perm)
    return t.reshape(t.shape[:nd - 4] + (KV2, a.shape[nd - 4]))


def _row_major(a):
    nd = a.ndim
    t = a.reshape(a.shape[:nd - 2] + (2, N_KV, HEAD_DIM, a.shape[-1]))
    perm = tuple(range(nd - 2)) + (nd + 1, nd - 2, nd - 1, nd)
    return t.transpose(perm)


def kernel(x_prompt, x_sample, cache_kv_cmp, cache_kv_slc, state_win_kv, state_ssm_re, state_ssm_im,
           page_table, rel_bias, w_in, cmp_w1, cmp_b1, cmp_w2, cmp_b2, cmp_pos,
           ssm_a_re, ssm_a_im, ssm_log_dt, ssm_b_re, ssm_b_im, ssm_c_re, ssm_c_im, ssm_d,
           ssm_glu_w, ssm_glu_b, w_att_out, w_ssm_out, w_o, ln_g, ln_b):
    depth = w_in.shape[0]
    alpha = (2 * depth) ** 0.25
    bp, seq = x_prompt.shape[:2]
    bs, t_len = x_sample.shape[:2]
    page = cache_kv_cmp.shape[2]
    past_len = page_table.shape[1] * page
    w_buf = state_win_kv.shape[2]
    w_pad = -(-(w_buf + t_len) // LANES) * LANES
    n_win = min(WINDOW, seq)
    p_tables = _prompt_tables(rel_bias, seq)
    s_tables = _sample_tables(rel_bias, past_len, t_len, w_buf, w_pad, PAGES_PER_STEP * page)
    cmp_t, slc_t, win_t = _feature_major(cache_kv_cmp), _feature_major(cache_kv_slc), _feature_major(state_win_kv)
    h_p, h_s = x_prompt, x_sample
    outs_p, outs_s = [], []
    for l in range(depth):
        lw = _layer_weights(l, w_in, cmp_w1, cmp_b1, cmp_w2, cmp_b2, cmp_pos, ssm_d, ssm_glu_w, ssm_glu_b,
                            w_att_out, w_ssm_out, w_o, ln_g, ln_b)
        ssm = (ssm_a_re[l], ssm_a_im[l], ssm_log_dt[l], ssm_b_re[l], ssm_b_im[l], ssm_c_re[l], ssm_c_im[l])
        h_p, kvct, kvst, kvwt, hr, hi = _prompt_layer(h_p, lw, _s5_mats(*ssm, S5_CHUNK), p_tables, alpha)
        outs_p.append((kvct, kvst, kvwt[:, :, seq - n_win:], hr, hi))
        h_s, kvc, kvs, win, hr, hi = _sample_layer(
            l, h_s, lw, _s5_mats(*ssm, t_len), s_tables, alpha, cmp_t, slc_t, win_t[l],
            state_ssm_re[l], state_ssm_im[l], page_table, w_pad)
        outs_s.append((kvc.reshape(bs, t_len, 2, N_KV, HEAD_DIM), kvs.reshape(bs, t_len, 2, N_KV, HEAD_DIM),
                       win, hr, hi))
    stack = lambda outs, i: jnp.stack([o[i] for o in outs])
    return ((h_p, h_s)
            + tuple(_row_major(stack(outs_p, i)) for i in range(3)) + (stack(outs_p, 3), stack(outs_p, 4))
            + (stack(outs_s, 0), stack(outs_s, 1), _row_major(stack(outs_s, 2)), stack(outs_s, 3), stack(outs_s, 4)))
```
